```python
import math
import jax, jax.numpy as jnp
from jax import lax
import numpy as np

D_MODEL = 1024
BATCH = 8
SEQ = 2048
DEPTH = 1

CONV_WIDTH = D_MODEL // 2
CONV_K = 3
HEAD_DIM = 64
N_HEADS = (D_MODEL - CONV_WIDTH) // HEAD_DIM
N_KV_HEADS = 2
GQA_GROUP = N_HEADS // N_KV_HEADS
ATTN_WIDTH = N_HEADS * HEAD_DIM
KV_WIDTH = N_KV_HEADS * HEAD_DIM
WINDOW = 128
BLK = 128
NUM_BUCKETS = 32
MAX_DISTANCE = 128
MAX_EXACT = NUM_BUCKETS // 2
D_FF = 2816
FFN_K = 3
EPS = 1e-6
NEG_INF = -1e30
IN_WIDTH = 3 * CONV_WIDTH + ATTN_WIDTH + 2 * KV_WIDTH

kernel_name = "hybrid_shortconv_swa_sink_convffn"


def rms_norm(x, g):
    xf = x.astype(jnp.float32)
    y = xf * lax.rsqrt(jnp.mean(xf * xf, axis=-1, keepdims=True) + EPS)
    return (y * g.astype(jnp.float32)).astype(x.dtype)


def causal_dwconv(x, w):
    K = w.shape[0]
    S = x.shape[1]
    xp = jnp.pad(x, ((0, 0), (K - 1, 0), (0, 0)))
    y = xp[:, 0:S] * w[0]
    for k in range(1, K):
        y = y + xp[:, k:k + S] * w[k]
    return y


def band_offsets():
    q = jnp.arange(BLK, dtype=jnp.int32)[:, None]
    j = jnp.arange(2 * BLK, dtype=jnp.int32)[None, :]
    return q + BLK - j


def t5_band_bias(rel_table, d):
    n = jnp.maximum(d, 0)
    nf = jnp.maximum(n, 1).astype(jnp.float32)
    large = MAX_EXACT + (jnp.log(nf / MAX_EXACT) / math.log(MAX_DISTANCE / MAX_EXACT)
                         * (NUM_BUCKETS - MAX_EXACT)).astype(jnp.int32)
    large = jnp.minimum(large, NUM_BUCKETS - 1)
    bucket = jnp.where(n < MAX_EXACT, n, large)
    bias = rel_table[bucket].astype(jnp.float32)
    return bias.transpose(2, 0, 1).reshape(N_KV_HEADS, GQA_GROUP, BLK, 2 * BLK)


def band_blocks(t):
    Bn, S = t.shape[0], t.shape[1]
    nb = S // BLK
    tb = t.reshape(Bn, nb, BLK, N_KV_HEADS, HEAD_DIM)
    prev = jnp.pad(tb, ((0, 0), (1, 0), (0, 0), (0, 0), (0, 0)))[:, :-1]
    return jnp.concatenate([prev, tb], axis=2)


def sliding_window_attention(q, k, v, sinks, bias, d):
    Bn, S = q.shape[0], q.shape[1]
    nb = S // BLK
    qb = q.reshape(Bn, nb, BLK, N_KV_HEADS, GQA_GROUP, HEAD_DIM)
    kband = band_blocks(k)
    vband = band_blocks(v)
    scale = HEAD_DIM ** -0.5
    logits = jnp.einsum('bnqhgd,bnkhd->bnhgqk', qb, kband).astype(jnp.float32) * scale + bias
    within = (d >= 0) & (d < WINDOW)
    key_pos = (jnp.arange(nb, dtype=jnp.int32)[:, None, None] - 1) * BLK \
        + jnp.arange(2 * BLK, dtype=jnp.int32)[None, None, :]
    valid = within[None] & (key_pos >= 0)
    logits = jnp.where(valid[None, :, None, None], logits, NEG_INF)
    sink = sinks.astype(jnp.float32).reshape(1, 1, N_KV_HEADS, GQA_GROUP, 1, 1)
    m = jnp.maximum(jnp.max(logits, axis=-1, keepdims=True), sink)
    p = jnp.exp(logits - m)
    denom = jnp.sum(p, axis=-1, keepdims=True) + jnp.exp(sink - m)
    probs = (p / denom).astype(v.dtype)
    out = jnp.einsum('bnhgqk,bnkhd->bnqhgd', probs, vband)
    return out.reshape(Bn, S, ATTN_WIDTH)


def setup_inputs(seed: int = 0) -> dict:
    key = jax.random.key(seed)
    ks = jax.random.split(key, 17)
    f32 = jnp.float32

    def nrm(k, shape, scale):
        return jax.random.normal(k, shape, f32) * scale

    def gain(k, shape):
        return 1.0 + 0.02 * jax.random.normal(k, shape, f32)

    return {
        "x": jax.random.normal(ks[0], (BATCH, SEQ, D_MODEL), f32),
        "norm_mix_g": gain(ks[1], (DEPTH, D_MODEL)),
        "w_in": nrm(ks[2], (DEPTH, D_MODEL, IN_WIDTH), D_MODEL ** -0.5),
        "conv_w": nrm(ks[3], (DEPTH, CONV_K, CONV_WIDTH), CONV_K ** -0.5),
        "q_norm_g": gain(ks[4], (DEPTH, HEAD_DIM)),
        "k_norm_g": gain(ks[5], (DEPTH, HEAD_DIM)),
        "rel_bias_table": nrm(ks[6], (NUM_BUCKETS, N_HEADS), 0.5),
        "sinks": nrm(ks[7], (DEPTH, N_HEADS), 1.0),
        "out_norm_conv_g": gain(ks[8], (DEPTH, CONV_WIDTH)),
        "out_norm_attn_g": gain(ks[9], (DEPTH, ATTN_WIDTH)),
        "w_out": nrm(ks[10], (DEPTH, CONV_WIDTH + ATTN_WIDTH, D_MODEL), (CONV_WIDTH + ATTN_WIDTH) ** -0.5),
        "norm_ffn_g": gain(ks[11], (DEPTH, D_MODEL)),
        "w_up": nrm(ks[12], (DEPTH, D_MODEL, 2 * D_FF), D_MODEL ** -0.5),
        "ffn_conv_w": nrm(ks[13], (DEPTH, FFN_K, 2 * D_FF), FFN_K ** -0.5),
        "ffn_conv_b": nrm(ks[14], (DEPTH, 2 * D_FF), 0.02),
        "w_down": nrm(ks[15], (DEPTH, D_FF, D_MODEL), D_FF ** -0.5),
    }


def reference(x, norm_mix_g, w_in, conv_w, q_norm_g, k_norm_g, rel_bias_table, sinks,
              out_norm_conv_g, out_norm_attn_g, w_out, norm_ffn_g, w_up, ffn_conv_w,
              ffn_conv_b, w_down):
    Bn, S, _ = x.shape
    d = band_offsets()
    bias = t5_band_bias(rel_bias_table, d)
    h = x
    for l in range(DEPTH):
        u = rms_norm(h, norm_mix_g[l])
        proj = jnp.einsum('bsd,de->bse', u, w_in[l])
        c0 = 0
        gate_b = proj[..., c0:c0 + CONV_WIDTH]; c0 += CONV_WIDTH
        gate_c = proj[..., c0:c0 + CONV_WIDTH]; c0 += CONV_WIDTH
        hc = proj[..., c0:c0 + CONV_WIDTH]; c0 += CONV_WIDTH
        q = proj[..., c0:c0 + ATTN_WIDTH]; c0 += ATTN_WIDTH
        k = proj[..., c0:c0 + KV_WIDTH]; c0 += KV_WIDTH
        v = proj[..., c0:c0 + KV_WIDTH]

        y_conv = gate_b * causal_dwconv(gate_c * hc, conv_w[l])

        q = rms_norm(q.reshape(Bn, S, N_HEADS, HEAD_DIM), q_norm_g[l])
        k = rms_norm(k.reshape(Bn, S, N_KV_HEADS, HEAD_DIM), k_norm_g[l])
        v = v.reshape(Bn, S, N_KV_HEADS, HEAD_DIM)
        y_attn = sliding_window_attention(q, k, v, sinks[l], bias, d)

        y = jnp.concatenate([rms_norm(y_conv, out_norm_conv_g[l]),
                             rms_norm(y_attn, out_norm_attn_g[l])], axis=-1)
        h = h + jnp.einsum('bse,ed->bsd', y, w_out[l])

        u = rms_norm(h, norm_ffn_g[l])
        up = jnp.einsum('bsd,df->bsf', u, w_up[l])
        up = causal_dwconv(up, ffn_conv_w[l]) + ffn_conv_b[l]
        g, val = up[..., :D_FF], up[..., D_FF:]
        h = h + jnp.einsum('bsf,fd->bsd', jax.nn.silu(g) * val, w_down[l])
    return h
```

```python
import functools
import math

import jax
import jax.numpy as jnp
import numpy as np
from jax import lax
from jax.experimental import pallas as pl
from jax.experimental.pallas import tpu as pltpu

D_MODEL = 1024
CONV_WIDTH = 512
HEAD_DIM = 64
N_HEADS = 8
N_KV_HEADS = 2
GQA_GROUP = N_HEADS // N_KV_HEADS
ATTN_WIDTH = N_HEADS * HEAD_DIM
KV_WIDTH = N_KV_HEADS * HEAD_DIM
WINDOW = 128
BLK = 128
NUM_BUCKETS = 32
MAX_DISTANCE = 128
MAX_EXACT = NUM_BUCKETS // 2
D_FF = 2816
EPS = 1e-6
NEG_INF = -1e30
IN_WIDTH = 3 * CONV_WIDTH + ATTN_WIDTH + 2 * KV_WIDTH

ROW_TILE = 1024
FF_CHUNK = 256
N_FF_CHUNKS = D_FF // FF_CHUNK
HALO = 8
VMEM_LIMIT_BYTES = 60 * 1024 * 1024

_BF16 = jnp.bfloat16
_F32 = jnp.float32


def _dot(a, b):
    return jnp.dot(a, b, preferred_element_type=_F32)


def _rms(xf, g):
    return xf * lax.rsqrt(jnp.mean(xf * xf, axis=-1, keepdims=True) + EPS) * g


def _bucket_map():
    q = np.arange(BLK, dtype=np.int32)[:, None]
    j = np.arange(2 * BLK, dtype=np.int32)[None, :]
    d = q + BLK - j
    n = np.maximum(d, 0)
    nf = np.maximum(n, 1).astype(np.float32)
    large = MAX_EXACT + (np.log(nf / MAX_EXACT) / math.log(MAX_DISTANCE / MAX_EXACT)
                         * (NUM_BUCKETS - MAX_EXACT)).astype(np.int32)
    large = np.minimum(large, NUM_BUCKETS - 1)
    bucket = np.where(n < MAX_EXACT, n, large).astype(np.int32)
    within = (d >= 0) & (d < WINDOW)
    return np.where(within, bucket, -1).astype(np.int32)


def _head_mean_matrix():
    i = np.arange(ATTN_WIDTH)
    m = (i[:, None] // HEAD_DIM == i[None, :] // HEAD_DIM).astype(np.float32) / HEAD_DIM
    return jnp.asarray(m, dtype=_BF16)


def _const_spec(shape):
    return pl.BlockSpec(shape, lambda i: (0,) * len(shape), pipeline_mode=pl.Buffered(1))


def _mix_in_kernel(tiles_per_seq, x_ref, g_ref, w_ref, cw_ref, gq_ref, gk_ref, gc_ref, hm_ref,
                   yc_ref, q_ref, kv_ref, zbuf):
    ts = x_ref.shape[0]
    i = pl.program_id(0)

    @pl.when(i % tiles_per_seq == 0)
    def _():
        zbuf[0:HALO, :] = jnp.zeros((HALO, CONV_WIDTH), _F32)

    u = _rms(x_ref[...], g_ref[...]).astype(_BF16)

    gate_b = _dot(u, w_ref[:, 0:CONV_WIDTH])
    gate_c = _dot(u, w_ref[:, CONV_WIDTH:2 * CONV_WIDTH])
    hc = _dot(u, w_ref[:, 2 * CONV_WIDTH:3 * CONV_WIDTH])
    z = gate_c * hc
    zbuf[HALO:HALO + ts, :] = z
    conv = (cw_ref[0:1, :] * zbuf[HALO - 2:HALO - 2 + ts, :]
            + cw_ref[1:2, :] * zbuf[HALO - 1:HALO - 1 + ts, :]
            + cw_ref[2:3, :] * z)
    zbuf[0:HALO, :] = zbuf[ts:ts + HALO, :]
    yc_ref[...] = _rms(gate_b * conv, gc_ref[...]).astype(_BF16)

    c0 = 3 * CONV_WIDTH
    q = _dot(u, w_ref[:, c0:c0 + ATTN_WIDTH])
    msq = _dot((q * q).astype(_BF16), hm_ref[...])
    q_ref[...] = (q * lax.rsqrt(msq + EPS) * gq_ref[...] * (HEAD_DIM ** -0.5)).astype(_BF16)

    c1 = c0 + ATTN_WIDTH
    kv = _dot(u, w_ref[:, c1:c1 + 2 * KV_WIDTH])
    k = kv[:, 0:KV_WIDTH]
    msk = _dot((k * k).astype(_BF16), hm_ref[0:KV_WIDTH, 0:KV_WIDTH])
    kv_ref[:, 0:KV_WIDTH] = (k * lax.rsqrt(msk + EPS) * gk_ref[...]).astype(_BF16)
    kv_ref[:, KV_WIDTH:2 * KV_WIDTH] = kv[:, KV_WIDTH:2 * KV_WIDTH].astype(_BF16)


def _mix_in(x2, g_mix, w_in, conv_w, gq, gk, g_conv, hm, tiles_per_seq):
    n = x2.shape[0]
    ts = ROW_TILE
    return pl.pallas_call(
        functools.partial(_mix_in_kernel, tiles_per_seq),
        grid=(n // ts,),
        in_specs=[
            pl.BlockSpec((ts, D_MODEL), lambda i: (i, 0)),
            _const_spec((1, D_MODEL)),
            _const_spec((D_MODEL, IN_WIDTH)),
            _const_spec((3, CONV_WIDTH)),
            _const_spec((1, ATTN_WIDTH)),
            _const_spec((1, KV_WIDTH)),
            _const_spec((1, CONV_WIDTH)),
            _const_spec((ATTN_WIDTH, ATTN_WIDTH)),
        ],
        out_specs=[
            pl.BlockSpec((ts, CONV_WIDTH), lambda i: (i, 0)),
            pl.BlockSpec((ts, ATTN_WIDTH), lambda i: (i, 0)),
            pl.BlockSpec((ts, 2 * KV_WIDTH), lambda i: (i, 0)),
        ],
        out_shape=[
            jax.ShapeDtypeStruct((n, CONV_WIDTH), _BF16),
            jax.ShapeDtypeStruct((n, ATTN_WIDTH), _BF16),
            jax.ShapeDtypeStruct((n, 2 * KV_WIDTH), _BF16),
        ],
        scratch_shapes=[pltpu.VMEM((ts + HALO, CONV_WIDTH), _F32)],
        compiler_params=pltpu.CompilerParams(
            dimension_semantics=("arbitrary",), vmem_limit_bytes=VMEM_LIMIT_BYTES),
        name="mix_in",
    )(x2, g_mix, w_in, conv_w, gq, gk, g_conv, hm)


def _attn_out_kernel(tiles_per_seq, tab_ref, sink_ref, q_ref, kv_ref, kvp_ref, yc_ref, x_ref, bkt_ref,
                     ga_ref, wo_ref, o_ref, bias_ref, kvx_ref, ya_ref):
    ts = q_ref.shape[0]
    nblk = ts // BLK
    i = pl.program_id(0)

    @pl.when(i == 0)
    def _():
        bkt = bkt_ref[...]
        col = lax.broadcasted_iota(jnp.int32, (BLK, 2 * BLK), 1)

        def head_body(h, carry):
            b = jnp.full((BLK, 2 * BLK), NEG_INF, _F32)
            for t in range(NUM_BUCKETS):
                b = jnp.where(bkt == t, tab_ref[t, h], b)
            bias_ref[0, h] = b
            bias_ref[1, h] = jnp.where(col >= BLK, b, NEG_INF)
            return carry

        lax.fori_loop(0, N_HEADS, head_body, 0)

    kvx_ref[0:BLK, :] = kvp_ref[...]
    kvx_ref[BLK:BLK + ts, :] = kv_ref[...]
    seq_start = (i % tiles_per_seq == 0)

    def block_body(j, carry):
        r0 = pl.multiple_of(j * BLK, BLK)
        first = jnp.logical_and(seq_start, j == 0).astype(jnp.int32)
        qb = q_ref[pl.ds(r0, BLK), :]
        band = kvx_ref[pl.ds(r0, 2 * BLK), :]
        for g in range(N_KV_HEADS):
            kb = band[:, g * HEAD_DIM:(g + 1) * HEAD_DIM]
            vb = band[:, KV_WIDTH + g * HEAD_DIM:KV_WIDTH + (g + 1) * HEAD_DIM]
            qg = jnp.concatenate(
                [qb[:, (g * GQA_GROUP + hh) * HEAD_DIM:(g * GQA_GROUP + hh + 1) * HEAD_DIM]
                 for hh in range(GQA_GROUP)], axis=0)
            logits = lax.dot_general(qg, kb, (((1,), (1,)), ((), ())),
                                     preferred_element_type=_F32)
            outs = []
            for hh in range(GQA_GROUP):
                h = g * GQA_GROUP + hh
                lg = logits[hh * BLK:(hh + 1) * BLK, :] + bias_ref[first, h]
                sink = sink_ref[h]
                m = jnp.maximum(jnp.max(lg, axis=-1, keepdims=True), sink)
                p = jnp.exp(lg - m)
                denom = jnp.sum(p, axis=-1, keepdims=True) + jnp.exp(sink - m)
                o = _dot(p.astype(_BF16), vb)
                outs.append(o / denom)
            for hh in range(GQA_GROUP):
                h = g * GQA_GROUP + hh
                ya_ref[pl.ds(r0, BLK), h * HEAD_DIM:(h + 1) * HEAD_DIM] = outs[hh]
        return carry

    lax.fori_loop(0, nblk, block_body, 0)

    ya = _rms(ya_ref[...], ga_ref[...]).astype(_BF16)
    o_ref[...] = (x_ref[...] + _dot(yc_ref[...], wo_ref[0:CONV_WIDTH, :])
                  + _dot(ya, wo_ref[CONV_WIDTH:CONV_WIDTH + ATTN_WIDTH, :]))


def _attn_out(tab, sinks, q, kv, yc, x2, bkt, g_attn, w_out, tiles_per_seq):
    n = x2.shape[0]
    ts = ROW_TILE
    bpt = ts // BLK
    smem = pl.BlockSpec(memory_space=pltpu.SMEM)
    return pl.pallas_call(
        functools.partial(_attn_out_kernel, tiles_per_seq),
        grid=(n // ts,),
        in_specs=[
            smem,
            smem,
            pl.BlockSpec((ts, ATTN_WIDTH), lambda i: (i, 0)),
            pl.BlockSpec((ts, 2 * KV_WIDTH), lambda i: (i, 0)),
            pl.BlockSpec((BLK, 2 * KV_WIDTH), lambda i: (jnp.maximum(i * bpt - 1, 0), 0)),
            pl.BlockSpec((ts, CONV_WIDTH), lambda i: (i, 0)),
            pl.BlockSpec((ts, D_MODEL), lambda i: (i, 0)),
            _const_spec((BLK, 2 * BLK)),
            _const_spec((1, ATTN_WIDTH)),
            _const_spec((CONV_WIDTH + ATTN_WIDTH, D_MODEL)),
        ],
        out_specs=pl.BlockSpec((ts, D_MODEL), lambda i: (i, 0)),
        out_shape=jax.ShapeDtypeStruct((n, D_MODEL), _F32),
        scratch_shapes=[
            pltpu.VMEM((2, N_HEADS, BLK, 2 * BLK), _F32),
            pltpu.VMEM((ts + BLK, 2 * KV_WIDTH), _BF16),
            pltpu.VMEM((ts, ATTN_WIDTH), _F32),
        ],
        compiler_params=pltpu.CompilerParams(
            dimension_semantics=("arbitrary",), vmem_limit_bytes=VMEM_LIMIT_BYTES),
        name="attn_out",
    )(tab, sinks, q, kv, kv, yc, x2, bkt, g_attn, w_out)


def _ffn_kernel(tiles_per_seq, h_ref, g_ref, wu_ref, cw_ref, cb_ref, wd_ref, o_ref,
                u_ref, ubuf, carry_ref, act_ref):
    ts = h_ref.shape[0]
    i = pl.program_id(0)
    cw2 = 2 * FF_CHUNK

    @pl.when(i % tiles_per_seq == 0)
    def _():
        carry_ref[...] = jnp.zeros(carry_ref.shape, _F32)

    u_ref[...] = _rms(h_ref[...], g_ref[...]).astype(_BF16)

    for c in range(N_FF_CHUNKS):
        buf = ubuf.at[c % 2]
        cols = slice(c * cw2, (c + 1) * cw2)
        up = _dot(u_ref[...], wu_ref[:, cols])
        buf[0:HALO, :] = carry_ref[c]
        buf[HALO:HALO + ts, :] = up
        a = (cw_ref[0:1, cols] * buf[HALO - 2:HALO - 2 + ts, :]
             + cw_ref[1:2, cols] * buf[HALO - 1:HALO - 1 + ts, :]
             + cw_ref[2:3, cols] * up + cb_ref[:, cols])
        carry_ref[c] = buf[ts:ts + HALO, :]
        gate = a[:, 0:FF_CHUNK]
        val = a[:, FF_CHUNK:cw2]
        act_ref[:, c * FF_CHUNK:(c + 1) * FF_CHUNK] = (gate / (1.0 + jnp.exp(-gate)) * val).astype(_BF16)

    o_ref[...] = h_ref[...] + _dot(act_ref[...], wd_ref[...])


def _ffn(h, g_ffn, w_up, conv_w, conv_b, w_down, tiles_per_seq):
    n = h.shape[0]
    ts = ROW_TILE
    return pl.pallas_call(
        functools.partial(_ffn_kernel, tiles_per_seq),
        grid=(n // ts,),
        in_specs=[
            pl.BlockSpec((ts, D_MODEL), lambda i: (i, 0)),
            _const_spec((1, D_MODEL)),
            _const_spec((D_MODEL, 2 * D_FF)),
            _const_spec((3, 2 * D_FF)),
            _const_spec((1, 2 * D_FF)),
            _const_spec((D_FF, D_MODEL)),
        ],
        out_specs=pl.BlockSpec((ts, D_MODEL), lambda i: (i, 0)),
        out_shape=jax.ShapeDtypeStruct((n, D_MODEL), _F32),
        scratch_shapes=[
            pltpu.VMEM((ts, D_MODEL), _BF16),
            pltpu.VMEM((2, ts + HALO, 2 * FF_CHUNK), _F32),
            pltpu.VMEM((N_FF_CHUNKS, HALO, 2 * FF_CHUNK), _F32),
            pltpu.VMEM((ts, D_FF), _BF16),
        ],
        compiler_params=pltpu.CompilerParams(
            dimension_semantics=("arbitrary",), vmem_limit_bytes=VMEM_LIMIT_BYTES),
        name="ffn",
    )(h, g_ffn, w_up, conv_w, conv_b, w_down)


def _interleave_ff(a):
    lead = a.shape[:-1]
    a = a.reshape(lead + (2, N_FF_CHUNKS, FF_CHUNK))
    a = jnp.swapaxes(a, -3, -2)
    return a.reshape(lead + (2 * D_FF,))


def kernel(x, norm_mix_g, w_in, conv_w, q_norm_g, k_norm_g, rel_bias_table, sinks, out_norm_conv_g,
           out_norm_attn_g, w_out, norm_ffn_g, w_up, ffn_conv_w, ffn_conv_b, w_down):
    bsz, seq, d = x.shape
    assert d == D_MODEL and seq % ROW_TILE == 0 and norm_mix_g.shape[0] == 1
    tiles_per_seq = seq // ROW_TILE
    n = bsz * seq
    x2 = x.reshape(n, d)
    bkt = jnp.asarray(_bucket_map())
    hm = _head_mean_matrix()

    yc, q, kv = _mix_in(
        x2, norm_mix_g[0][None, :], w_in[0].astype(_BF16), conv_w[0],
        jnp.tile(q_norm_g[0], N_HEADS)[None, :], jnp.tile(k_norm_g[0], N_KV_HEADS)[None, :],
        out_norm_conv_g[0][None, :], hm, tiles_per_seq)
    h = _attn_out(rel_bias_table, sinks[0], q, kv, yc, x2, bkt, out_norm_attn_g[0][None, :],
                  w_out[0].astype(_BF16), tiles_per_seq)
    out = _ffn(h, norm_ffn_g[0][None, :], _interleave_ff(w_up[0]).astype(_BF16),
               _interleave_ff(ffn_conv_w[0]), _interleave_ff(ffn_conv_b[0])[None, :],
               w_down[0].astype(_BF16), tiles_per_seq)
    return out.reshape(bsz, seq, d)
```

```python
import functools
import math

import jax
import jax.numpy as jnp
import numpy as np
from jax import lax
from jax.experimental import pallas as pl
from jax.experimental.pallas import tpu as pltpu

D_MODEL = 1024
CONV_WIDTH = 512
HEAD_DIM = 64
N_HEADS = 8
N_KV_HEADS = 2
GQA_GROUP = N_HEADS // N_KV_HEADS
ATTN_WIDTH = N_HEADS * HEAD_DIM
KV_WIDTH = N_KV_HEADS * HEAD_DIM
WINDOW = 128
BLK = 128
NUM_BUCKETS = 32
MAX_DISTANCE = 128
MAX_EXACT = NUM_BUCKETS // 2
D_FF = 2816
EPS = 1e-6
NEG_INF = -1e30
IN_WIDTH = 3 * CONV_WIDTH + ATTN_WIDTH + 2 * KV_WIDTH

LANES = 128
PAIR = 2 * HEAD_DIM
assert PAIR == LANES and KV_WIDTH == LANES and GQA_GROUP == 4
KX_WIDTH = 2 * N_KV_HEADS * LANES
VX_WIDTH = 2 * KX_WIDTH

ROW_TILE = 1024
FF_CHUNK = 256
N_FF_CHUNKS = D_FF // FF_CHUNK
HALO = 8
VMEM_LIMIT_BYTES = 60 * 1024 * 1024

_BF16 = jnp.bfloat16
_F32 = jnp.float32


def _dot(a, b):
    return jnp.dot(a, b, preferred_element_type=_F32)


def _dot_nt(a, b):
    return lax.dot_general(a, b, (((1,), (1,)), ((), ())), preferred_element_type=_F32)


def _rms(xf, g):
    return xf * lax.rsqrt(jnp.mean(xf * xf, axis=-1, keepdims=True) + EPS) * g


def _bucket_map():
    q = np.arange(BLK, dtype=np.int32)[:, None]
    j = np.arange(2 * BLK, dtype=np.int32)[None, :]
    d = q + BLK - j
    n = np.maximum(d, 0)
    nf = np.maximum(n, 1).astype(np.float32)
    large = MAX_EXACT + (np.log(nf / MAX_EXACT) / math.log(MAX_DISTANCE / MAX_EXACT)
                         * (NUM_BUCKETS - MAX_EXACT)).astype(np.int32)
    large = np.minimum(large, NUM_BUCKETS - 1)
    bucket = np.where(n < MAX_EXACT, n, large).astype(np.int32)
    within = (d >= 0) & (d < WINDOW)
    return np.where(within, bucket, -1).astype(np.int32)


def _head_mean_matrix():
    i = np.arange(ATTN_WIDTH)
    m = (i[:, None] // HEAD_DIM == i[None, :] // HEAD_DIM).astype(np.float32) / HEAD_DIM
    return jnp.asarray(m, dtype=_BF16)


def _const_spec(shape):
    return pl.BlockSpec(shape, lambda i: (0,) * len(shape), pipeline_mode=pl.Buffered(1))


def _lo_hi_layout(a, lo):
    ar = pltpu.roll(a, HEAD_DIM, axis=1)
    zero = jnp.zeros_like(a)
    return jnp.concatenate([jnp.where(lo, a, zero), jnp.where(lo, zero, ar),
                            jnp.where(lo, ar, zero), jnp.where(lo, zero, a)], axis=1)


def _mix_in_kernel(tiles_per_seq, x_ref, g_ref, w_ref, cw_ref, gq_ref, gk_ref, gc_ref, hm_ref,
                   yc_ref, q_ref, kx_ref, vx_ref, zbuf):
    ts = x_ref.shape[0]
    i = pl.program_id(0)

    @pl.when(i % tiles_per_seq == 0)
    def _():
        zbuf[0:HALO, :] = jnp.zeros((HALO, CONV_WIDTH), _F32)

    u = _rms(x_ref[...], g_ref[...]).astype(_BF16)

    gate_b = _dot(u, w_ref[:, 0:CONV_WIDTH])
    gate_c = _dot(u, w_ref[:, CONV_WIDTH:2 * CONV_WIDTH])
    hc = _dot(u, w_ref[:, 2 * CONV_WIDTH:3 * CONV_WIDTH])
    z = gate_c * hc
    zbuf[HALO:HALO + ts, :] = z
    conv = (cw_ref[0:1, :] * zbuf[HALO - 2:HALO - 2 + ts, :]
            + cw_ref[1:2, :] * zbuf[HALO - 1:HALO - 1 + ts, :]
            + cw_ref[2:3, :] * z)
    zbuf[0:HALO, :] = zbuf[ts:ts + HALO, :]
    yc_ref[...] = _rms(gate_b * conv, gc_ref[...]).astype(_BF16)

    c0 = 3 * CONV_WIDTH
    q = _dot(u, w_ref[:, c0:c0 + ATTN_WIDTH])
    msq = _dot((q * q).astype(_BF16), hm_ref[...])
    q_ref[...] = (q * lax.rsqrt(msq + EPS) * gq_ref[...] * (HEAD_DIM ** -0.5)).astype(_BF16)

    c1 = c0 + ATTN_WIDTH
    kv = _dot(u, w_ref[:, c1:c1 + 2 * KV_WIDTH])
    k = kv[:, 0:KV_WIDTH]
    msk = _dot((k * k).astype(_BF16), hm_ref[0:KV_WIDTH, 0:KV_WIDTH])
    kn = k * lax.rsqrt(msk + EPS) * gk_ref[...]
    lo = lax.broadcasted_iota(jnp.int32, (ts, LANES), 1) < HEAD_DIM
    kx_ref[...] = _lo_hi_layout(kn, lo).astype(_BF16)
    vx_ref[...] = _lo_hi_layout(kv[:, KV_WIDTH:2 * KV_WIDTH], lo).astype(_BF16)


def _mix_in(x2, g_mix, w_in, conv_w, gq, gk, g_conv, hm, tiles_per_seq):
    n = x2.shape[0]
    ts = ROW_TILE
    row_spec = lambda w: pl.BlockSpec((ts, w), lambda i: (i, 0))
    return pl.pallas_call(
        functools.partial(_mix_in_kernel, tiles_per_seq),
        grid=(n // ts,),
        in_specs=[
            row_spec(D_MODEL),
            _const_spec((1, D_MODEL)),
            _const_spec((D_MODEL, IN_WIDTH)),
            _const_spec((3, CONV_WIDTH)),
            _const_spec((1, ATTN_WIDTH)),
            _const_spec((1, KV_WIDTH)),
            _const_spec((1, CONV_WIDTH)),
            _const_spec((ATTN_WIDTH, ATTN_WIDTH)),
        ],
        out_specs=[row_spec(CONV_WIDTH), row_spec(ATTN_WIDTH), row_spec(KX_WIDTH), row_spec(KX_WIDTH)],
        out_shape=[
            jax.ShapeDtypeStruct((n, CONV_WIDTH), _BF16),
            jax.ShapeDtypeStruct((n, ATTN_WIDTH), _BF16),
            jax.ShapeDtypeStruct((n, KX_WIDTH), _BF16),
            jax.ShapeDtypeStruct((n, KX_WIDTH), _BF16),
        ],
        scratch_shapes=[pltpu.VMEM((ts + HALO, CONV_WIDTH), _F32)],
        compiler_params=pltpu.CompilerParams(
            dimension_semantics=("arbitrary",), vmem_limit_bytes=VMEM_LIMIT_BYTES),
        name="mix_in",
    )(x2, g_mix, w_in, conv_w, gq, gk, g_conv, hm)


def _attn_out_kernel(tiles_per_seq, tab_ref, sink_ref, q_ref, kx_ref, kxp_ref, vx_ref, vxp_ref, yc_ref, x_ref,
                     bkt_ref, ga_ref, wo_ref, o_ref, bias_ref, kxx_ref, vxx_ref, ya_ref, p_ref, st_ref):
    ts = q_ref.shape[0]
    nblk = ts // BLK
    i = pl.program_id(0)

    @pl.when(i == 0)
    def _():
        bkt = bkt_ref[...]
        col = lax.broadcasted_iota(jnp.int32, (BLK, 2 * BLK), 1)

        def head_body(h, carry):
            b = jnp.full((BLK, 2 * BLK), NEG_INF, _F32)
            for t in range(NUM_BUCKETS):
                b = jnp.where(bkt == t, tab_ref[t, h], b)
            bias_ref[0, h] = b
            bias_ref[1, h] = jnp.where(col >= BLK, b, NEG_INF)
            return carry

        lax.fori_loop(0, N_HEADS, head_body, 0)

        lane = lax.broadcasted_iota(jnp.int32, (ts + BLK, LANES), 1)
        ones_lo = jnp.where(lane < HEAD_DIM, 1.0, 0.0).astype(_BF16)
        ones_hi = jnp.where(lane < HEAD_DIM, 0.0, 1.0).astype(_BF16)
        for g in range(N_KV_HEADS):
            base = g * 4 * LANES
            vxx_ref[:, base + LANES:base + 2 * LANES] = ones_lo
            vxx_ref[:, base + 3 * LANES:base + 4 * LANES] = ones_hi

    kxx_ref[0:BLK, :] = kxp_ref[...]
    kxx_ref[BLK:BLK + ts, :] = kx_ref[...]
    for s in range(2 * N_KV_HEADS):
        vxx_ref[0:BLK, 2 * s * LANES:(2 * s + 1) * LANES] = vxp_ref[:, s * LANES:(s + 1) * LANES]
        vxx_ref[BLK:BLK + ts, 2 * s * LANES:(2 * s + 1) * LANES] = vx_ref[:, s * LANES:(s + 1) * LANES]
    seq_start = (i % tiles_per_seq == 0)
    lane_lo = lax.broadcasted_iota(jnp.int32, (BLK, LANES), 1) < HEAD_DIM

    def scores_stage(j):
        r0 = j * BLK
        slot = j % 2
        first = seq_start.astype(jnp.int32) if j == 0 else 0
        for g in range(N_KV_HEADS):
            qq = q_ref[pl.ds(r0, BLK), g * 2 * LANES:(g + 1) * 2 * LANES]
            lhs = jnp.concatenate([qq[:, 0:LANES], qq[:, LANES:2 * LANES]], axis=0)
            sink_terms = [[None, None], [None, None]]
            for par in range(2):
                kb = kxx_ref[pl.ds(r0, 2 * BLK), (2 * g + par) * LANES:(2 * g + par + 1) * LANES]
                logits = _dot_nt(lhs, kb)
                for pair in range(2):
                    h = g * GQA_GROUP + 2 * pair + par
                    lg = logits[pair * BLK:(pair + 1) * BLK, :] + bias_ref[first, h]
                    sink = sink_ref[h]
                    m = jnp.maximum(jnp.max(lg, axis=-1, keepdims=True), sink)
                    p_ref[slot, 2 * g + par, pair * BLK:(pair + 1) * BLK, :] = jnp.exp(lg - m).astype(_BF16)
                    sink_terms[pair][par] = jnp.exp(sink - m)
            for pair in range(2):
                st_ref[slot, 2 * g + pair] = jnp.where(lane_lo, sink_terms[pair][0], sink_terms[pair][1])

    def values_stage(j):
        r0 = j * BLK
        slot = j % 2
        for g in range(N_KV_HEADS):
            out = None
            for par in range(2):
                c = (4 * g + 2 * par) * LANES
                o = _dot(p_ref[slot, 2 * g + par],
                         vxx_ref[pl.ds(r0, 2 * BLK), c:c + 2 * LANES])
                out = o if out is None else out + o
            for pair in range(2):
                blk = out[pair * BLK:(pair + 1) * BLK, :]
                den = blk[:, LANES:2 * LANES] + st_ref[slot, 2 * g + pair]
                c = (2 * g + pair) * LANES
                ya_ref[pl.ds(r0, BLK), c:c + LANES] = blk[:, 0:LANES] / den

    for j in range(nblk):
        scores_stage(j)
        values_stage(j)

    ya = _rms(ya_ref[...], ga_ref[...]).astype(_BF16)
    o_ref[...] = (x_ref[...] + _dot(yc_ref[...], wo_ref[0:CONV_WIDTH, :])
                  + _dot(ya, wo_ref[CONV_WIDTH:CONV_WIDTH + ATTN_WIDTH, :]))


def _attn_out(tab, sinks, q, kx, vx, yc, x2, bkt, g_attn, w_out, tiles_per_seq):
    n = x2.shape[0]
    ts = ROW_TILE
    bpt = ts // BLK
    smem = pl.BlockSpec(memory_space=pltpu.SMEM)
    row_spec = lambda w: pl.BlockSpec((ts, w), lambda i: (i, 0))
    prev_spec = pl.BlockSpec((BLK, KX_WIDTH), lambda i: (jnp.maximum(i * bpt - 1, 0), 0))
    return pl.pallas_call(
        functools.partial(_attn_out_kernel, tiles_per_seq),
        grid=(n // ts,),
        in_specs=[
            smem,
            smem,
            row_spec(ATTN_WIDTH),
            row_spec(KX_WIDTH),
            prev_spec,
            row_spec(KX_WIDTH),
            prev_spec,
            row_spec(CONV_WIDTH),
            row_spec(D_MODEL),
            _const_spec((BLK, 2 * BLK)),
            _const_spec((1, ATTN_WIDTH)),
            _const_spec((CONV_WIDTH + ATTN_WIDTH, D_MODEL)),
        ],
        out_specs=row_spec(D_MODEL),
        out_shape=jax.ShapeDtypeStruct((n, D_MODEL), _F32),
        scratch_shapes=[
            pltpu.VMEM((2, N_HEADS, BLK, 2 * BLK), _F32),
            pltpu.VMEM((ts + BLK, KX_WIDTH), _BF16),
            pltpu.VMEM((ts + BLK, VX_WIDTH), _BF16),
            pltpu.VMEM((ts, ATTN_WIDTH), _F32),
            pltpu.VMEM((2, 2 * N_KV_HEADS, 2 * BLK, 2 * BLK), _BF16),
            pltpu.VMEM((2, 2 * N_KV_HEADS, BLK, LANES), _F32),
        ],
        compiler_params=pltpu.CompilerParams(
            dimension_semantics=("arbitrary",), vmem_limit_bytes=VMEM_LIMIT_BYTES),
        name="attn_out",
    )(tab, sinks, q, kx, kx, vx, vx, yc, x2, bkt, g_attn, w_out)


def _ffn_kernel(tiles_per_seq, h_ref, g_ref, wu_ref, cw_ref, cb_ref, wd_ref, o_ref,
                u_ref, ubuf, carry_ref, act_ref):
    ts = h_ref.shape[0]
    i = pl.program_id(0)

    @pl.when(i % tiles_per_seq == 0)
    def _():
        carry_ref[...] = jnp.zeros(carry_ref.shape, _F32)

    u_ref[...] = _rms(h_ref[...], g_ref[...]).astype(_BF16)

    def conv_half(c, part):
        slot = 2 * c + part
        buf = ubuf.at[slot % ubuf.shape[0]]
        cols = slice(part * D_FF + c * FF_CHUNK, part * D_FF + (c + 1) * FF_CHUNK)
        up = _dot(u_ref[...], wu_ref[:, cols])
        buf[0:HALO, :] = carry_ref[slot]
        buf[HALO:HALO + ts, :] = up
        a = (cw_ref[0:1, cols] * buf[HALO - 2:HALO - 2 + ts, :]
             + cw_ref[1:2, cols] * buf[HALO - 1:HALO - 1 + ts, :]
             + cw_ref[2:3, cols] * up + cb_ref[:, cols])
        carry_ref[slot] = buf[ts:ts + HALO, :]
        return a

    for c in range(N_FF_CHUNKS):
        gate = conv_half(c, 0)
        val = conv_half(c, 1)
        act_ref[:, c * FF_CHUNK:(c + 1) * FF_CHUNK] = (gate / (1.0 + jnp.exp(-gate)) * val).astype(_BF16)

    o_ref[...] = h_ref[...] + _dot(act_ref[...], wd_ref[...])


def _ffn(h, g_ffn, w_up, conv_w, conv_b, w_down, tiles_per_seq):
    n = h.shape[0]
    ts = ROW_TILE
    return pl.pallas_call(
        functools.partial(_ffn_kernel, tiles_per_seq),
        grid=(n // ts,),
        in_specs=[
            pl.BlockSpec((ts, D_MODEL), lambda i: (i, 0)),
            _const_spec((1, D_MODEL)),
            _const_spec((D_MODEL, 2 * D_FF)),
            _const_spec((3, 2 * D_FF)),
            _const_spec((1, 2 * D_FF)),
            _const_spec((D_FF, D_MODEL)),
        ],
        out_specs=pl.BlockSpec((ts, D_MODEL), lambda i: (i, 0)),
        out_shape=jax.ShapeDtypeStruct((n, D_MODEL), _F32),
        scratch_shapes=[
            pltpu.VMEM((ts, D_MODEL), _BF16),
            pltpu.VMEM((4, ts + HALO, FF_CHUNK), _F32),
            pltpu.VMEM((2 * N_FF_CHUNKS, HALO, FF_CHUNK), _F32),
            pltpu.VMEM((ts, D_FF), _BF16),
        ],
        compiler_params=pltpu.CompilerParams(
            dimension_semantics=("arbitrary",), vmem_limit_bytes=VMEM_LIMIT_BYTES),
        name="ffn",
    )(h, g_ffn, w_up, conv_w, conv_b, w_down)


def kernel(x, norm_mix_g, w_in, conv_w, q_norm_g, k_norm_g, rel_bias_table, sinks, out_norm_conv_g,
           out_norm_attn_g, w_out, norm_ffn_g, w_up, ffn_conv_w, ffn_conv_b, w_down):
    bsz, seq, d = x.shape
    assert d == D_MODEL and seq % ROW_TILE == 0 and norm_mix_g.shape[0] == 1
    tiles_per_seq = seq // ROW_TILE
    n = bsz * seq
    x2 = x.reshape(n, d)
    bkt = jnp.asarray(_bucket_map())
    hm = _head_mean_matrix()

    yc, q, kx, vx = _mix_in(
        x2, norm_mix_g[0][None, :], w_in[0].astype(_BF16), conv_w[0],
        jnp.tile(q_norm_g[0], N_HEADS)[None, :], jnp.tile(k_norm_g[0], N_KV_HEADS)[None, :],
        out_norm_conv_g[0][None, :], hm, tiles_per_seq)
    h = _attn_out(rel_bias_table, sinks[0], q, kx, vx, yc, x2, bkt, out_norm_attn_g[0][None, :],
                  w_out[0].astype(_BF16), tiles_per_seq)
    out = _ffn(h, norm_ffn_g[0][None, :], w_up[0].astype(_BF16), ffn_conv_w[0], ffn_conv_b[0][None, :],
               w_down[0].astype(_BF16), tiles_per_seq)
    return out.reshape(bsz, seq, d)
```

```python
import functools
import math

import jax
import jax.numpy as jnp
import numpy as np
from jax import lax
from jax.experimental import pallas as pl
from jax.experimental.pallas import tpu as pltpu

D_MODEL = 1024
CONV_WIDTH = 512
HEAD_DIM = 64
N_HEADS = 8
N_KV_HEADS = 2
GQA_GROUP = N_HEADS // N_KV_HEADS
ATTN_WIDTH = N_HEADS * HEAD_DIM
KV_WIDTH = N_KV_HEADS * HEAD_DIM
WINDOW = 128
BLK = 128
NUM_BUCKETS = 32
MAX_DISTANCE = 128
MAX_EXACT = NUM_BUCKETS // 2
D_FF = 2816
EPS = 1e-6
NEG_INF = -1e30
IN_WIDTH = 3 * CONV_WIDTH + ATTN_WIDTH + 2 * KV_WIDTH

LANES = 128
SUBLANES = 8
PAIR = 2 * HEAD_DIM
assert PAIR == LANES and KV_WIDTH == LANES and GQA_GROUP == 4
KX_WIDTH = 2 * N_KV_HEADS * LANES
VX_WIDTH = 2 * KX_WIDTH

ROW_TILE = 1024
FF_CHUNK = 256
N_FF_CHUNKS = D_FF // FF_CHUNK
HALO = 8
VMEM_LIMIT_BYTES = 60 * 1024 * 1024

_BF16 = jnp.bfloat16
_F32 = jnp.float32


def _dot(a, b):
    return jnp.dot(a, b, preferred_element_type=_F32)


def _dot_nt(a, b):
    return lax.dot_general(a, b, (((1,), (1,)), ((), ())), preferred_element_type=_F32)


def _rms(xf, g):
    return xf * lax.rsqrt(jnp.mean(xf * xf, axis=-1, keepdims=True) + EPS) * g


def _bucket_map():
    q = np.arange(BLK, dtype=np.int32)[:, None]
    j = np.arange(2 * BLK, dtype=np.int32)[None, :]
    d = q + BLK - j
    n = np.maximum(d, 0)
    nf = np.maximum(n, 1).astype(np.float32)
    large = MAX_EXACT + (np.log(nf / MAX_EXACT) / math.log(MAX_DISTANCE / MAX_EXACT)
                         * (NUM_BUCKETS - MAX_EXACT)).astype(np.int32)
    large = np.minimum(large, NUM_BUCKETS - 1)
    bucket = np.where(n < MAX_EXACT, n, large).astype(np.int32)
    within = (d >= 0) & (d < WINDOW)
    return np.where(within, bucket, -1).astype(np.int32)


def _head_mean_matrix():
    i = np.arange(ATTN_WIDTH)
    m = (i[:, None] // HEAD_DIM == i[None, :] // HEAD_DIM).astype(np.float32) / HEAD_DIM
    return jnp.asarray(m, dtype=_BF16)


def _const_spec(shape):
    return pl.BlockSpec(shape, lambda i: (0,) * len(shape), pipeline_mode=pl.Buffered(1))


def _lo_hi_layout(a, lo):
    ar = pltpu.roll(a, HEAD_DIM, axis=1)
    zero = jnp.zeros_like(a)
    return jnp.concatenate([jnp.where(lo, a, zero), jnp.where(lo, zero, ar),
                            jnp.where(lo, ar, zero), jnp.where(lo, zero, a)], axis=1)


def _mix_in_kernel(tiles_per_seq, x_ref, g_ref, w_ref, cw_ref, gq_ref, gk_ref, gc_ref, hm_ref,
                   yc_ref, q_ref, kx_ref, vx_ref, zbuf):
    ts = x_ref.shape[0]
    i = pl.program_id(0)

    @pl.when(i % tiles_per_seq == 0)
    def _():
        zbuf[0:HALO, :] = jnp.zeros((HALO, CONV_WIDTH), _F32)

    u = _rms(x_ref[...], g_ref[...]).astype(_BF16)

    gate_b = _dot(u, w_ref[:, 0:CONV_WIDTH])
    gate_c = _dot(u, w_ref[:, CONV_WIDTH:2 * CONV_WIDTH])
    hc = _dot(u, w_ref[:, 2 * CONV_WIDTH:3 * CONV_WIDTH])
    z = gate_c * hc
    zbuf[HALO:HALO + ts, :] = z
    conv = (cw_ref[0:1, :] * zbuf[HALO - 2:HALO - 2 + ts, :]
            + cw_ref[1:2, :] * zbuf[HALO - 1:HALO - 1 + ts, :]
            + cw_ref[2:3, :] * z)
    zbuf[0:HALO, :] = zbuf[ts:ts + HALO, :]
    yc_ref[...] = _rms(gate_b * conv, gc_ref[...]).astype(_BF16)

    c0 = 3 * CONV_WIDTH
    q = _dot(u, w_ref[:, c0:c0 + ATTN_WIDTH])
    msq = _dot((q * q).astype(_BF16), hm_ref[...])
    q_ref[...] = (q * lax.rsqrt(msq + EPS) * gq_ref[...] * (HEAD_DIM ** -0.5)).astype(_BF16)

    c1 = c0 + ATTN_WIDTH
    kv = _dot(u, w_ref[:, c1:c1 + 2 * KV_WIDTH])
    k = kv[:, 0:KV_WIDTH]
    msk = _dot((k * k).astype(_BF16), hm_ref[0:KV_WIDTH, 0:KV_WIDTH])
    kn = k * lax.rsqrt(msk + EPS) * gk_ref[...]
    lo = lax.broadcasted_iota(jnp.int32, (ts, LANES), 1) < HEAD_DIM
    kx_ref[...] = _lo_hi_layout(kn, lo).astype(_BF16)
    vx_ref[...] = _lo_hi_layout(kv[:, KV_WIDTH:2 * KV_WIDTH], lo).astype(_BF16)


def _mix_in(x2, g_mix, w_in, conv_w, gq, gk, g_conv, hm, tiles_per_seq):
    n = x2.shape[0]
    ts = ROW_TILE
    row_spec = lambda w: pl.BlockSpec((ts, w), lambda i: (i, 0))
    return pl.pallas_call(
        functools.partial(_mix_in_kernel, tiles_per_seq),
        grid=(n // ts,),
        in_specs=[
            row_spec(D_MODEL),
            _const_spec((1, D_MODEL)),
            _const_spec((D_MODEL, IN_WIDTH)),
            _const_spec((3, CONV_WIDTH)),
            _const_spec((1, ATTN_WIDTH)),
            _const_spec((1, KV_WIDTH)),
            _const_spec((1, CONV_WIDTH)),
            _const_spec((ATTN_WIDTH, ATTN_WIDTH)),
        ],
        out_specs=[row_spec(CONV_WIDTH), row_spec(ATTN_WIDTH), row_spec(KX_WIDTH), row_spec(KX_WIDTH)],
        out_shape=[
            jax.ShapeDtypeStruct((n, CONV_WIDTH), _BF16),
            jax.ShapeDtypeStruct((n, ATTN_WIDTH), _BF16),
            jax.ShapeDtypeStruct((n, KX_WIDTH), _BF16),
            jax.ShapeDtypeStruct((n, KX_WIDTH), _BF16),
        ],
        scratch_shapes=[pltpu.VMEM((ts + HALO, CONV_WIDTH), _F32)],
        compiler_params=pltpu.CompilerParams(
            dimension_semantics=("arbitrary",), vmem_limit_bytes=VMEM_LIMIT_BYTES),
        name="mix_in",
    )(x2, g_mix, w_in, conv_w, gq, gk, g_conv, hm)


def _store_interleaved(slab_ref, val):
    ts = val.shape[0]
    seg = ts // SUBLANES
    for l in range(D_MODEL // LANES):
        for s in range(SUBLANES):
            slab_ref[l, pl.ds(s, seg, stride=SUBLANES), :] = val[s * seg:(s + 1) * seg, l * LANES:(l + 1) * LANES]


def _load_deinterleaved(slab_ref, out_ref):
    ts = out_ref.shape[0]
    seg = ts // SUBLANES
    for l in range(D_MODEL // LANES):
        for s in range(SUBLANES):
            out_ref[s * seg:(s + 1) * seg, l * LANES:(l + 1) * LANES] = slab_ref[l, pl.ds(s, seg, stride=SUBLANES), :]


def _attn_out_kernel(tiles_per_seq, tab_ref, sink_ref, q_ref, kx_ref, kxp_ref, vx_ref, vxp_ref, yc_ref, x_ref,
                     bkt_ref, ga_ref, wo_ref, o_ref, bias_ref, kxx_ref, vxx_ref, ya_ref, p_ref, st_ref):
    ts = q_ref.shape[0]
    nblk = ts // BLK
    i = pl.program_id(0)

    @pl.when(i == 0)
    def _():
        bkt = bkt_ref[...]
        col = lax.broadcasted_iota(jnp.int32, (BLK, 2 * BLK), 1)

        def head_body(h, carry):
            b = jnp.full((BLK, 2 * BLK), NEG_INF, _F32)
            for t in range(NUM_BUCKETS):
                b = jnp.where(bkt == t, tab_ref[t, h], b)
            bias_ref[0, h] = b
            bias_ref[1, h] = jnp.where(col >= BLK, b, NEG_INF)
            return carry

        lax.fori_loop(0, N_HEADS, head_body, 0)

        lane = lax.broadcasted_iota(jnp.int32, (ts + BLK, LANES), 1)
        ones_lo = jnp.where(lane < HEAD_DIM, 1.0, 0.0).astype(_BF16)
        ones_hi = jnp.where(lane < HEAD_DIM, 0.0, 1.0).astype(_BF16)
        for g in range(N_KV_HEADS):
            base = g * 4 * LANES
            vxx_ref[:, base + LANES:base + 2 * LANES] = ones_lo
            vxx_ref[:, base + 3 * LANES:base + 4 * LANES] = ones_hi

    kxx_ref[0:BLK, :] = kxp_ref[...]
    kxx_ref[BLK:BLK + ts, :] = kx_ref[...]
    for s in range(2 * N_KV_HEADS):
        vxx_ref[0:BLK, 2 * s * LANES:(2 * s + 1) * LANES] = vxp_ref[:, s * LANES:(s + 1) * LANES]
        vxx_ref[BLK:BLK + ts, 2 * s * LANES:(2 * s + 1) * LANES] = vx_ref[:, s * LANES:(s + 1) * LANES]
    seq_start = (i % tiles_per_seq == 0)
    lane_lo = lax.broadcasted_iota(jnp.int32, (BLK, LANES), 1) < HEAD_DIM

    def scores_stage(j):
        r0 = j * BLK
        slot = j % 2
        first = seq_start.astype(jnp.int32) if j == 0 else 0
        for g in range(N_KV_HEADS):
            qq = q_ref[pl.ds(r0, BLK), g * 2 * LANES:(g + 1) * 2 * LANES]
            lhs = jnp.concatenate([qq[:, 0:LANES], qq[:, LANES:2 * LANES]], axis=0)
            sink_terms = [[None, None], [None, None]]
            for par in range(2):
                kb = kxx_ref[pl.ds(r0, 2 * BLK), (2 * g + par) * LANES:(2 * g + par + 1) * LANES]
                logits = _dot_nt(lhs, kb)
                for pair in range(2):
                    h = g * GQA_GROUP + 2 * pair + par
                    lg = logits[pair * BLK:(pair + 1) * BLK, :] + bias_ref[first, h]
                    sink = sink_ref[h]
                    m = jnp.maximum(jnp.max(lg, axis=-1, keepdims=True), sink)
                    p_ref[slot, 2 * g + par, pair * BLK:(pair + 1) * BLK, :] = jnp.exp(lg - m).astype(_BF16)
                    sink_terms[pair][par] = jnp.exp(sink - m)
            for pair in range(2):
                st_ref[slot, 2 * g + pair] = jnp.where(lane_lo, sink_terms[pair][0], sink_terms[pair][1])

    def values_stage(j):
        r0 = j * BLK
        slot = j % 2
        for g in range(N_KV_HEADS):
            out = None
            for par in range(2):
                c = (4 * g + 2 * par) * LANES
                o = _dot(p_ref[slot, 2 * g + par],
                         vxx_ref[pl.ds(r0, 2 * BLK), c:c + 2 * LANES])
                out = o if out is None else out + o
            for pair in range(2):
                blk = out[pair * BLK:(pair + 1) * BLK, :]
                den = blk[:, LANES:2 * LANES] + st_ref[slot, 2 * g + pair]
                c = (2 * g + pair) * LANES
                ya_ref[pl.ds(r0, BLK), c:c + LANES] = blk[:, 0:LANES] / den

    for j in range(nblk):
        scores_stage(j)
        values_stage(j)

    ya = _rms(ya_ref[...], ga_ref[...]).astype(_BF16)
    h = (x_ref[...] + _dot(yc_ref[...], wo_ref[0:CONV_WIDTH, :])
         + _dot(ya, wo_ref[CONV_WIDTH:CONV_WIDTH + ATTN_WIDTH, :]))
    _store_interleaved(o_ref, h)


def _attn_out(tab, sinks, q, kx, vx, yc, x2, bkt, g_attn, w_out, tiles_per_seq):
    n = x2.shape[0]
    ts = ROW_TILE
    bpt = ts // BLK
    smem = pl.BlockSpec(memory_space=pltpu.SMEM)
    row_spec = lambda w: pl.BlockSpec((ts, w), lambda i: (i, 0))
    prev_spec = pl.BlockSpec((BLK, KX_WIDTH), lambda i: (jnp.maximum(i * bpt - 1, 0), 0))
    return pl.pallas_call(
        functools.partial(_attn_out_kernel, tiles_per_seq),
        grid=(n // ts,),
        in_specs=[
            smem,
            smem,
            row_spec(ATTN_WIDTH),
            row_spec(KX_WIDTH),
            prev_spec,
            row_spec(KX_WIDTH),
            prev_spec,
            row_spec(CONV_WIDTH),
            row_spec(D_MODEL),
            _const_spec((BLK, 2 * BLK)),
            _const_spec((1, ATTN_WIDTH)),
            _const_spec((CONV_WIDTH + ATTN_WIDTH, D_MODEL)),
        ],
        out_specs=pl.BlockSpec((D_MODEL // LANES, ts, LANES), lambda i: (0, i, 0)),
        out_shape=jax.ShapeDtypeStruct((D_MODEL // LANES, n, LANES), _F32),
        scratch_shapes=[
            pltpu.VMEM((2, N_HEADS, BLK, 2 * BLK), _F32),
            pltpu.VMEM((ts + BLK, KX_WIDTH), _BF16),
            pltpu.VMEM((ts + BLK, VX_WIDTH), _BF16),
            pltpu.VMEM((ts, ATTN_WIDTH), _F32),
            pltpu.VMEM((2, 2 * N_KV_HEADS, 2 * BLK, 2 * BLK), _BF16),
            pltpu.VMEM((2, 2 * N_KV_HEADS, BLK, LANES), _F32),
        ],
        compiler_params=pltpu.CompilerParams(
            dimension_semantics=("arbitrary",), vmem_limit_bytes=VMEM_LIMIT_BYTES),
        name="attn_out",
    )(tab, sinks, q, kx, kx, vx, vx, yc, x2, bkt, g_attn, w_out)


def _ffn_kernel(tiles_per_seq, h_ref, g_ref, wu_ref, cw_ref, cb_ref, wd_ref, o_ref,
                u_ref, ubuf, carry_ref, act_ref, res_ref):
    ts = o_ref.shape[0]
    i = pl.program_id(0)
    g2 = 2 * SUBLANES

    @pl.when(i % tiles_per_seq == 0)
    def _():
        carry_ref[...] = jnp.zeros(carry_ref.shape, _F32)

    h = jnp.concatenate([h_ref[l] for l in range(D_MODEL // LANES)], axis=1)
    u_ref[...] = _rms(h, g_ref[...]).astype(_BF16)
    first_sublane = lax.broadcasted_iota(jnp.int32, (SUBLANES, FF_CHUNK), 0) == 0

    def conv_half(c, part):
        slot = 2 * c + part
        buf = ubuf.at[slot % ubuf.shape[0]]
        cols = slice(part * D_FF + c * FF_CHUNK, part * D_FF + (c + 1) * FF_CHUNK)
        buf[g2:g2 + ts, :] = _dot(u_ref[...], wu_ref[:, cols])
        for k in range(2):
            cur = pltpu.roll(buf[ts + k * SUBLANES:ts + (k + 1) * SUBLANES, :], 1, axis=0)
            prev = pltpu.roll(carry_ref[slot, k * SUBLANES:(k + 1) * SUBLANES, :], 1, axis=0)
            buf[k * SUBLANES:(k + 1) * SUBLANES, :] = jnp.where(first_sublane, prev, cur)
        a = (cw_ref[0:1, cols] * buf[0:ts, :]
             + cw_ref[1:2, cols] * buf[SUBLANES:SUBLANES + ts, :]
             + cw_ref[2:3, cols] * buf[g2:g2 + ts, :] + cb_ref[:, cols])
        carry_ref[slot] = buf[ts:ts + g2, :]
        return a

    for c in range(N_FF_CHUNKS):
        gate = conv_half(c, 0)
        val = conv_half(c, 1)
        act_ref[:, c * FF_CHUNK:(c + 1) * FF_CHUNK] = (gate / (1.0 + jnp.exp(-gate)) * val).astype(_BF16)

    res = h + _dot(act_ref[...], wd_ref[...])
    for l in range(D_MODEL // LANES):
        res_ref[l] = res[:, l * LANES:(l + 1) * LANES]
    _load_deinterleaved(res_ref, o_ref)


def _ffn(h, g_ffn, w_up, conv_w, conv_b, w_down, tiles_per_seq):
    n = h.shape[1]
    ts = ROW_TILE
    g2 = 2 * SUBLANES
    return pl.pallas_call(
        functools.partial(_ffn_kernel, tiles_per_seq),
        grid=(n // ts,),
        in_specs=[
            pl.BlockSpec((D_MODEL // LANES, ts, LANES), lambda i: (0, i, 0)),
            _const_spec((1, D_MODEL)),
            _const_spec((D_MODEL, 2 * D_FF)),
            _const_spec((3, 2 * D_FF)),
            _const_spec((1, 2 * D_FF)),
            _const_spec((D_FF, D_MODEL)),
        ],
        out_specs=pl.BlockSpec((ts, D_MODEL), lambda i: (i, 0)),
        out_shape=jax.ShapeDtypeStruct((n, D_MODEL), _F32),
        scratch_shapes=[
            pltpu.VMEM((ts, D_MODEL), _BF16),
            pltpu.VMEM((4, ts + g2, FF_CHUNK), _F32),
            pltpu.VMEM((2 * N_FF_CHUNKS, g2, FF_CHUNK), _F32),
            pltpu.VMEM((ts, D_FF), _BF16),
            pltpu.VMEM((D_MODEL // LANES, ts, LANES), _F32),
        ],
        compiler_params=pltpu.CompilerParams(
            dimension_semantics=("arbitrary",), vmem_limit_bytes=VMEM_LIMIT_BYTES),
        name="ffn",
    )(h, g_ffn, w_up, conv_w, conv_b, w_down)


def kernel(x, norm_mix_g, w_in, conv_w, q_norm_g, k_norm_g, rel_bias_table, sinks, out_norm_conv_g,
           out_norm_attn_g, w_out, norm_ffn_g, w_up, ffn_conv_w, ffn_conv_b, w_down):
    bsz, seq, d = x.shape
    assert d == D_MODEL and seq % ROW_TILE == 0 and norm_mix_g.shape[0] == 1
    tiles_per_seq = seq // ROW_TILE
    n = bsz * seq
    x2 = x.reshape(n, d)
    bkt = jnp.asarray(_bucket_map())
    hm = _head_mean_matrix()

    yc, q, kx, vx = _mix_in(
        x2, norm_mix_g[0][None, :], w_in[0].astype(_BF16), conv_w[0],
        jnp.tile(q_norm_g[0], N_HEADS)[None, :], jnp.tile(k_norm_g[0], N_KV_HEADS)[None, :],
        out_norm_conv_g[0][None, :], hm, tiles_per_seq)
    h = _attn_out(rel_bias_table, sinks[0], q, kx, vx, yc, x2, bkt, out_norm_attn_g[0][None, :],
                  w_out[0].astype(_BF16), tiles_per_seq)
    out = _ffn(h, norm_ffn_g[0][None, :], w_up[0].astype(_BF16), ffn_conv_w[0], ffn_conv_b[0][None, :],
               w_down[0].astype(_BF16), tiles_per_seq)
    return out.reshape(bsz, seq, d)
```

```python
import functools
import math

import jax
import jax.numpy as jnp
import numpy as np
from jax import lax
from jax.experimental import pallas as pl
from jax.experimental.pallas import tpu as pltpu

D_MODEL = 1024
CONV_WIDTH = 512
HEAD_DIM = 64
N_HEADS = 8
N_KV_HEADS = 2
GQA_GROUP = N_HEADS // N_KV_HEADS
ATTN_WIDTH = N_HEADS * HEAD_DIM
KV_WIDTH = N_KV_HEADS * HEAD_DIM
WINDOW = 128
BLK = 128
NUM_BUCKETS = 32
MAX_DISTANCE = 128
MAX_EXACT = NUM_BUCKETS // 2
D_FF = 2816
EPS = 1e-6
NEG_INF = -1e30
LOG2E = math.log2(math.e)
IN_WIDTH = 3 * CONV_WIDTH + ATTN_WIDTH + 2 * KV_WIDTH

LANES = 128
SUBLANES = 8
PAIR = 2 * HEAD_DIM
assert PAIR == LANES and KV_WIDTH == LANES and GQA_GROUP == 4
KX_WIDTH = 2 * N_KV_HEADS * LANES
VX_WIDTH = 2 * KX_WIDTH

ROW_TILE = 1024
FF_CHUNK = 256
N_FF_CHUNKS = D_FF // FF_CHUNK
HALO = 8
VMEM_LIMIT_BYTES = 60 * 1024 * 1024

_BF16 = jnp.bfloat16
_F32 = jnp.float32


def _dot(a, b):
    return jnp.dot(a, b, preferred_element_type=_F32)


def _dot_nt(a, b):
    return lax.dot_general(a, b, (((1,), (1,)), ((), ())), preferred_element_type=_F32)


def _rms(xf, g):
    return xf * lax.rsqrt(jnp.mean(xf * xf, axis=-1, keepdims=True) + EPS) * g


def _bucket_map():
    q = np.arange(BLK, dtype=np.int32)[:, None]
    j = np.arange(2 * BLK, dtype=np.int32)[None, :]
    d = q + BLK - j
    n = np.maximum(d, 0)
    nf = np.maximum(n, 1).astype(np.float32)
    large = MAX_EXACT + (np.log(nf / MAX_EXACT) / math.log(MAX_DISTANCE / MAX_EXACT)
                         * (NUM_BUCKETS - MAX_EXACT)).astype(np.int32)
    large = np.minimum(large, NUM_BUCKETS - 1)
    bucket = np.where(n < MAX_EXACT, n, large).astype(np.int32)
    within = (d >= 0) & (d < WINDOW)
    return np.where(within, bucket, -1).astype(np.int32)


def _head_mean_matrix():
    i = np.arange(ATTN_WIDTH)
    m = (i[:, None] // HEAD_DIM == i[None, :] // HEAD_DIM).astype(np.float32) / HEAD_DIM
    return jnp.asarray(m, dtype=_BF16)


def _const_spec(shape):
    return pl.BlockSpec(shape, lambda i: (0,) * len(shape), pipeline_mode=pl.Buffered(1))


def _lo_hi_layout(a, lo):
    ar = pltpu.roll(a, HEAD_DIM, axis=1)
    zero = jnp.zeros_like(a)
    return jnp.concatenate([jnp.where(lo, a, zero), jnp.where(lo, zero, ar),
                            jnp.where(lo, ar, zero), jnp.where(lo, zero, a)], axis=1)


def _mix_in_kernel(tiles_per_seq, x_ref, g_ref, w_ref, cw_ref, gq_ref, gk_ref, gc_ref, hm_ref,
                   wo_f32, wu_f32, wd_f32,
                   yc_ref, q_ref, kx_ref, vx_ref, wo_bf16, wu_bf16, wd_bf16, zbuf):
    ts = x_ref.shape[0]
    i = pl.program_id(0)

    wo_bf16[...] = wo_f32[...].astype(_BF16)
    wu_bf16[...] = wu_f32[...].astype(_BF16)
    wd_bf16[...] = wd_f32[...].astype(_BF16)

    @pl.when(i % tiles_per_seq == 0)
    def _():
        zbuf[0:HALO, :] = jnp.zeros((HALO, CONV_WIDTH), _F32)

    u = _rms(x_ref[...], g_ref[...]).astype(_BF16)

    c0 = 3 * CONV_WIDTH
    c1 = c0 + ATTN_WIDTH
    q = _dot(u, w_ref[:, c0:c0 + ATTN_WIDTH])
    kv = _dot(u, w_ref[:, c1:c1 + 2 * KV_WIDTH])
    k = kv[:, 0:KV_WIDTH]
    gate_c = _dot(u, w_ref[:, CONV_WIDTH:2 * CONV_WIDTH])
    hc = _dot(u, w_ref[:, 2 * CONV_WIDTH:3 * CONV_WIDTH])
    gate_b = _dot(u, w_ref[:, 0:CONV_WIDTH])
    msq = _dot((q * q).astype(_BF16), hm_ref[...])
    msk = _dot((k * k).astype(_BF16), hm_ref[0:KV_WIDTH, 0:KV_WIDTH])

    z = gate_c * hc
    zbuf[HALO:HALO + ts, :] = z
    conv = (cw_ref[0:1, :] * zbuf[HALO - 2:HALO - 2 + ts, :]
            + cw_ref[1:2, :] * zbuf[HALO - 1:HALO - 1 + ts, :]
            + cw_ref[2:3, :] * z)
    zbuf[0:HALO, :] = zbuf[ts:ts + HALO, :]
    yc_ref[...] = _rms(gate_b * conv, gc_ref[...]).astype(_BF16)

    q_ref[...] = (q * lax.rsqrt(msq + EPS) * gq_ref[...] * (HEAD_DIM ** -0.5 * LOG2E)).astype(_BF16)
    kn = k * lax.rsqrt(msk + EPS) * gk_ref[...]
    lo = lax.broadcasted_iota(jnp.int32, (ts, LANES), 1) < HEAD_DIM
    kx_ref[...] = _lo_hi_layout(kn, lo).astype(_BF16)
    vx_ref[...] = _lo_hi_layout(kv[:, KV_WIDTH:2 * KV_WIDTH], lo).astype(_BF16)


def _mix_in(x2, g_mix, w_in, conv_w, gq, gk, g_conv, hm, w_out, w_up, w_down, tiles_per_seq):
    n = x2.shape[0]
    ts = ROW_TILE
    steps = n // ts
    row_spec = lambda w: pl.BlockSpec((ts, w), lambda i: (i, 0))

    def slice_spec(w):
        rows, cols = w.shape
        assert rows % (steps * 2 * SUBLANES) == 0
        return pl.BlockSpec((rows // steps, cols), lambda i: (i, 0))

    cast_weights = (w_out, w_up, w_down)
    return pl.pallas_call(
        functools.partial(_mix_in_kernel, tiles_per_seq),
        grid=(steps,),
        in_specs=[
            row_spec(D_MODEL),
            _const_spec((1, D_MODEL)),
            _const_spec((D_MODEL, IN_WIDTH)),
            _const_spec((3, CONV_WIDTH)),
            _const_spec((1, ATTN_WIDTH)),
            _const_spec((1, KV_WIDTH)),
            _const_spec((1, CONV_WIDTH)),
            _const_spec((ATTN_WIDTH, ATTN_WIDTH)),
        ] + [slice_spec(w) for w in cast_weights],
        out_specs=[row_spec(CONV_WIDTH), row_spec(ATTN_WIDTH), row_spec(KX_WIDTH), row_spec(KX_WIDTH)]
        + [slice_spec(w) for w in cast_weights],
        out_shape=[
            jax.ShapeDtypeStruct((n, CONV_WIDTH), _BF16),
            jax.ShapeDtypeStruct((n, ATTN_WIDTH), _BF16),
            jax.ShapeDtypeStruct((n, KX_WIDTH), _BF16),
            jax.ShapeDtypeStruct((n, KX_WIDTH), _BF16),
        ] + [jax.ShapeDtypeStruct(w.shape, _BF16) for w in cast_weights],
        scratch_shapes=[pltpu.VMEM((ts + HALO, CONV_WIDTH), _F32)],
        compiler_params=pltpu.CompilerParams(
            dimension_semantics=("arbitrary",), vmem_limit_bytes=VMEM_LIMIT_BYTES),
        name="mix_in",
    )(x2, g_mix, w_in, conv_w, gq, gk, g_conv, hm, *cast_weights)


def _store_interleaved(slab_ref, val):
    ts = val.shape[0]
    seg = ts // SUBLANES
    for l in range(D_MODEL // LANES):
        for s in range(SUBLANES):
            slab_ref[l, pl.ds(s, seg, stride=SUBLANES), :] = val[s * seg:(s + 1) * seg, l * LANES:(l + 1) * LANES]


def _load_deinterleaved(slab_ref, out_ref):
    ts = out_ref.shape[0]
    seg = ts // SUBLANES
    for l in range(D_MODEL // LANES):
        for s in range(SUBLANES):
            out_ref[s * seg:(s + 1) * seg, l * LANES:(l + 1) * LANES] = slab_ref[l, pl.ds(s, seg, stride=SUBLANES), :]


def _attn_out_kernel(tiles_per_seq, tab_ref, sink_ref, q_ref, kx_ref, kxp_ref, vx_ref, vxp_ref, yc_ref, x_ref,
                     bkt_ref, ga_ref, wo_ref, o_ref, bias_ref, kxx_ref, vxx_ref, ya_ref, p_ref, st_ref):
    ts = q_ref.shape[0]
    nblk = ts // BLK
    i = pl.program_id(0)

    @pl.when(i == 0)
    def _():
        bkt = bkt_ref[...]
        col = lax.broadcasted_iota(jnp.int32, (BLK, 2 * BLK), 1)

        def head_body(h, carry):
            b = jnp.full((BLK, 2 * BLK), NEG_INF, _F32)
            for t in range(NUM_BUCKETS):
                b = jnp.where(bkt == t, tab_ref[t, h] * LOG2E, b)
            bias_ref[0, h] = b
            bias_ref[1, h] = jnp.where(col >= BLK, b, NEG_INF)
            return carry

        lax.fori_loop(0, N_HEADS, head_body, 0)

        lane = lax.broadcasted_iota(jnp.int32, (ts + BLK, LANES), 1)
        ones_lo = jnp.where(lane < HEAD_DIM, 1.0, 0.0).astype(_BF16)
        ones_hi = jnp.where(lane < HEAD_DIM, 0.0, 1.0).astype(_BF16)
        for g in range(N_KV_HEADS):
            base = g * 4 * LANES
            vxx_ref[:, base + LANES:base + 2 * LANES] = ones_lo
            vxx_ref[:, base + 3 * LANES:base + 4 * LANES] = ones_hi

    kxx_ref[0:BLK, :] = kxp_ref[...]
    kxx_ref[BLK:BLK + ts, :] = kx_ref[...]
    for s in range(2 * N_KV_HEADS):
        vxx_ref[0:BLK, 2 * s * LANES:(2 * s + 1) * LANES] = vxp_ref[:, s * LANES:(s + 1) * LANES]
        vxx_ref[BLK:BLK + ts, 2 * s * LANES:(2 * s + 1) * LANES] = vx_ref[:, s * LANES:(s + 1) * LANES]
    seq_start = (i % tiles_per_seq == 0)
    lane_lo = lax.broadcasted_iota(jnp.int32, (BLK, LANES), 1) < HEAD_DIM

    def scores_stage(j):
        r0 = j * BLK
        slot = j % 2
        first = seq_start.astype(jnp.int32) if j == 0 else 0
        for g in range(N_KV_HEADS):
            qq = q_ref[pl.ds(r0, BLK), g * 2 * LANES:(g + 1) * 2 * LANES]
            lhs = jnp.concatenate([qq[:, 0:LANES], qq[:, LANES:2 * LANES]], axis=0)
            sink_terms = [[None, None], [None, None]]
            for par in range(2):
                kb = kxx_ref[pl.ds(r0, 2 * BLK), (2 * g + par) * LANES:(2 * g + par + 1) * LANES]
                logits = _dot_nt(lhs, kb)
                for pair in range(2):
                    h = g * GQA_GROUP + 2 * pair + par
                    lg = logits[pair * BLK:(pair + 1) * BLK, :] + bias_ref[first, h]
                    sink = sink_ref[h] * LOG2E
                    m = jnp.maximum(jnp.max(lg, axis=-1, keepdims=True), sink)
                    p_ref[slot, 2 * g + par, pair * BLK:(pair + 1) * BLK, :] = jnp.exp2(lg - m).astype(_BF16)
                    sink_terms[pair][par] = jnp.exp2(sink - m)
            for pair in range(2):
                st_ref[slot, 2 * g + pair] = jnp.where(lane_lo, sink_terms[pair][0], sink_terms[pair][1])

    def values_stage(j):
        r0 = j * BLK
        slot = j % 2
        for g in range(N_KV_HEADS):
            out = None
            for par in range(2):
                c = (4 * g + 2 * par) * LANES
                o = _dot(p_ref[slot, 2 * g + par],
                         vxx_ref[pl.ds(r0, 2 * BLK), c:c + 2 * LANES])
                out = o if out is None else out + o
            for pair in range(2):
                blk = out[pair * BLK:(pair + 1) * BLK, :]
                den = blk[:, LANES:2 * LANES] + st_ref[slot, 2 * g + pair]
                c = (2 * g + pair) * LANES
                ya_ref[pl.ds(r0, BLK), c:c + LANES] = blk[:, 0:LANES] / den

    scores_stage(0)
    for j in range(nblk):
        if j + 1 < nblk:
            scores_stage(j + 1)
        values_stage(j)

    ya = _rms(ya_ref[...], ga_ref[...]).astype(_BF16)
    h = (x_ref[...] + _dot(yc_ref[...], wo_ref[0:CONV_WIDTH, :])
         + _dot(ya, wo_ref[CONV_WIDTH:CONV_WIDTH + ATTN_WIDTH, :]))
    _store_interleaved(o_ref, h)


def _attn_out(tab, sinks, q, kx, vx, yc, x2, bkt, g_attn, w_out, tiles_per_seq):
    n = x2.shape[0]
    ts = ROW_TILE
    bpt = ts // BLK
    smem = pl.BlockSpec(memory_space=pltpu.SMEM)
    row_spec = lambda w: pl.BlockSpec((ts, w), lambda i: (i, 0))
    prev_spec = pl.BlockSpec((BLK, KX_WIDTH), lambda i: (jnp.maximum(i * bpt - 1, 0), 0))
    return pl.pallas_call(
        functools.partial(_attn_out_kernel, tiles_per_seq),
        grid=(n // ts,),
        in_specs=[
            smem,
            smem,
            row_spec(ATTN_WIDTH),
            row_spec(KX_WIDTH),
            prev_spec,
            row_spec(KX_WIDTH),
            prev_spec,
            row_spec(CONV_WIDTH),
            row_spec(D_MODEL),
            _const_spec((BLK, 2 * BLK)),
            _const_spec((1, ATTN_WIDTH)),
            _const_spec((CONV_WIDTH + ATTN_WIDTH, D_MODEL)),
        ],
        out_specs=pl.BlockSpec((D_MODEL // LANES, ts, LANES), lambda i: (0, i, 0)),
        out_shape=jax.ShapeDtypeStruct((D_MODEL // LANES, n, LANES), _F32),
        scratch_shapes=[
            pltpu.VMEM((2, N_HEADS, BLK, 2 * BLK), _F32),
            pltpu.VMEM((ts + BLK, KX_WIDTH), _BF16),
            pltpu.VMEM((ts + BLK, VX_WIDTH), _BF16),
            pltpu.VMEM((ts, ATTN_WIDTH), _F32),
            pltpu.VMEM((2, 2 * N_KV_HEADS, 2 * BLK, 2 * BLK), _BF16),
            pltpu.VMEM((2, 2 * N_KV_HEADS, BLK, LANES), _F32),
        ],
        compiler_params=pltpu.CompilerParams(
            dimension_semantics=("arbitrary",), vmem_limit_bytes=VMEM_LIMIT_BYTES),
        name="attn_out",
    )(tab, sinks, q, kx, kx, vx, vx, yc, x2, bkt, g_attn, w_out)


def _ffn_kernel(tiles_per_seq, h_ref, g_ref, wu_ref, cw_ref, cb_ref, wd_ref, o_ref,
                u_ref, ubuf, carry_ref, act_ref, res_ref):
    ts = o_ref.shape[0]
    i = pl.program_id(0)
    g2 = 2 * SUBLANES

    @pl.when(i % tiles_per_seq == 0)
    def _():
        carry_ref[...] = jnp.zeros(carry_ref.shape, _F32)

    h = jnp.concatenate([h_ref[l] for l in range(D_MODEL // LANES)], axis=1)
    u_ref[...] = _rms(h, g_ref[...]).astype(_BF16)
    first_sublane = lax.broadcasted_iota(jnp.int32, (SUBLANES, FF_CHUNK), 0) == 0

    def conv_half(c, part):
        slot = 2 * c + part
        buf = ubuf.at[slot % ubuf.shape[0]]
        cols = slice(part * D_FF + c * FF_CHUNK, part * D_FF + (c + 1) * FF_CHUNK)
        buf[g2:g2 + ts, :] = _dot(u_ref[...], wu_ref[:, cols])
        for k in range(2):
            cur = pltpu.roll(buf[ts + k * SUBLANES:ts + (k + 1) * SUBLANES, :], 1, axis=0)
            prev = pltpu.roll(carry_ref[slot, k * SUBLANES:(k + 1) * SUBLANES, :], 1, axis=0)
            buf[k * SUBLANES:(k + 1) * SUBLANES, :] = jnp.where(first_sublane, prev, cur)
        a = (cw_ref[0:1, cols] * buf[0:ts, :]
             + cw_ref[1:2, cols] * buf[SUBLANES:SUBLANES + ts, :]
             + cw_ref[2:3, cols] * buf[g2:g2 + ts, :] + cb_ref[:, cols])
        carry_ref[slot] = buf[ts:ts + g2, :]
        return a

    for c in range(N_FF_CHUNKS):
        gate = conv_half(c, 0)
        val = conv_half(c, 1)
        act_ref[:, c * FF_CHUNK:(c + 1) * FF_CHUNK] = (gate / (1.0 + jnp.exp(-gate)) * val).astype(_BF16)

    res = h + _dot(act_ref[...], wd_ref[...])
    for l in range(D_MODEL // LANES):
        res_ref[l] = res[:, l * LANES:(l + 1) * LANES]
    _load_deinterleaved(res_ref, o_ref)


def _ffn(h, g_ffn, w_up, conv_w, conv_b, w_down, tiles_per_seq):
    n = h.shape[1]
    ts = ROW_TILE
    g2 = 2 * SUBLANES
    return pl.pallas_call(
        functools.partial(_ffn_kernel, tiles_per_seq),
        grid=(n // ts,),
        in_specs=[
            pl.BlockSpec((D_MODEL // LANES, ts, LANES), lambda i: (0, i, 0)),
            _const_spec((1, D_MODEL)),
            _const_spec((D_MODEL, 2 * D_FF)),
            _const_spec((3, 2 * D_FF)),
            _const_spec((1, 2 * D_FF)),
            _const_spec((D_FF, D_MODEL)),
        ],
        out_specs=pl.BlockSpec((ts, D_MODEL), lambda i: (i, 0)),
        out_shape=jax.ShapeDtypeStruct((n, D_MODEL), _F32),
        scratch_shapes=[
            pltpu.VMEM((ts, D_MODEL), _BF16),
            pltpu.VMEM((4, ts + g2, FF_CHUNK), _F32),
            pltpu.VMEM((2 * N_FF_CHUNKS, g2, FF_CHUNK), _F32),
            pltpu.VMEM((ts, D_FF), _BF16),
            pltpu.VMEM((D_MODEL // LANES, ts, LANES), _F32),
        ],
        compiler_params=pltpu.CompilerParams(
            dimension_semantics=("arbitrary",), vmem_limit_bytes=VMEM_LIMIT_BYTES),
        name="ffn",
    )(h, g_ffn, w_up, conv_w, conv_b, w_down)


def kernel(x, norm_mix_g, w_in, conv_w, q_norm_g, k_norm_g, rel_bias_table, sinks, out_norm_conv_g,
           out_norm_attn_g, w_out, norm_ffn_g, w_up, ffn_conv_w, ffn_conv_b, w_down):
    bsz, seq, d = x.shape
    assert d == D_MODEL and seq % ROW_TILE == 0 and norm_mix_g.shape[0] == 1
    tiles_per_seq = seq // ROW_TILE
    n = bsz * seq
    x2 = x.reshape(n, d)
    bkt = jnp.asarray(_bucket_map())
    hm = _head_mean_matrix()

    yc, q, kx, vx, w_out_b, w_up_b, w_down_b = _mix_in(
        x2, norm_mix_g[0][None, :], w_in[0].astype(_BF16), conv_w[0],
        jnp.tile(q_norm_g[0], N_HEADS)[None, :], jnp.tile(k_norm_g[0], N_KV_HEADS)[None, :],
        out_norm_conv_g[0][None, :], hm, w_out[0], w_up[0], w_down[0], tiles_per_seq)
    h = _attn_out(rel_bias_table, sinks[0], q, kx, vx, yc, x2, bkt, out_norm_attn_g[0][None, :],
                  w_out_b, tiles_per_seq)
    out = _ffn(h, norm_ffn_g[0][None, :], w_up_b, ffn_conv_w[0], ffn_conv_b[0][None, :],
               w_down_b, tiles_per_seq)
    return out.reshape(bsz, seq, d)
```

```python
import functools
import math

import jax
import jax.numpy as jnp
import numpy as np
from jax import lax
from jax.experimental import pallas as pl
from jax.experimental.pallas import tpu as pltpu

D_MODEL = 1024
CONV_WIDTH = 512
HEAD_DIM = 64
N_HEADS = 8
N_KV_HEADS = 2
GQA_GROUP = N_HEADS // N_KV_HEADS
ATTN_WIDTH = N_HEADS * HEAD_DIM
KV_WIDTH = N_KV_HEADS * HEAD_DIM
WINDOW = 128
BLK = 128
NUM_BUCKETS = 32
MAX_DISTANCE = 128
MAX_EXACT = NUM_BUCKETS // 2
D_FF = 2816
EPS = 1e-6
NEG_INF = -1e30
LOG2E = math.log2(math.e)
IN_WIDTH = 3 * CONV_WIDTH + ATTN_WIDTH + 2 * KV_WIDTH

LANES = 128
SUBLANES = 8
PAIR = 2 * HEAD_DIM
assert PAIR == LANES and KV_WIDTH == LANES and GQA_GROUP == 4
KX_WIDTH = 2 * N_KV_HEADS * LANES
VX_WIDTH = 2 * KX_WIDTH

ROW_TILE = 1024
FF_CHUNK = 256
OUT_TILE = 256
N_FF_CHUNKS = D_FF // FF_CHUNK
HALO = 8
VMEM_LIMIT_BYTES = 60 * 1024 * 1024

_BF16 = jnp.bfloat16
_F32 = jnp.float32


def _dot(a, b):
    return jnp.dot(a, b, preferred_element_type=_F32)


def _dot_nt(a, b):
    return lax.dot_general(a, b, (((1,), (1,)), ((), ())), preferred_element_type=_F32)


def _rms(xf, g):
    return xf * lax.rsqrt(jnp.mean(xf * xf, axis=-1, keepdims=True) + EPS) * g


def _bucket_map():
    q = np.arange(BLK, dtype=np.int32)[:, None]
    j = np.arange(2 * BLK, dtype=np.int32)[None, :]
    d = q + BLK - j
    n = np.maximum(d, 0)
    nf = np.maximum(n, 1).astype(np.float32)
    large = MAX_EXACT + (np.log(nf / MAX_EXACT) / math.log(MAX_DISTANCE / MAX_EXACT)
                         * (NUM_BUCKETS - MAX_EXACT)).astype(np.int32)
    large = np.minimum(large, NUM_BUCKETS - 1)
    bucket = np.where(n < MAX_EXACT, n, large).astype(np.int32)
    within = (d >= 0) & (d < WINDOW)
    return np.where(within, bucket, -1).astype(np.int32)


def _head_mean_matrix():
    i = np.arange(ATTN_WIDTH)
    m = (i[:, None] // HEAD_DIM == i[None, :] // HEAD_DIM).astype(np.float32) / HEAD_DIM
    return jnp.asarray(m, dtype=_BF16)


def _const_spec(shape):
    return pl.BlockSpec(shape, lambda i: (0,) * len(shape), pipeline_mode=pl.Buffered(1))


def _lo_hi_layout(a, lo):
    ar = pltpu.roll(a, HEAD_DIM, axis=1)
    zero = jnp.zeros_like(a)
    return [jnp.where(lo, a, zero), jnp.where(lo, zero, ar), jnp.where(lo, ar, zero), jnp.where(lo, zero, a)]


def _mix_in_kernel(tiles_per_seq, x_ref, g_ref, w_ref, cw_ref, gq_ref, gk_ref, gc_ref, hm_ref,
                   wo_f32, wu_f32, wd_f32,
                   yc_ref, q_ref, kx_ref, vx_ref, wo_bf16, wu_bf16, wd_bf16, zbuf):
    ts = x_ref.shape[0]
    i = pl.program_id(0)

    wo_bf16[...] = wo_f32[...].astype(_BF16)
    wu_bf16[...] = wu_f32[...].astype(_BF16)
    wd_bf16[...] = wd_f32[...].astype(_BF16)

    @pl.when(i % tiles_per_seq == 0)
    def _():
        zbuf[0:HALO, :] = jnp.zeros((HALO, CONV_WIDTH), _F32)

    u = _rms(x_ref[...], g_ref[...]).astype(_BF16)

    c0 = 3 * CONV_WIDTH
    c1 = c0 + ATTN_WIDTH
    q = _dot(u, w_ref[:, c0:c0 + ATTN_WIDTH])
    kv = _dot(u, w_ref[:, c1:c1 + 2 * KV_WIDTH])
    k = kv[:, 0:KV_WIDTH]
    gate_c = _dot(u, w_ref[:, CONV_WIDTH:2 * CONV_WIDTH])
    hc = _dot(u, w_ref[:, 2 * CONV_WIDTH:3 * CONV_WIDTH])
    gate_b = _dot(u, w_ref[:, 0:CONV_WIDTH])
    msq = _dot((q * q).astype(_BF16), hm_ref[...])
    msk = _dot((k * k).astype(_BF16), hm_ref[0:KV_WIDTH, 0:KV_WIDTH])

    z = gate_c * hc
    zbuf[HALO:HALO + ts, :] = z
    conv = (cw_ref[0:1, :] * zbuf[HALO - 2:HALO - 2 + ts, :]
            + cw_ref[1:2, :] * zbuf[HALO - 1:HALO - 1 + ts, :]
            + cw_ref[2:3, :] * z)
    zbuf[0:HALO, :] = zbuf[ts:ts + HALO, :]
    yc_ref[...] = _rms(gate_b * conv, gc_ref[...]).astype(_BF16)

    q_ref[...] = (q * lax.rsqrt(msq + EPS) * gq_ref[...] * (HEAD_DIM ** -0.5 * LOG2E)).astype(_BF16)
    kn = k * lax.rsqrt(msk + EPS) * gk_ref[...]
    lo = lax.broadcasted_iota(jnp.int32, (ts, LANES), 1) < HEAD_DIM
    kx_ref[...] = jnp.concatenate(_lo_hi_layout(kn, lo), axis=1).astype(_BF16)
    ones = [jnp.where(lo, 1.0, 0.0).astype(_F32), jnp.where(lo, 0.0, 1.0).astype(_F32)]
    v_pieces = _lo_hi_layout(kv[:, KV_WIDTH:2 * KV_WIDTH], lo)
    vx_ref[...] = jnp.concatenate([piece for s, v in enumerate(v_pieces) for piece in (v, ones[s % 2])],
                                  axis=1).astype(_BF16)


def _mix_in(x2, g_mix, w_in, conv_w, gq, gk, g_conv, hm, w_out, w_up, w_down, tiles_per_seq):
    n = x2.shape[0]
    ts = ROW_TILE
    steps = n // ts
    row_spec = lambda w: pl.BlockSpec((ts, w), lambda i: (i, 0))

    def slice_spec(w):
        rows, cols = w.shape
        assert rows % (steps * 2 * SUBLANES) == 0
        return pl.BlockSpec((rows // steps, cols), lambda i: (i, 0))

    cast_weights = (w_out, w_up, w_down)
    return pl.pallas_call(
        functools.partial(_mix_in_kernel, tiles_per_seq),
        grid=(steps,),
        in_specs=[
            row_spec(D_MODEL),
            _const_spec((1, D_MODEL)),
            _const_spec((D_MODEL, IN_WIDTH)),
            _const_spec((3, CONV_WIDTH)),
            _const_spec((1, ATTN_WIDTH)),
            _const_spec((1, KV_WIDTH)),
            _const_spec((1, CONV_WIDTH)),
            _const_spec((ATTN_WIDTH, ATTN_WIDTH)),
        ] + [slice_spec(w) for w in cast_weights],
        out_specs=[row_spec(CONV_WIDTH), row_spec(ATTN_WIDTH), row_spec(KX_WIDTH), row_spec(VX_WIDTH)]
        + [slice_spec(w) for w in cast_weights],
        out_shape=[
            jax.ShapeDtypeStruct((n, CONV_WIDTH), _BF16),
            jax.ShapeDtypeStruct((n, ATTN_WIDTH), _BF16),
            jax.ShapeDtypeStruct((n, KX_WIDTH), _BF16),
            jax.ShapeDtypeStruct((n, VX_WIDTH), _BF16),
        ] + [jax.ShapeDtypeStruct(w.shape, _BF16) for w in cast_weights],
        scratch_shapes=[pltpu.VMEM((ts + HALO, CONV_WIDTH), _F32)],
        compiler_params=pltpu.CompilerParams(
            dimension_semantics=("arbitrary",), vmem_limit_bytes=VMEM_LIMIT_BYTES),
        name="mix_in",
    )(x2, g_mix, w_in, conv_w, gq, gk, g_conv, hm, *cast_weights)


def _store_interleaved(slab_ref, val, first_slab):
    ts = val.shape[0]
    seg = ts // SUBLANES
    for l in range(val.shape[1] // LANES):
        for s in range(SUBLANES):
            slab_ref[first_slab + l, pl.ds(s, seg, stride=SUBLANES), :] = (
                val[s * seg:(s + 1) * seg, l * LANES:(l + 1) * LANES])


def _load_deinterleaved(slab_ref, out_ref):
    ts = out_ref.shape[0]
    seg = ts // SUBLANES
    for l in range(D_MODEL // LANES):
        for s in range(SUBLANES):
            out_ref[s * seg:(s + 1) * seg, l * LANES:(l + 1) * LANES] = slab_ref[l, pl.ds(s, seg, stride=SUBLANES), :]


def _attn_out_kernel(tiles_per_seq, n_tiles, tab_ref, sink_ref, q_ref, kx_ref, kxp_ref, vx_ref, vxp_ref,
                     yc_ref, x_ref, bkt_ref, ga_ref, wo_ref, o_ref,
                     bias_ref, kb0_ref, vb0_ref, yan_ref, acc_ref, p_ref, st_ref):
    ts = q_ref.shape[0]
    nblk = ts // BLK
    i = pl.program_id(0)

    @pl.when(i == 0)
    def _():
        bkt = bkt_ref[...]
        col = lax.broadcasted_iota(jnp.int32, (BLK, 2 * BLK), 1)

        def head_body(h, carry):
            b = jnp.full((BLK, 2 * BLK), NEG_INF, _F32)
            for t in range(NUM_BUCKETS):
                b = jnp.where(bkt == t, tab_ref[t, h] * LOG2E, b)
            bias_ref[0, h] = b
            bias_ref[1, h] = jnp.where(col >= BLK, b, NEG_INF)
            return carry

        lax.fori_loop(0, N_HEADS, head_body, 0)

        yan_ref[1] = jnp.zeros(yan_ref.shape[1:], _BF16)

    seq_start = (jnp.minimum(i, n_tiles - 1) % tiles_per_seq == 0)
    lane_lo = lax.broadcasted_iota(jnp.int32, (BLK, LANES), 1) < HEAD_DIM

    def band(tile_ref, band0_ref, j, cols):
        if j == 0:
            return band0_ref[:, cols]
        return tile_ref[(j - 1) * BLK:(j + 1) * BLK, cols]

    def fill_band0():
        kb0_ref[0:BLK, :] = kxp_ref[...]
        kb0_ref[BLK:2 * BLK, :] = kx_ref[0:BLK, :]
        vb0_ref[0:BLK, :] = vxp_ref[...]
        vb0_ref[BLK:2 * BLK, :] = vx_ref[0:BLK, :]

    def scores_stage(j):
        r0 = j * BLK
        slot = j % 2
        first = seq_start.astype(jnp.int32) if j == 0 else 0
        for g in range(N_KV_HEADS):
            qq = q_ref[pl.ds(r0, BLK), g * 2 * LANES:(g + 1) * 2 * LANES]
            lhs = jnp.concatenate([qq[:, 0:LANES], qq[:, LANES:2 * LANES]], axis=0)
            sink_terms = [[None, None], [None, None]]
            for par in range(2):
                kb = band(kx_ref, kb0_ref, j, slice((2 * g + par) * LANES, (2 * g + par + 1) * LANES))
                logits = _dot_nt(lhs, kb)
                for pair in range(2):
                    h = g * GQA_GROUP + 2 * pair + par
                    lg = logits[pair * BLK:(pair + 1) * BLK, :] + bias_ref[first, h]
                    sink = sink_ref[h] * LOG2E
                    m = jnp.maximum(jnp.max(lg, axis=-1, keepdims=True), sink)
                    p_ref[slot, 2 * g + par, pair * BLK:(pair + 1) * BLK, :] = jnp.exp2(lg - m).astype(_BF16)
                    sink_terms[pair][par] = jnp.exp2(sink - m)
            for pair in range(2):
                st_ref[slot, 2 * g + pair] = jnp.where(lane_lo, sink_terms[pair][0], sink_terms[pair][1])

    def values_stage(j, ya_slot):
        r0 = j * BLK
        slot = j % 2
        pairs = []
        for g in range(N_KV_HEADS):
            out = None
            for par in range(2):
                c = (4 * g + 2 * par) * LANES
                o = _dot(p_ref[slot, 2 * g + par],
                         band(vx_ref, vb0_ref, j, slice(c, c + 2 * LANES)))
                out = o if out is None else out + o
            for pair in range(2):
                blk = out[pair * BLK:(pair + 1) * BLK, :]
                den = blk[:, LANES:2 * LANES] + st_ref[slot, 2 * g + pair]
                pairs.append(blk[:, 0:LANES] / den)
        ya = jnp.concatenate(pairs, axis=1)
        yan_ref[ya_slot, r0:r0 + BLK, :] = _rms(ya, ga_ref[...]).astype(_BF16)

    def out_proj_part(j, ya_slot):
        nt = j // 2
        cols = slice(nt * OUT_TILE, (nt + 1) * OUT_TILE)
        if j % 2 == 0:
            acc_ref[...] = x_ref[:, cols] + _dot(yc_ref[...], wo_ref[0:CONV_WIDTH, cols])
        else:
            h = acc_ref[...] + _dot(yan_ref[ya_slot], wo_ref[CONV_WIDTH:CONV_WIDTH + ATTN_WIDTH, cols])
            _store_interleaved(o_ref, h, nt * OUT_TILE // LANES)

    def step_body(slot_attn, slot_proj):
        fill_band0()
        scores_stage(0)
        for j in range(nblk):
            if j + 1 < nblk:
                scores_stage(j + 1)
            out_proj_part(j, slot_proj)
            values_stage(j, slot_attn)

    @pl.when(jnp.logical_and(i % 2 == 0, i < n_tiles))
    def _():
        step_body(0, 1)

    @pl.when(jnp.logical_and(i % 2 == 1, i < n_tiles))
    def _():
        step_body(1, 0)

    @pl.when(i == n_tiles)
    def _():
        for j in range(nblk):
            out_proj_part(j, (n_tiles - 1) % 2)


def _attn_out(tab, sinks, q, kx, vx, yc, x2, bkt, g_attn, w_out, tiles_per_seq):
    n = x2.shape[0]
    ts = ROW_TILE
    bpt = ts // BLK
    steps = n // ts
    assert bpt == 2 * (D_MODEL // OUT_TILE)
    smem = pl.BlockSpec(memory_space=pltpu.SMEM)
    attn_tile = lambda i: jnp.minimum(i, steps - 1)
    proj_tile = lambda i: jnp.maximum(i - 1, 0)
    attn_spec = lambda w: pl.BlockSpec((ts, w), lambda i: (attn_tile(i), 0))
    proj_spec = lambda w: pl.BlockSpec((ts, w), lambda i: (proj_tile(i), 0))
    prev_spec = lambda w: pl.BlockSpec((BLK, w), lambda i: (jnp.maximum(attn_tile(i) * bpt - 1, 0), 0))
    return pl.pallas_call(
        functools.partial(_attn_out_kernel, tiles_per_seq, steps),
        grid=(steps + 1,),
        in_specs=[
            smem,
            smem,
            attn_spec(ATTN_WIDTH),
            attn_spec(KX_WIDTH),
            prev_spec(KX_WIDTH),
            attn_spec(VX_WIDTH),
            prev_spec(VX_WIDTH),
            proj_spec(CONV_WIDTH),
            proj_spec(D_MODEL),
            _const_spec((BLK, 2 * BLK)),
            _const_spec((1, ATTN_WIDTH)),
            _const_spec((CONV_WIDTH + ATTN_WIDTH, D_MODEL)),
        ],
        out_specs=pl.BlockSpec((D_MODEL // LANES, ts, LANES), lambda i: (0, proj_tile(i), 0)),
        out_shape=jax.ShapeDtypeStruct((D_MODEL // LANES, n, LANES), _F32),
        scratch_shapes=[
            pltpu.VMEM((2, N_HEADS, BLK, 2 * BLK), _F32),
            pltpu.VMEM((2 * BLK, KX_WIDTH), _BF16),
            pltpu.VMEM((2 * BLK, VX_WIDTH), _BF16),
            pltpu.VMEM((2, ts, ATTN_WIDTH), _BF16),
            pltpu.VMEM((ts, OUT_TILE), _F32),
            pltpu.VMEM((2, 2 * N_KV_HEADS, 2 * BLK, 2 * BLK), _BF16),
            pltpu.VMEM((2, 2 * N_KV_HEADS, BLK, LANES), _F32),
        ],
        compiler_params=pltpu.CompilerParams(
            dimension_semantics=("arbitrary",), vmem_limit_bytes=VMEM_LIMIT_BYTES),
        name="attn_out",
    )(tab, sinks, q, kx, kx, vx, vx, yc, x2, bkt, g_attn, w_out)


def _ffn_kernel(tiles_per_seq, h_ref, g_ref, wu_ref, cw_ref, cb_ref, wd_ref, o_ref,
                u_ref, ubuf, carry_ref, act_ref, res_ref):
    ts = o_ref.shape[0]
    i = pl.program_id(0)
    g2 = 2 * SUBLANES

    @pl.when(i % tiles_per_seq == 0)
    def _():
        carry_ref[...] = jnp.zeros(carry_ref.shape, _F32)

    h = jnp.concatenate([h_ref[l] for l in range(D_MODEL // LANES)], axis=1)
    u_ref[...] = _rms(h, g_ref[...]).astype(_BF16)
    first_sublane = lax.broadcasted_iota(jnp.int32, (SUBLANES, FF_CHUNK), 0) == 0

    def conv_half(c, part):
        slot = 2 * c + part
        buf = ubuf.at[slot % ubuf.shape[0]]
        cols = slice(part * D_FF + c * FF_CHUNK, part * D_FF + (c + 1) * FF_CHUNK)
        buf[g2:g2 + ts, :] = _dot(u_ref[...], wu_ref[:, cols])
        for k in range(2):
            cur = pltpu.roll(buf[ts + k * SUBLANES:ts + (k + 1) * SUBLANES, :], 1, axis=0)
            prev = pltpu.roll(carry_ref[slot, k * SUBLANES:(k + 1) * SUBLANES, :], 1, axis=0)
            buf[k * SUBLANES:(k + 1) * SUBLANES, :] = jnp.where(first_sublane, prev, cur)
        a = (cw_ref[0:1, cols] * buf[0:ts, :]
             + cw_ref[1:2, cols] * buf[SUBLANES:SUBLANES + ts, :]
             + cw_ref[2:3, cols] * buf[g2:g2 + ts, :] + cb_ref[:, cols])
        carry_ref[slot] = buf[ts:ts + g2, :]
        return a

    for c in range(N_FF_CHUNKS):
        gate = conv_half(c, 0)
        val = conv_half(c, 1)
        act_ref[:, c * FF_CHUNK:(c + 1) * FF_CHUNK] = (gate / (1.0 + jnp.exp(-gate)) * val).astype(_BF16)

    res = h + _dot(act_ref[...], wd_ref[...])
    for l in range(D_MODEL // LANES):
        res_ref[l] = res[:, l * LANES:(l + 1) * LANES]
    _load_deinterleaved(res_ref, o_ref)


def _ffn(h, g_ffn, w_up, conv_w, conv_b, w_down, tiles_per_seq):
    n = h.shape[1]
    ts = ROW_TILE
    g2 = 2 * SUBLANES
    return pl.pallas_call(
        functools.partial(_ffn_kernel, tiles_per_seq),
        grid=(n // ts,),
        in_specs=[
            pl.BlockSpec((D_MODEL // LANES, ts, LANES), lambda i: (0, i, 0)),
            _const_spec((1, D_MODEL)),
            _const_spec((D_MODEL, 2 * D_FF)),
            _const_spec((3, 2 * D_FF)),
            _const_spec((1, 2 * D_FF)),
            _const_spec((D_FF, D_MODEL)),
        ],
        out_specs=pl.BlockSpec((ts, D_MODEL), lambda i: (i, 0)),
        out_shape=jax.ShapeDtypeStruct((n, D_MODEL), _F32),
        scratch_shapes=[
            pltpu.VMEM((ts, D_MODEL), _BF16),
            pltpu.VMEM((4, ts + g2, FF_CHUNK), _F32),
            pltpu.VMEM((2 * N_FF_CHUNKS, g2, FF_CHUNK), _F32),
            pltpu.VMEM((ts, D_FF), _BF16),
            pltpu.VMEM((D_MODEL // LANES, ts, LANES), _F32),
        ],
        compiler_params=pltpu.CompilerParams(
            dimension_semantics=("arbitrary",), vmem_limit_bytes=VMEM_LIMIT_BYTES),
        name="ffn",
    )(h, g_ffn, w_up, conv_w, conv_b, w_down)


def kernel(x, norm_mix_g, w_in, conv_w, q_norm_g, k_norm_g, rel_bias_table, sinks, out_norm_conv_g,
           out_norm_attn_g, w_out, norm_ffn_g, w_up, ffn_conv_w, ffn_conv_b, w_down):
    bsz, seq, d = x.shape
    assert d == D_MODEL and seq % ROW_TILE == 0 and norm_mix_g.shape[0] == 1
    tiles_per_seq = seq // ROW_TILE
    n = bsz * seq
    x2 = x.reshape(n, d)
    bkt = jnp.asarray(_bucket_map())
    hm = _head_mean_matrix()

    yc, q, kx, vx, w_out_b, w_up_b, w_down_b = _mix_in(
        x2, norm_mix_g[0][None, :], w_in[0].astype(_BF16), conv_w[0],
        jnp.tile(q_norm_g[0], N_HEADS)[None, :], jnp.tile(k_norm_g[0], N_KV_HEADS)[None, :],
        out_norm_conv_g[0][None, :], hm, w_out[0], w_up[0], w_down[0], tiles_per_seq)
    h = _attn_out(rel_bias_table, sinks[0], q, kx, vx, yc, x2, bkt, out_norm_attn_g[0][None, :],
                  w_out_b, tiles_per_seq)
    out = _ffn(h, norm_ffn_g[0][None, :], w_up_b, ffn_conv_w[0], ffn_conv_b[0][None, :],
               w_down_b, tiles_per_seq)
    return out.reshape(bsz, seq, d)
```

```python
import functools
import math

import jax
import jax.numpy as jnp
import numpy as np
from jax import lax
from jax.experimental import pallas as pl
from jax.experimental.pallas import tpu as pltpu

D_MODEL = 1024
CONV_WIDTH = 512
HEAD_DIM = 64
N_HEADS = 8
N_KV_HEADS = 2
GQA_GROUP = N_HEADS // N_KV_HEADS
ATTN_WIDTH = N_HEADS * HEAD_DIM
KV_WIDTH = N_KV_HEADS * HEAD_DIM
WINDOW = 128
BLK = 128
NUM_BUCKETS = 32
MAX_DISTANCE = 128
MAX_EXACT = NUM_BUCKETS // 2
D_FF = 2816
EPS = 1e-6
NEG_INF = -1e30
LOG2E = math.log2(math.e)
IN_WIDTH = 3 * CONV_WIDTH + ATTN_WIDTH + 2 * KV_WIDTH

LANES = 128
SUBLANES = 8
MXU_TILE = 256
PAIR = 2 * HEAD_DIM
assert PAIR == LANES and KV_WIDTH == LANES and GQA_GROUP == 4
KX_WIDTH = 2 * N_KV_HEADS * LANES
VX_WIDTH = 2 * KX_WIDTH

ROW_TILE = 1024
FF_CHUNK = MXU_TILE
N_FF_CHUNKS = D_FF // FF_CHUNK
HALO = 8
VMEM_LIMIT_BYTES = 60 * 1024 * 1024

_BF16 = jnp.bfloat16
_F32 = jnp.float32


def _dot(a, b):
    return jnp.dot(a, b, preferred_element_type=_F32)


def _dot_nt(a, b):
    return lax.dot_general(a, b, (((1,), (1,)), ((), ())), preferred_element_type=_F32)


def _rms(xf, g):
    return xf * lax.rsqrt(jnp.mean(xf * xf, axis=-1, keepdims=True) + EPS) * g


def _bucket_map():
    q = np.arange(BLK, dtype=np.int32)[:, None]
    j = np.arange(2 * BLK, dtype=np.int32)[None, :]
    d = q + BLK - j
    n = np.maximum(d, 0)
    nf = np.maximum(n, 1).astype(np.float32)
    large = MAX_EXACT + (np.log(nf / MAX_EXACT) / math.log(MAX_DISTANCE / MAX_EXACT)
                         * (NUM_BUCKETS - MAX_EXACT)).astype(np.int32)
    large = np.minimum(large, NUM_BUCKETS - 1)
    bucket = np.where(n < MAX_EXACT, n, large).astype(np.int32)
    within = (d >= 0) & (d < WINDOW)
    return np.where(within, bucket, -1).astype(np.int32)


def _head_mean_matrix():
    i = np.arange(ATTN_WIDTH)
    m = (i[:, None] // HEAD_DIM == i[None, :] // HEAD_DIM).astype(np.float32) / HEAD_DIM
    return jnp.asarray(m, dtype=_BF16)


def _const_spec(shape):
    return pl.BlockSpec(shape, lambda i: (0,) * len(shape), pipeline_mode=pl.Buffered(1))


def _store_interleaved(slab_ref, val, first_slab):
    ts = val.shape[0]
    seg = ts // SUBLANES
    for l in range(val.shape[1] // LANES):
        for s in range(SUBLANES):
            slab_ref[first_slab + l, pl.ds(s, seg, stride=SUBLANES), :] = (
                val[s * seg:(s + 1) * seg, l * LANES:(l + 1) * LANES])


def _load_deinterleaved(slab_ref, out_ref):
    ts = out_ref.shape[0]
    seg = ts // SUBLANES
    for l in range(D_MODEL // LANES):
        for s in range(SUBLANES):
            out_ref[s * seg:(s + 1) * seg, l * LANES:(l + 1) * LANES] = slab_ref[l, pl.ds(s, seg, stride=SUBLANES), :]


def _lo_hi_layout(a, lo):
    ar = pltpu.roll(a, HEAD_DIM, axis=1)
    zero = jnp.zeros_like(a)
    return [jnp.where(lo, a, zero), jnp.where(lo, zero, ar), jnp.where(lo, ar, zero), jnp.where(lo, zero, a)]


def _mix_attn_kernel(tiles_per_seq, tab_ref, sink_ref, x_ref, g_ref, w_ref, cw_ref, gq_ref, gk_ref, gc_ref,
                     hm_ref, bkt_ref, ga_ref, wo_ref, wu_f32, wd_f32,
                     o_ref, wu_bf16, wd_bf16,
                     bias_ref, q_ref, kx_ref, vx_ref, kb0_ref, vb0_ref, gatec_ref, gateb_ref, zbuf,
                     yan_ref, p_ref, st_ref):
    ts = x_ref.shape[0]
    nblk = ts // BLK
    i = pl.program_id(0)
    seq_start = (i % tiles_per_seq == 0)

    wu_bf16[...] = wu_f32[...].astype(_BF16)
    wd_bf16[...] = wd_f32[...].astype(_BF16)

    @pl.when(i == 0)
    def _():
        bkt = bkt_ref[...]
        col = lax.broadcasted_iota(jnp.int32, (BLK, 2 * BLK), 1)

        def head_body(h, carry):
            b = jnp.full((BLK, 2 * BLK), NEG_INF, _F32)
            for t in range(NUM_BUCKETS):
                b = jnp.where(bkt == t, tab_ref[t, h] * LOG2E, b)
            bias_ref[0, h] = b
            bias_ref[1, h] = jnp.where(col >= BLK, b, NEG_INF)
            return carry

        lax.fori_loop(0, N_HEADS, head_body, 0)
        kx_ref[ts - BLK:ts, :] = jnp.zeros((BLK, KX_WIDTH), _BF16)
        vx_ref[ts - BLK:ts, :] = jnp.zeros((BLK, VX_WIDTH), _BF16)

    @pl.when(seq_start)
    def _():
        zbuf[0:HALO, :] = jnp.zeros((HALO, CONV_WIDTH), _F32)

    kb0_ref[0:BLK, :] = kx_ref[ts - BLK:ts, :]
    vb0_ref[0:BLK, :] = vx_ref[ts - BLK:ts, :]

    u = _rms(x_ref[...], g_ref[...]).astype(_BF16)
    c0 = 3 * CONV_WIDTH
    c1 = c0 + ATTN_WIDTH
    q = _dot(u, w_ref[:, c0:c0 + ATTN_WIDTH])
    kv = _dot(u, w_ref[:, c1:c1 + 2 * KV_WIDTH])
    k = kv[:, 0:KV_WIDTH]
    gatec_ref[...] = _dot(u, w_ref[:, CONV_WIDTH:2 * CONV_WIDTH])
    msq = _dot((q * q).astype(_BF16), hm_ref[...])
    msk = _dot((k * k).astype(_BF16), hm_ref[0:KV_WIDTH, 0:KV_WIDTH])

    q_ref[...] = (q * lax.rsqrt(msq + EPS) * gq_ref[...] * (HEAD_DIM ** -0.5 * LOG2E)).astype(_BF16)
    kn = k * lax.rsqrt(msk + EPS) * gk_ref[...]
    lo = lax.broadcasted_iota(jnp.int32, (ts, LANES), 1) < HEAD_DIM
    kx_ref[...] = jnp.concatenate(_lo_hi_layout(kn, lo), axis=1).astype(_BF16)
    ones = [jnp.where(lo, 1.0, 0.0).astype(_F32), jnp.where(lo, 0.0, 1.0).astype(_F32)]
    v_pieces = _lo_hi_layout(kv[:, KV_WIDTH:2 * KV_WIDTH], lo)
    vx_ref[...] = jnp.concatenate([piece for s, v in enumerate(v_pieces) for piece in (v, ones[s % 2])],
                                  axis=1).astype(_BF16)
    kb0_ref[BLK:2 * BLK, :] = kx_ref[0:BLK, :]
    vb0_ref[BLK:2 * BLK, :] = vx_ref[0:BLK, :]

    lane_lo = lax.broadcasted_iota(jnp.int32, (BLK, LANES), 1) < HEAD_DIM

    def band(tile_ref, band0_ref, j, cols):
        if j == 0:
            return band0_ref[:, cols]
        return tile_ref[(j - 1) * BLK:(j + 1) * BLK, cols]

    def scores_stage(j):
        r0 = j * BLK
        slot = j % 2
        first = seq_start.astype(jnp.int32) if j == 0 else 0
        for g in range(N_KV_HEADS):
            qq = q_ref[pl.ds(r0, BLK), g * 2 * LANES:(g + 1) * 2 * LANES]
            lhs = jnp.concatenate([qq[:, 0:LANES], qq[:, LANES:2 * LANES]], axis=0)
            sink_terms = [[None, None], [None, None]]
            for par in range(2):
                kb = band(kx_ref, kb0_ref, j, slice((2 * g + par) * LANES, (2 * g + par + 1) * LANES))
                logits = _dot_nt(lhs, kb)
                for pair in range(2):
                    h = g * GQA_GROUP + 2 * pair + par
                    lg = logits[pair * BLK:(pair + 1) * BLK, :] + bias_ref[first, h]
                    sink = sink_ref[h] * LOG2E
                    m = jnp.maximum(jnp.max(lg, axis=-1, keepdims=True), sink)
                    p_ref[slot, 2 * g + par, pair * BLK:(pair + 1) * BLK, :] = jnp.exp2(lg - m).astype(_BF16)
                    sink_terms[pair][par] = jnp.exp2(sink - m)
            for pair in range(2):
                st_ref[slot, 2 * g + pair] = jnp.where(lane_lo, sink_terms[pair][0], sink_terms[pair][1])

    def values_stage(j):
        r0 = j * BLK
        slot = j % 2
        pairs = []
        for g in range(N_KV_HEADS):
            out = None
            for par in range(2):
                c = (4 * g + 2 * par) * LANES
                o = _dot(p_ref[slot, 2 * g + par],
                         band(vx_ref, vb0_ref, j, slice(c, c + 2 * LANES)))
                out = o if out is None else out + o
            for pair in range(2):
                blk = out[pair * BLK:(pair + 1) * BLK, :]
                den = blk[:, LANES:2 * LANES] + st_ref[slot, 2 * g + pair]
                pairs.append(blk[:, 0:LANES] / den)
        ya = jnp.concatenate(pairs, axis=1)
        yan_ref[r0:r0 + BLK, :] = _rms(ya, ga_ref[...]).astype(_BF16)

    def conv_piece(piece):
        half = piece % 2
        cols = slice(half * MXU_TILE, (half + 1) * MXU_TILE)
        if piece < 2:
            hc = _dot(u, w_ref[:, 2 * CONV_WIDTH + half * MXU_TILE:2 * CONV_WIDTH + (half + 1) * MXU_TILE])
            zbuf[HALO:HALO + ts, cols] = gatec_ref[:, cols] * hc
        else:
            gateb_ref[:, cols] = _dot(u, w_ref[:, cols])

    conv_piece(0)
    scores_stage(0)
    for j in range(nblk):
        if j + 1 < nblk:
            scores_stage(j + 1)
        if j % 2 == 0 and 1 + j // 2 < 4:
            conv_piece(1 + j // 2)
        values_stage(j)

    conv = (cw_ref[0:1, :] * zbuf[HALO - 2:HALO - 2 + ts, :]
            + cw_ref[1:2, :] * zbuf[HALO - 1:HALO - 1 + ts, :]
            + cw_ref[2:3, :] * zbuf[HALO:HALO + ts, :])
    yc = _rms(gateb_ref[...] * conv, gc_ref[...]).astype(_BF16)
    zbuf[0:HALO, :] = zbuf[ts:ts + HALO, :]

    h = (x_ref[...] + _dot(yan_ref[...], wo_ref[CONV_WIDTH:CONV_WIDTH + ATTN_WIDTH, :])
         + _dot(yc, wo_ref[0:CONV_WIDTH, :]))
    _store_interleaved(o_ref, h, 0)


def _mix_attn(tab, sinks, x2, g_mix, w_in, conv_w, gq, gk, g_conv, hm, bkt, g_attn, w_out, w_up, w_down,
              tiles_per_seq):
    n = x2.shape[0]
    ts = ROW_TILE
    steps = n // ts
    assert ts // BLK >= 8 and CONV_WIDTH == 2 * MXU_TILE
    smem = pl.BlockSpec(memory_space=pltpu.SMEM)

    def slice_spec(w):
        rows, cols = w.shape
        assert rows % (steps * 2 * SUBLANES) == 0
        return pl.BlockSpec((rows // steps, cols), lambda i: (i, 0))

    cast_weights = (w_up, w_down)
    return pl.pallas_call(
        functools.partial(_mix_attn_kernel, tiles_per_seq),
        grid=(steps,),
        in_specs=[
            smem,
            smem,
            pl.BlockSpec((ts, D_MODEL), lambda i: (i, 0)),
            _const_spec((1, D_MODEL)),
            _const_spec((D_MODEL, IN_WIDTH)),
            _const_spec((3, CONV_WIDTH)),
            _const_spec((1, ATTN_WIDTH)),
            _const_spec((1, KV_WIDTH)),
            _const_spec((1, CONV_WIDTH)),
            _const_spec((ATTN_WIDTH, ATTN_WIDTH)),
            _const_spec((BLK, 2 * BLK)),
            _const_spec((1, ATTN_WIDTH)),
            _const_spec((CONV_WIDTH + ATTN_WIDTH, D_MODEL)),
        ] + [slice_spec(w) for w in cast_weights],
        out_specs=[pl.BlockSpec((D_MODEL // LANES, ts, LANES), lambda i: (0, i, 0))]
        + [slice_spec(w) for w in cast_weights],
        out_shape=[jax.ShapeDtypeStruct((D_MODEL // LANES, n, LANES), _F32)]
        + [jax.ShapeDtypeStruct(w.shape, _BF16) for w in cast_weights],
        scratch_shapes=[
            pltpu.VMEM((2, N_HEADS, BLK, 2 * BLK), _F32),
            pltpu.VMEM((ts, ATTN_WIDTH), _BF16),
            pltpu.VMEM((ts, KX_WIDTH), _BF16),
            pltpu.VMEM((ts, VX_WIDTH), _BF16),
            pltpu.VMEM((2 * BLK, KX_WIDTH), _BF16),
            pltpu.VMEM((2 * BLK, VX_WIDTH), _BF16),
            pltpu.VMEM((ts, CONV_WIDTH), _F32),
            pltpu.VMEM((ts, CONV_WIDTH), _F32),
            pltpu.VMEM((ts + HALO, CONV_WIDTH), _F32),
            pltpu.VMEM((ts, ATTN_WIDTH), _BF16),
            pltpu.VMEM((2, 2 * N_KV_HEADS, 2 * BLK, 2 * BLK), _BF16),
            pltpu.VMEM((2, 2 * N_KV_HEADS, BLK, LANES), _F32),
        ],
        compiler_params=pltpu.CompilerParams(
            dimension_semantics=("arbitrary",), vmem_limit_bytes=VMEM_LIMIT_BYTES),
        name="mix_attn",
    )(tab, sinks, x2, g_mix, w_in, conv_w, gq, gk, g_conv, hm, bkt, g_attn, w_out, *cast_weights)


def _ffn_kernel(tiles_per_seq, h_ref, g_ref, wu_ref, cw_ref, cb_ref, wd_ref, o_ref,
                u_ref, ubuf, carry_ref, act_ref, res_ref):
    ts = o_ref.shape[0]
    i = pl.program_id(0)
    g2 = 2 * SUBLANES

    @pl.when(i % tiles_per_seq == 0)
    def _():
        carry_ref[...] = jnp.zeros(carry_ref.shape, _F32)

    h = jnp.concatenate([h_ref[l] for l in range(D_MODEL // LANES)], axis=1)
    u_ref[...] = _rms(h, g_ref[...]).astype(_BF16)
    first_sublane = lax.broadcasted_iota(jnp.int32, (SUBLANES, FF_CHUNK), 0) == 0

    def conv_half(c, part):
        slot = 2 * c + part
        buf = ubuf.at[slot % ubuf.shape[0]]
        cols = slice(part * D_FF + c * FF_CHUNK, part * D_FF + (c + 1) * FF_CHUNK)
        buf[g2:g2 + ts, :] = _dot(u_ref[...], wu_ref[:, cols])
        for k in range(2):
            cur = pltpu.roll(buf[ts + k * SUBLANES:ts + (k + 1) * SUBLANES, :], 1, axis=0)
            prev = pltpu.roll(carry_ref[slot, k * SUBLANES:(k + 1) * SUBLANES, :], 1, axis=0)
            buf[k * SUBLANES:(k + 1) * SUBLANES, :] = jnp.where(first_sublane, prev, cur)
        a = (cw_ref[0:1, cols] * buf[0:ts, :]
             + cw_ref[1:2, cols] * buf[SUBLANES:SUBLANES + ts, :]
             + cw_ref[2:3, cols] * buf[g2:g2 + ts, :] + cb_ref[:, cols])
        carry_ref[slot] = buf[ts:ts + g2, :]
        return a

    for c in range(N_FF_CHUNKS):
        gate = conv_half(c, 0)
        val = conv_half(c, 1)
        act_ref[:, c * FF_CHUNK:(c + 1) * FF_CHUNK] = (gate / (1.0 + jnp.exp(-gate)) * val).astype(_BF16)

    res = h + _dot(act_ref[...], wd_ref[...])
    for l in range(D_MODEL // LANES):
        res_ref[l] = res[:, l * LANES:(l + 1) * LANES]
    _load_deinterleaved(res_ref, o_ref)


def _ffn(h, g_ffn, w_up, conv_w, conv_b, w_down, tiles_per_seq):
    n = h.shape[1]
    ts = ROW_TILE
    g2 = 2 * SUBLANES
    return pl.pallas_call(
        functools.partial(_ffn_kernel, tiles_per_seq),
        grid=(n // ts,),
        in_specs=[
            pl.BlockSpec((D_MODEL // LANES, ts, LANES), lambda i: (0, i, 0)),
            _const_spec((1, D_MODEL)),
            _const_spec((D_MODEL, 2 * D_FF)),
            _const_spec((3, 2 * D_FF)),
            _const_spec((1, 2 * D_FF)),
            _const_spec((D_FF, D_MODEL)),
        ],
        out_specs=pl.BlockSpec((ts, D_MODEL), lambda i: (i, 0)),
        out_shape=jax.ShapeDtypeStruct((n, D_MODEL), _F32),
        scratch_shapes=[
            pltpu.VMEM((ts, D_MODEL), _BF16),
            pltpu.VMEM((4, ts + g2, FF_CHUNK), _F32),
            pltpu.VMEM((2 * N_FF_CHUNKS, g2, FF_CHUNK), _F32),
            pltpu.VMEM((ts, D_FF), _BF16),
            pltpu.VMEM((D_MODEL // LANES, ts, LANES), _F32),
        ],
        compiler_params=pltpu.CompilerParams(
            dimension_semantics=("arbitrary",), vmem_limit_bytes=VMEM_LIMIT_BYTES),
        name="ffn",
    )(h, g_ffn, w_up, conv_w, conv_b, w_down)


def kernel(x, norm_mix_g, w_in, conv_w, q_norm_g, k_norm_g, rel_bias_table, sinks, out_norm_conv_g,
           out_norm_attn_g, w_out, norm_ffn_g, w_up, ffn_conv_w, ffn_conv_b, w_down):
    bsz, seq, d = x.shape
    assert d == D_MODEL and seq % ROW_TILE == 0 and norm_mix_g.shape[0] == 1
    tiles_per_seq = seq // ROW_TILE
    n = bsz * seq
    x2 = x.reshape(n, d)
    bkt = jnp.asarray(_bucket_map())
    hm = _head_mean_matrix()

    h, w_up_b, w_down_b = _mix_attn(
        rel_bias_table, sinks[0], x2, norm_mix_g[0][None, :], w_in[0].astype(_BF16), conv_w[0],
        jnp.tile(q_norm_g[0], N_HEADS)[None, :], jnp.tile(k_norm_g[0], N_KV_HEADS)[None, :],
        out_norm_conv_g[0][None, :], hm, bkt, out_norm_attn_g[0][None, :], w_out[0].astype(_BF16),
        w_up[0], w_down[0], tiles_per_seq)
    out = _ffn(h, norm_ffn_g[0][None, :], w_up_b, ffn_conv_w[0], ffn_conv_b[0][None, :],
               w_down_b, tiles_per_seq)
    return out.reshape(bsz, seq, d)
```

```python
import functools
import math

import jax
import jax.numpy as jnp
import numpy as np
from jax import lax
from jax.experimental import pallas as pl
from jax.experimental.pallas import tpu as pltpu

D_MODEL = 1024
CONV_WIDTH = 512
HEAD_DIM = 64
N_HEADS = 8
N_KV_HEADS = 2
GQA_GROUP = N_HEADS // N_KV_HEADS
ATTN_WIDTH = N_HEADS * HEAD_DIM
KV_WIDTH = N_KV_HEADS * HEAD_DIM
WINDOW = 128
BLK = 128
NUM_BUCKETS = 32
MAX_DISTANCE = 128
MAX_EXACT = NUM_BUCKETS // 2
D_FF = 2816
EPS = 1e-6
NEG_INF = -1e30
LOG2E = math.log2(math.e)
IN_WIDTH = 3 * CONV_WIDTH + ATTN_WIDTH + 2 * KV_WIDTH

LANES = 128
SUBLANES = 8
MXU_TILE = 256
PAIR = 2 * HEAD_DIM
assert PAIR == LANES and KV_WIDTH == LANES and GQA_GROUP == 4
KX_WIDTH = 2 * N_KV_HEADS * LANES
VX_WIDTH = 2 * KX_WIDTH

ROW_TILE = 1024
FF_CHUNK = MXU_TILE
N_FF_CHUNKS = D_FF // FF_CHUNK
HALO = 8
LOOKAHEAD = 2
VMEM_LIMIT_BYTES = 60 * 1024 * 1024

_BF16 = jnp.bfloat16
_F32 = jnp.float32


def _dot(a, b):
    return jnp.dot(a, b, preferred_element_type=_F32)


def _dot_nt(a, b):
    return lax.dot_general(a, b, (((1,), (1,)), ((), ())), preferred_element_type=_F32)


def _rms(xf, g):
    return xf * lax.rsqrt(jnp.mean(xf * xf, axis=-1, keepdims=True) + EPS) * g


def _bucket_map():
    q = np.arange(BLK, dtype=np.int32)[:, None]
    j = np.arange(2 * BLK, dtype=np.int32)[None, :]
    d = q + BLK - j
    n = np.maximum(d, 0)
    nf = np.maximum(n, 1).astype(np.float32)
    large = MAX_EXACT + (np.log(nf / MAX_EXACT) / math.log(MAX_DISTANCE / MAX_EXACT)
                         * (NUM_BUCKETS - MAX_EXACT)).astype(np.int32)
    large = np.minimum(large, NUM_BUCKETS - 1)
    bucket = np.where(n < MAX_EXACT, n, large).astype(np.int32)
    within = (d >= 0) & (d < WINDOW)
    return np.where(within, bucket, -1).astype(np.int32)


def _head_mean_matrix():
    i = np.arange(ATTN_WIDTH)
    m = (i[:, None] // HEAD_DIM == i[None, :] // HEAD_DIM).astype(np.float32) / HEAD_DIM
    return jnp.asarray(m, dtype=_BF16)


def _const_spec(shape):
    return pl.BlockSpec(shape, lambda i: (0,) * len(shape), pipeline_mode=pl.Buffered(1))


def _store_interleaved(slab_ref, val, first_slab):
    ts = val.shape[0]
    seg = ts // SUBLANES
    for l in range(val.shape[1] // LANES):
        for s in range(SUBLANES):
            slab_ref[first_slab + l, pl.ds(s, seg, stride=SUBLANES), :] = (
                val[s * seg:(s + 1) * seg, l * LANES:(l + 1) * LANES])


def _load_deinterleaved(slab_ref, out_ref):
    ts = out_ref.shape[0]
    seg = ts // SUBLANES
    for l in range(D_MODEL // LANES):
        for s in range(SUBLANES):
            out_ref[s * seg:(s + 1) * seg, l * LANES:(l + 1) * LANES] = slab_ref[l, pl.ds(s, seg, stride=SUBLANES), :]


def _lo_hi_layout(a, lo):
    ar = pltpu.roll(a, HEAD_DIM, axis=1)
    zero = jnp.zeros_like(a)
    return [jnp.where(lo, a, zero), jnp.where(lo, zero, ar), jnp.where(lo, ar, zero), jnp.where(lo, zero, a)]


def _mix_attn_kernel(tiles_per_seq, tab_ref, sink_ref, x_ref, g_ref, w_ref, cw_ref, gq_ref, gk_ref, gc_ref,
                     hm_ref, bkt_ref, ga_ref, wo_f32, wu_f32, wd_f32,
                     o_ref, wu_bf16, wd_bf16,
                     bias_ref, wo_ref, q_ref, kx_ref, vx_ref, kb0_ref, vb0_ref, gatec_ref, gateb_ref, zbuf,
                     yan_ref, p_ref, st_ref):
    ts = x_ref.shape[0]
    nblk = ts // BLK
    i = pl.program_id(0)
    seq_start = (i % tiles_per_seq == 0)

    wu_bf16[...] = wu_f32[...].astype(_BF16)
    wd_bf16[...] = wd_f32[...].astype(_BF16)

    @pl.when(i == 0)
    def _():
        bkt = bkt_ref[...]
        col = lax.broadcasted_iota(jnp.int32, (BLK, 2 * BLK), 1)

        def head_body(h, carry):
            b = jnp.full((BLK, 2 * BLK), NEG_INF, _F32)
            for t in range(NUM_BUCKETS):
                b = jnp.where(bkt == t, tab_ref[t, h] * LOG2E, b)
            bias_ref[0, h] = b
            bias_ref[1, h] = jnp.where(col >= BLK, b, NEG_INF)
            return carry

        lax.fori_loop(0, N_HEADS, head_body, 0)
        wo_ref[...] = wo_f32[...].astype(_BF16)
        kx_ref[ts - BLK:ts, :] = jnp.zeros((BLK, KX_WIDTH), _BF16)
        vx_ref[ts - BLK:ts, :] = jnp.zeros((BLK, VX_WIDTH), _BF16)

    @pl.when(seq_start)
    def _():
        zbuf[0:HALO, :] = jnp.zeros((HALO, CONV_WIDTH), _F32)

    kb0_ref[0:BLK, :] = kx_ref[ts - BLK:ts, :]
    vb0_ref[0:BLK, :] = vx_ref[ts - BLK:ts, :]

    u = _rms(x_ref[...], g_ref[...]).astype(_BF16)
    c0 = 3 * CONV_WIDTH
    c1 = c0 + ATTN_WIDTH
    q = _dot(u, w_ref[:, c0:c0 + ATTN_WIDTH])
    kv = _dot(u, w_ref[:, c1:c1 + 2 * KV_WIDTH])
    k = kv[:, 0:KV_WIDTH]
    gatec_ref[...] = _dot(u, w_ref[:, CONV_WIDTH:2 * CONV_WIDTH])
    msq = _dot((q * q).astype(_BF16), hm_ref[...])
    msk = _dot((k * k).astype(_BF16), hm_ref[0:KV_WIDTH, 0:KV_WIDTH])

    q_ref[...] = (q * lax.rsqrt(msq + EPS) * gq_ref[...] * (HEAD_DIM ** -0.5 * LOG2E)).astype(_BF16)
    kn = k * lax.rsqrt(msk + EPS) * gk_ref[...]
    lo = lax.broadcasted_iota(jnp.int32, (ts, LANES), 1) < HEAD_DIM
    kx_ref[...] = jnp.concatenate(_lo_hi_layout(kn, lo), axis=1).astype(_BF16)
    ones = [jnp.where(lo, 1.0, 0.0).astype(_F32), jnp.where(lo, 0.0, 1.0).astype(_F32)]
    v_pieces = _lo_hi_layout(kv[:, KV_WIDTH:2 * KV_WIDTH], lo)
    vx_ref[...] = jnp.concatenate([piece for s, v in enumerate(v_pieces) for piece in (v, ones[s % 2])],
                                  axis=1).astype(_BF16)
    kb0_ref[BLK:2 * BLK, :] = kx_ref[0:BLK, :]
    vb0_ref[BLK:2 * BLK, :] = vx_ref[0:BLK, :]

    lane_lo = lax.broadcasted_iota(jnp.int32, (BLK, LANES), 1) < HEAD_DIM

    def band(tile_ref, band0_ref, j, cols):
        if j == 0:
            return band0_ref[:, cols]
        return tile_ref[(j - 1) * BLK:(j + 1) * BLK, cols]

    def scores_stage(j):
        r0 = j * BLK
        slot = j % (LOOKAHEAD + 1)
        first = seq_start.astype(jnp.int32) if j == 0 else 0
        for g in range(N_KV_HEADS):
            qq = q_ref[pl.ds(r0, BLK), g * 2 * LANES:(g + 1) * 2 * LANES]
            lhs = jnp.concatenate([qq[:, 0:LANES], qq[:, LANES:2 * LANES]], axis=0)
            sink_terms = [[None, None], [None, None]]
            for par in range(2):
                kb = band(kx_ref, kb0_ref, j, slice((2 * g + par) * LANES, (2 * g + par + 1) * LANES))
                logits = _dot_nt(lhs, kb)
                for pair in range(2):
                    h = g * GQA_GROUP + 2 * pair + par
                    lg = logits[pair * BLK:(pair + 1) * BLK, :] + bias_ref[first, h]
                    sink = sink_ref[h] * LOG2E
                    m = jnp.maximum(jnp.max(lg, axis=-1, keepdims=True), sink)
                    p_ref[slot, 2 * g + par, pair * BLK:(pair + 1) * BLK, :] = jnp.exp2(lg - m).astype(_BF16)
                    sink_terms[pair][par] = jnp.exp2(sink - m)
            for pair in range(2):
                st_ref[slot, 2 * g + pair] = jnp.where(lane_lo, sink_terms[pair][0], sink_terms[pair][1])

    def values_stage(j):
        r0 = j * BLK
        slot = j % (LOOKAHEAD + 1)
        pairs = []
        for g in range(N_KV_HEADS):
            out = None
            for par in range(2):
                c = (4 * g + 2 * par) * LANES
                o = _dot(p_ref[slot, 2 * g + par],
                         band(vx_ref, vb0_ref, j, slice(c, c + 2 * LANES)))
                out = o if out is None else out + o
            for pair in range(2):
                blk = out[pair * BLK:(pair + 1) * BLK, :]
                den = blk[:, LANES:2 * LANES] + st_ref[slot, 2 * g + pair]
                pairs.append(blk[:, 0:LANES] / den)
        ya = jnp.concatenate(pairs, axis=1)
        yan_ref[r0:r0 + BLK, :] = _rms(ya, ga_ref[...]).astype(_BF16)

    def conv_piece(piece):
        half = piece % 2
        cols = slice(half * MXU_TILE, (half + 1) * MXU_TILE)
        if piece < 2:
            hc = _dot(u, w_ref[:, 2 * CONV_WIDTH + half * MXU_TILE:2 * CONV_WIDTH + (half + 1) * MXU_TILE])
            zbuf[HALO:HALO + ts, cols] = gatec_ref[:, cols] * hc
        else:
            gateb_ref[:, cols] = _dot(u, w_ref[:, cols])

    conv_piece(0)
    for j in range(LOOKAHEAD):
        scores_stage(j)
    for j in range(nblk):
        if j + LOOKAHEAD < nblk:
            scores_stage(j + LOOKAHEAD)
        if j % 2 == 0 and 1 + j // 2 < 4:
            conv_piece(1 + j // 2)
        values_stage(j)

    conv = (cw_ref[0:1, :] * zbuf[HALO - 2:HALO - 2 + ts, :]
            + cw_ref[1:2, :] * zbuf[HALO - 1:HALO - 1 + ts, :]
            + cw_ref[2:3, :] * zbuf[HALO:HALO + ts, :])
    yc = _rms(gateb_ref[...] * conv, gc_ref[...]).astype(_BF16)
    zbuf[0:HALO, :] = zbuf[ts:ts + HALO, :]

    h = (x_ref[...] + _dot(yan_ref[...], wo_ref[CONV_WIDTH:CONV_WIDTH + ATTN_WIDTH, :])
         + _dot(yc, wo_ref[0:CONV_WIDTH, :]))
    _store_interleaved(o_ref, h, 0)


def _mix_attn(tab, sinks, x2, g_mix, w_in, conv_w, gq, gk, g_conv, hm, bkt, g_attn, w_out, w_up, w_down,
              tiles_per_seq):
    n = x2.shape[0]
    ts = ROW_TILE
    steps = n // ts
    assert ts // BLK >= 8 and CONV_WIDTH == 2 * MXU_TILE
    smem = pl.BlockSpec(memory_space=pltpu.SMEM)

    def slice_spec(w):
        rows, cols = w.shape
        assert rows % (steps * 2 * SUBLANES) == 0
        return pl.BlockSpec((rows // steps, cols), lambda i: (i, 0))

    cast_weights = (w_up, w_down)
    return pl.pallas_call(
        functools.partial(_mix_attn_kernel, tiles_per_seq),
        grid=(steps,),
        in_specs=[
            smem,
            smem,
            pl.BlockSpec((ts, D_MODEL), lambda i: (i, 0)),
            _const_spec((1, D_MODEL)),
            _const_spec((D_MODEL, IN_WIDTH)),
            _const_spec((3, CONV_WIDTH)),
            _const_spec((1, ATTN_WIDTH)),
            _const_spec((1, KV_WIDTH)),
            _const_spec((1, CONV_WIDTH)),
            _const_spec((ATTN_WIDTH, ATTN_WIDTH)),
            _const_spec((BLK, 2 * BLK)),
            _const_spec((1, ATTN_WIDTH)),
            _const_spec((CONV_WIDTH + ATTN_WIDTH, D_MODEL)),
        ] + [slice_spec(w) for w in cast_weights],
        out_specs=[pl.BlockSpec((D_MODEL // LANES, ts, LANES), lambda i: (0, i, 0))]
        + [slice_spec(w) for w in cast_weights],
        out_shape=[jax.ShapeDtypeStruct((D_MODEL // LANES, n, LANES), _F32)]
        + [jax.ShapeDtypeStruct(w.shape, _BF16) for w in cast_weights],
        scratch_shapes=[
            pltpu.VMEM((2, N_HEADS, BLK, 2 * BLK), _F32),
            pltpu.VMEM((CONV_WIDTH + ATTN_WIDTH, D_MODEL), _BF16),
            pltpu.VMEM((ts, ATTN_WIDTH), _BF16),
            pltpu.VMEM((ts, KX_WIDTH), _BF16),
            pltpu.VMEM((ts, VX_WIDTH), _BF16),
            pltpu.VMEM((2 * BLK, KX_WIDTH), _BF16),
            pltpu.VMEM((2 * BLK, VX_WIDTH), _BF16),
            pltpu.VMEM((ts, CONV_WIDTH), _F32),
            pltpu.VMEM((ts, CONV_WIDTH), _F32),
            pltpu.VMEM((ts + HALO, CONV_WIDTH), _F32),
            pltpu.VMEM((ts, ATTN_WIDTH), _BF16),
            pltpu.VMEM((LOOKAHEAD + 1, 2 * N_KV_HEADS, 2 * BLK, 2 * BLK), _BF16),
            pltpu.VMEM((LOOKAHEAD + 1, 2 * N_KV_HEADS, BLK, LANES), _F32),
        ],
        compiler_params=pltpu.CompilerParams(
            dimension_semantics=("arbitrary",), vmem_limit_bytes=VMEM_LIMIT_BYTES),
        name="mix_attn",
    )(tab, sinks, x2, g_mix, w_in, conv_w, gq, gk, g_conv, hm, bkt, g_attn, w_out, *cast_weights)


def _ffn_kernel(tiles_per_seq, h_ref, g_ref, wu_ref, cw_ref, cb_ref, wd_ref, o_ref,
                u_ref, ubuf, carry_ref, act_ref, res_ref):
    ts = o_ref.shape[0]
    i = pl.program_id(0)
    g2 = 2 * SUBLANES

    @pl.when(i % tiles_per_seq == 0)
    def _():
        carry_ref[...] = jnp.zeros(carry_ref.shape, _F32)

    h = jnp.concatenate([h_ref[l] for l in range(D_MODEL // LANES)], axis=1)
    u_ref[...] = _rms(h, g_ref[...]).astype(_BF16)
    first_sublane = lax.broadcasted_iota(jnp.int32, (SUBLANES, FF_CHUNK), 0) == 0

    def conv_half(c, part):
        slot = 2 * c + part
        buf = ubuf.at[slot % ubuf.shape[0]]
        cols = slice(part * D_FF + c * FF_CHUNK, part * D_FF + (c + 1) * FF_CHUNK)
        buf[g2:g2 + ts, :] = _dot(u_ref[...], wu_ref[:, cols])
        for k in range(2):
            cur = pltpu.roll(buf[ts + k * SUBLANES:ts + (k + 1) * SUBLANES, :], 1, axis=0)
            prev = pltpu.roll(carry_ref[slot, k * SUBLANES:(k + 1) * SUBLANES, :], 1, axis=0)
            buf[k * SUBLANES:(k + 1) * SUBLANES, :] = jnp.where(first_sublane, prev, cur)
        a = (cw_ref[0:1, cols] * buf[0:ts, :]
             + cw_ref[1:2, cols] * buf[SUBLANES:SUBLANES + ts, :]
             + cw_ref[2:3, cols] * buf[g2:g2 + ts, :] + cb_ref[:, cols])
        carry_ref[slot] = buf[ts:ts + g2, :]
        return a

    for c in range(N_FF_CHUNKS):
        gate = conv_half(c, 0)
        val = conv_half(c, 1)
        act_ref[:, c * FF_CHUNK:(c + 1) * FF_CHUNK] = (gate / (1.0 + jnp.exp(-gate)) * val).astype(_BF16)

    res = h + _dot(act_ref[...], wd_ref[...])
    for l in range(D_MODEL // LANES):
        res_ref[l] = res[:, l * LANES:(l + 1) * LANES]
    _load_deinterleaved(res_ref, o_ref)


def _ffn(h, g_ffn, w_up, conv_w, conv_b, w_down, tiles_per_seq):
    n = h.shape[1]
    ts = ROW_TILE
    g2 = 2 * SUBLANES
    return pl.pallas_call(
        functools.partial(_ffn_kernel, tiles_per_seq),
        grid=(n // ts,),
        in_specs=[
            pl.BlockSpec((D_MODEL // LANES, ts, LANES), lambda i: (0, i, 0)),
            _const_spec((1, D_MODEL)),
            _const_spec((D_MODEL, 2 * D_FF)),
            _const_spec((3, 2 * D_FF)),
            _const_spec((1, 2 * D_FF)),
            _const_spec((D_FF, D_MODEL)),
        ],
        out_specs=pl.BlockSpec((ts, D_MODEL), lambda i: (i, 0)),
        out_shape=jax.ShapeDtypeStruct((n, D_MODEL), _F32),
        scratch_shapes=[
            pltpu.VMEM((ts, D_MODEL), _BF16),
            pltpu.VMEM((4, ts + g2, FF_CHUNK), _F32),
            pltpu.VMEM((2 * N_FF_CHUNKS, g2, FF_CHUNK), _F32),
            pltpu.VMEM((ts, D_FF), _BF16),
            pltpu.VMEM((D_MODEL // LANES, ts, LANES), _F32),
        ],
        compiler_params=pltpu.CompilerParams(
            dimension_semantics=("arbitrary",), vmem_limit_bytes=VMEM_LIMIT_BYTES),
        name="ffn",
    )(h, g_ffn, w_up, conv_w, conv_b, w_down)


def kernel(x, norm_mix_g, w_in, conv_w, q_norm_g, k_norm_g, rel_bias_table, sinks, out_norm_conv_g,
           out_norm_attn_g, w_out, norm_ffn_g, w_up, ffn_conv_w, ffn_conv_b, w_down):
    bsz, seq, d = x.shape
    assert d == D_MODEL and seq % ROW_TILE == 0 and norm_mix_g.shape[0] == 1
    tiles_per_seq = seq // ROW_TILE
    n = bsz * seq
    x2 = x.reshape(n, d)
    bkt = jnp.asarray(_bucket_map())
    hm = _head_mean_matrix()

    h, w_up_b, w_down_b = _mix_attn(
        rel_bias_table, sinks[0], x2, norm_mix_g[0][None, :], w_in[0].astype(_BF16), conv_w[0],
        jnp.tile(q_norm_g[0], N_HEADS)[None, :], jnp.tile(k_norm_g[0], N_KV_HEADS)[None, :],
        out_norm_conv_g[0][None, :], hm, bkt, out_norm_attn_g[0][None, :], w_out[0],
        w_up[0], w_down[0], tiles_per_seq)
    out = _ffn(h, norm_ffn_g[0][None, :], w_up_b, ffn_conv_w[0], ffn_conv_b[0][None, :],
               w_down_b, tiles_per_seq)
    return out.reshape(bsz, seq, d)
```

```python
import functools
import math

import jax
import jax.numpy as jnp
import numpy as np
from jax import lax
from jax.experimental import pallas as pl
from jax.experimental.pallas import tpu as pltpu

D_MODEL = 1024
CONV_WIDTH = 512
HEAD_DIM = 64
N_HEADS = 8
N_KV_HEADS = 2
GQA_GROUP = N_HEADS // N_KV_HEADS
ATTN_WIDTH = N_HEADS * HEAD_DIM
KV_WIDTH = N_KV_HEADS * HEAD_DIM
WINDOW = 128
BLK = 128
NUM_BUCKETS = 32
MAX_DISTANCE = 128
MAX_EXACT = NUM_BUCKETS // 2
D_FF = 2816
EPS = 1e-6
NEG_INF = -1e30
LOG2E = math.log2(math.e)
IN_WIDTH = 3 * CONV_WIDTH + ATTN_WIDTH + 2 * KV_WIDTH

LANES = 128
SUBLANES = 8
MXU_TILE = 256
PAIR = 2 * HEAD_DIM
assert PAIR == LANES and KV_WIDTH == LANES and GQA_GROUP == 4
KX_WIDTH = 2 * N_KV_HEADS * LANES
VX_WIDTH = 2 * KX_WIDTH

ROW_TILE = 1024
FF_CHUNK = MXU_TILE
N_FF_CHUNKS = D_FF // FF_CHUNK
HALO = 8
LOOKAHEAD = 2
VMEM_LIMIT_BYTES = 60 * 1024 * 1024

_BF16 = jnp.bfloat16
_F32 = jnp.float32


def _dot(a, b):
    return jnp.dot(a, b, preferred_element_type=_F32)


def _dot_nt(a, b):
    return lax.dot_general(a, b, (((1,), (1,)), ((), ())), preferred_element_type=_F32)


def _rms(xf, g):
    return xf * lax.rsqrt(jnp.mean(xf * xf, axis=-1, keepdims=True) + EPS) * g


def _bucket_map():
    q = np.arange(BLK, dtype=np.int32)[:, None]
    j = np.arange(2 * BLK, dtype=np.int32)[None, :]
    d = q + BLK - j
    n = np.maximum(d, 0)
    nf = np.maximum(n, 1).astype(np.float32)
    large = MAX_EXACT + (np.log(nf / MAX_EXACT) / math.log(MAX_DISTANCE / MAX_EXACT)
                         * (NUM_BUCKETS - MAX_EXACT)).astype(np.int32)
    large = np.minimum(large, NUM_BUCKETS - 1)
    bucket = np.where(n < MAX_EXACT, n, large).astype(np.int32)
    within = (d >= 0) & (d < WINDOW)
    return np.where(within, bucket, -1).astype(np.int32)


def _head_mean_matrix():
    i = np.arange(ATTN_WIDTH)
    m = (i[:, None] // HEAD_DIM == i[None, :] // HEAD_DIM).astype(np.float32) / HEAD_DIM
    return jnp.asarray(m, dtype=_BF16)


def _const_spec(shape):
    return pl.BlockSpec(shape, lambda i: (0,) * len(shape), pipeline_mode=pl.Buffered(1))


def _store_interleaved(slab_ref, val, first_slab):
    ts = val.shape[0]
    seg = ts // SUBLANES
    for l in range(val.shape[1] // LANES):
        for s in range(SUBLANES):
            slab_ref[first_slab + l, pl.ds(s, seg, stride=SUBLANES), :] = (
                val[s * seg:(s + 1) * seg, l * LANES:(l + 1) * LANES])


def _load_deinterleaved(slab_ref, out_ref):
    ts = out_ref.shape[0]
    seg = ts // SUBLANES
    for l in range(D_MODEL // LANES):
        for s in range(SUBLANES):
            out_ref[s * seg:(s + 1) * seg, l * LANES:(l + 1) * LANES] = slab_ref[l, pl.ds(s, seg, stride=SUBLANES), :]


def _lo_hi_layout(a, lo):
    ar = pltpu.roll(a, HEAD_DIM, axis=1)
    zero = jnp.zeros_like(a)
    return [jnp.where(lo, a, zero), jnp.where(lo, zero, ar), jnp.where(lo, ar, zero), jnp.where(lo, zero, a)]


def _mix_attn_kernel(tiles_per_seq, tab_ref, sink_ref, x_ref, g_ref, w_ref, cw_ref, gq_ref, gk_ref, gc_ref,
                     hm_ref, bkt_ref, ga_ref, wo_f32, wu_f32, wd_f32,
                     o_ref, wu_bf16, wd_bf16,
                     bias_ref, wo_ref, q_ref, kx_ref, vx_ref, kb0_ref, vb0_ref, gatec_ref, gateb_ref, zbuf,
                     yan_ref, p_ref, st_ref):
    ts = x_ref.shape[0]
    nblk = ts // BLK
    i = pl.program_id(0)
    seq_start = (i % tiles_per_seq == 0)

    wu_bf16[...] = wu_f32[...].astype(_BF16)
    wd_bf16[...] = wd_f32[...].astype(_BF16)

    @pl.when(i == 0)
    def _():
        bkt = bkt_ref[...]
        col = lax.broadcasted_iota(jnp.int32, (BLK, 2 * BLK), 1)

        def head_body(h, carry):
            b = jnp.full((BLK, 2 * BLK), NEG_INF, _F32)
            for t in range(NUM_BUCKETS):
                b = jnp.where(bkt == t, tab_ref[t, h] * LOG2E, b)
            bias_ref[0, h] = b
            bias_ref[1, h] = jnp.where(col >= BLK, b, NEG_INF)
            return carry

        lax.fori_loop(0, N_HEADS, head_body, 0)
        wo_ref[...] = wo_f32[...].astype(_BF16)
        kx_ref[ts - BLK:ts, :] = jnp.zeros((BLK, KX_WIDTH), _BF16)
        vx_ref[ts - BLK:ts, :] = jnp.zeros((BLK, VX_WIDTH), _BF16)

    @pl.when(seq_start)
    def _():
        zbuf[0:HALO, :] = jnp.zeros((HALO, CONV_WIDTH), _F32)

    kb0_ref[0:BLK, :] = kx_ref[ts - BLK:ts, :]
    vb0_ref[0:BLK, :] = vx_ref[ts - BLK:ts, :]

    u = _rms(x_ref[...], g_ref[...]).astype(_BF16)
    c0 = 3 * CONV_WIDTH
    c1 = c0 + ATTN_WIDTH
    q = _dot(u, w_ref[:, c0:c0 + ATTN_WIDTH])
    kv = _dot(u, w_ref[:, c1:c1 + 2 * KV_WIDTH])
    k = kv[:, 0:KV_WIDTH]
    msq = _dot((q * q).astype(_BF16), hm_ref[...])
    msk = _dot((k * k).astype(_BF16), hm_ref[0:KV_WIDTH, 0:KV_WIDTH])

    q_ref[...] = (q * lax.rsqrt(msq + EPS) * gq_ref[...] * (HEAD_DIM ** -0.5 * LOG2E)).astype(_BF16)
    kn = k * lax.rsqrt(msk + EPS) * gk_ref[...]
    lo = lax.broadcasted_iota(jnp.int32, (ts, LANES), 1) < HEAD_DIM
    kx_ref[...] = jnp.concatenate(_lo_hi_layout(kn, lo), axis=1).astype(_BF16)
    ones = [jnp.where(lo, 1.0, 0.0).astype(_F32), jnp.where(lo, 0.0, 1.0).astype(_F32)]
    v_pieces = _lo_hi_layout(kv[:, KV_WIDTH:2 * KV_WIDTH], lo)
    vx_ref[...] = jnp.concatenate([piece for s, v in enumerate(v_pieces) for piece in (v, ones[s % 2])],
                                  axis=1).astype(_BF16)
    kb0_ref[BLK:2 * BLK, :] = kx_ref[0:BLK, :]
    vb0_ref[BLK:2 * BLK, :] = vx_ref[0:BLK, :]

    lane_lo = lax.broadcasted_iota(jnp.int32, (BLK, LANES), 1) < HEAD_DIM

    def band(tile_ref, band0_ref, j, cols):
        if j == 0:
            return band0_ref[:, cols]
        return tile_ref[(j - 1) * BLK:(j + 1) * BLK, cols]

    def scores_stage(j):
        r0 = j * BLK
        slot = j % (LOOKAHEAD + 1)
        first = seq_start.astype(jnp.int32) if j == 0 else 0
        for g in range(N_KV_HEADS):
            qq = q_ref[pl.ds(r0, BLK), g * 2 * LANES:(g + 1) * 2 * LANES]
            lhs = jnp.concatenate([qq[:, 0:LANES], qq[:, LANES:2 * LANES]], axis=0)
            sink_terms = [[None, None], [None, None]]
            for par in range(2):
                kb = band(kx_ref, kb0_ref, j, slice((2 * g + par) * LANES, (2 * g + par + 1) * LANES))
                logits = _dot_nt(lhs, kb)
                for pair in range(2):
                    h = g * GQA_GROUP + 2 * pair + par
                    lg = logits[pair * BLK:(pair + 1) * BLK, :] + bias_ref[first, h]
                    sink = sink_ref[h] * LOG2E
                    m = jnp.maximum(jnp.max(lg, axis=-1, keepdims=True), sink)
                    p_ref[slot, 2 * g + par, pair * BLK:(pair + 1) * BLK, :] = jnp.exp2(lg - m).astype(_BF16)
                    sink_terms[pair][par] = jnp.exp2(sink - m)
            for pair in range(2):
                st_ref[slot, 2 * g + pair] = jnp.where(lane_lo, sink_terms[pair][0], sink_terms[pair][1])

    def values_stage(j):
        r0 = j * BLK
        slot = j % (LOOKAHEAD + 1)
        pairs = []
        for g in range(N_KV_HEADS):
            out = None
            for par in range(2):
                c = (4 * g + 2 * par) * LANES
                o = _dot(p_ref[slot, 2 * g + par],
                         band(vx_ref, vb0_ref, j, slice(c, c + 2 * LANES)))
                out = o if out is None else out + o
            for pair in range(2):
                blk = out[pair * BLK:(pair + 1) * BLK, :]
                den = blk[:, LANES:2 * LANES] + st_ref[slot, 2 * g + pair]
                pairs.append(blk[:, 0:LANES] / den)
        ya = jnp.concatenate(pairs, axis=1)
        yan_ref[r0:r0 + BLK, :] = _rms(ya, ga_ref[...]).astype(_BF16)

    def conv_piece(piece):
        half = piece % 2
        cols = slice(half * MXU_TILE, (half + 1) * MXU_TILE)
        w_cols = lambda base: w_ref[:, base + half * MXU_TILE:base + (half + 1) * MXU_TILE]
        if piece < 2:
            gatec_ref[:, cols] = _dot(u, w_cols(CONV_WIDTH))
        elif piece < 4:
            zbuf[HALO:HALO + ts, cols] = gatec_ref[:, cols] * _dot(u, w_cols(2 * CONV_WIDTH))
        else:
            gateb_ref[:, cols] = _dot(u, w_cols(0))

    piece_after_block = {0: 1, 1: 2, 3: 3, 5: 4, 6: 5}
    conv_piece(0)
    for j in range(LOOKAHEAD):
        scores_stage(j)
    for j in range(nblk):
        if j + LOOKAHEAD < nblk:
            scores_stage(j + LOOKAHEAD)
        if j in piece_after_block:
            conv_piece(piece_after_block[j])
        values_stage(j)

    conv = (cw_ref[0:1, :] * zbuf[HALO - 2:HALO - 2 + ts, :]
            + cw_ref[1:2, :] * zbuf[HALO - 1:HALO - 1 + ts, :]
            + cw_ref[2:3, :] * zbuf[HALO:HALO + ts, :])
    yc = _rms(gateb_ref[...] * conv, gc_ref[...]).astype(_BF16)
    zbuf[0:HALO, :] = zbuf[ts:ts + HALO, :]

    h = (x_ref[...] + _dot(yan_ref[...], wo_ref[CONV_WIDTH:CONV_WIDTH + ATTN_WIDTH, :])
         + _dot(yc, wo_ref[0:CONV_WIDTH, :]))
    _store_interleaved(o_ref, h, 0)


def _mix_attn(tab, sinks, x2, g_mix, w_in, conv_w, gq, gk, g_conv, hm, bkt, g_attn, w_out, w_up, w_down,
              tiles_per_seq):
    n = x2.shape[0]
    ts = ROW_TILE
    steps = n // ts
    assert ts // BLK >= 8 and CONV_WIDTH == 2 * MXU_TILE
    smem = pl.BlockSpec(memory_space=pltpu.SMEM)

    def slice_spec(w):
        rows, cols = w.shape
        assert rows % (steps * 2 * SUBLANES) == 0
        return pl.BlockSpec((rows // steps, cols), lambda i: (i, 0))

    cast_weights = (w_up, w_down)
    return pl.pallas_call(
        functools.partial(_mix_attn_kernel, tiles_per_seq),
        grid=(steps,),
        in_specs=[
            smem,
            smem,
            pl.BlockSpec((ts, D_MODEL), lambda i: (i, 0)),
            _const_spec((1, D_MODEL)),
            _const_spec((D_MODEL, IN_WIDTH)),
            _const_spec((3, CONV_WIDTH)),
            _const_spec((1, ATTN_WIDTH)),
            _const_spec((1, KV_WIDTH)),
            _const_spec((1, CONV_WIDTH)),
            _const_spec((ATTN_WIDTH, ATTN_WIDTH)),
            _const_spec((BLK, 2 * BLK)),
            _const_spec((1, ATTN_WIDTH)),
            _const_spec((CONV_WIDTH + ATTN_WIDTH, D_MODEL)),
        ] + [slice_spec(w) for w in cast_weights],
        out_specs=[pl.BlockSpec((D_MODEL // LANES, ts, LANES), lambda i: (0, i, 0))]
        + [slice_spec(w) for w in cast_weights],
        out_shape=[jax.ShapeDtypeStruct((D_MODEL // LANES, n, LANES), _F32)]
        + [jax.ShapeDtypeStruct(w.shape, _BF16) for w in cast_weights],
        scratch_shapes=[
            pltpu.VMEM((2, N_HEADS, BLK, 2 * BLK), _F32),
            pltpu.VMEM((CONV_WIDTH + ATTN_WIDTH, D_MODEL), _BF16),
            pltpu.VMEM((ts, ATTN_WIDTH), _BF16),
            pltpu.VMEM((ts, KX_WIDTH), _BF16),
            pltpu.VMEM((ts, VX_WIDTH), _BF16),
            pltpu.VMEM((2 * BLK, KX_WIDTH), _BF16),
            pltpu.VMEM((2 * BLK, VX_WIDTH), _BF16),
            pltpu.VMEM((ts, CONV_WIDTH), _F32),
            pltpu.VMEM((ts, CONV_WIDTH), _F32),
            pltpu.VMEM((ts + HALO, CONV_WIDTH), _F32),
            pltpu.VMEM((ts, ATTN_WIDTH), _BF16),
            pltpu.VMEM((LOOKAHEAD + 1, 2 * N_KV_HEADS, 2 * BLK, 2 * BLK), _BF16),
            pltpu.VMEM((LOOKAHEAD + 1, 2 * N_KV_HEADS, BLK, LANES), _F32),
        ],
        compiler_params=pltpu.CompilerParams(
            dimension_semantics=("arbitrary",), vmem_limit_bytes=VMEM_LIMIT_BYTES),
        name="mix_attn",
    )(tab, sinks, x2, g_mix, w_in, conv_w, gq, gk, g_conv, hm, bkt, g_attn, w_out, *cast_weights)


def _ffn_kernel(tiles_per_seq, h_ref, g_ref, wu_ref, cw_ref, cb_ref, wd_ref, o_ref,
                u_ref, ubuf, carry_ref, act_ref, res_ref):
    ts = o_ref.shape[0]
    i = pl.program_id(0)
    g2 = 2 * SUBLANES

    @pl.when(i % tiles_per_seq == 0)
    def _():
        carry_ref[...] = jnp.zeros(carry_ref.shape, _F32)

    h = jnp.concatenate([h_ref[l] for l in range(D_MODEL // LANES)], axis=1)
    u_ref[...] = _rms(h, g_ref[...]).astype(_BF16)
    first_sublane = lax.broadcasted_iota(jnp.int32, (SUBLANES, FF_CHUNK), 0) == 0

    def conv_half(c, part):
        slot = 2 * c + part
        buf = ubuf.at[slot % ubuf.shape[0]]
        cols = slice(part * D_FF + c * FF_CHUNK, part * D_FF + (c + 1) * FF_CHUNK)
        buf[g2:g2 + ts, :] = _dot(u_ref[...], wu_ref[:, cols])
        for k in range(2):
            cur = pltpu.roll(buf[ts + k * SUBLANES:ts + (k + 1) * SUBLANES, :], 1, axis=0)
            prev = pltpu.roll(carry_ref[slot, k * SUBLANES:(k + 1) * SUBLANES, :], 1, axis=0)
            buf[k * SUBLANES:(k + 1) * SUBLANES, :] = jnp.where(first_sublane, prev, cur)
        a = (cw_ref[0:1, cols] * buf[0:ts, :]
             + cw_ref[1:2, cols] * buf[SUBLANES:SUBLANES + ts, :]
             + cw_ref[2:3, cols] * buf[g2:g2 + ts, :] + cb_ref[:, cols])
        carry_ref[slot] = buf[ts:ts + g2, :]
        return a

    for c in range(N_FF_CHUNKS):
        gate = conv_half(c, 0)
        val = conv_half(c, 1)
        act_ref[:, c * FF_CHUNK:(c + 1) * FF_CHUNK] = (gate / (1.0 + jnp.exp(-gate)) * val).astype(_BF16)

    res = h + _dot(act_ref[...], wd_ref[...])
    for l in range(D_MODEL // LANES):
        res_ref[l] = res[:, l * LANES:(l + 1) * LANES]
    _load_deinterleaved(res_ref, o_ref)


def _ffn(h, g_ffn, w_up, conv_w, conv_b, w_down, tiles_per_seq):
    n = h.shape[1]
    ts = ROW_TILE
    g2 = 2 * SUBLANES
    return pl.pallas_call(
        functools.partial(_ffn_kernel, tiles_per_seq),
        grid=(n // ts,),
        in_specs=[
            pl.BlockSpec((D_MODEL // LANES, ts, LANES), lambda i: (0, i, 0)),
            _const_spec((1, D_MODEL)),
            _const_spec((D_MODEL, 2 * D_FF)),
            _const_spec((3, 2 * D_FF)),
            _const_spec((1, 2 * D_FF)),
            _const_spec((D_FF, D_MODEL)),
        ],
        out_specs=pl.BlockSpec((ts, D_MODEL), lambda i: (i, 0)),
        out_shape=jax.ShapeDtypeStruct((n, D_MODEL), _F32),
        scratch_shapes=[
            pltpu.VMEM((ts, D_MODEL), _BF16),
            pltpu.VMEM((4, ts + g2, FF_CHUNK), _F32),
            pltpu.VMEM((2 * N_FF_CHUNKS, g2, FF_CHUNK), _F32),
            pltpu.VMEM((ts, D_FF), _BF16),
            pltpu.VMEM((D_MODEL // LANES, ts, LANES), _F32),
        ],
        compiler_params=pltpu.CompilerParams(
            dimension_semantics=("arbitrary",), vmem_limit_bytes=VMEM_LIMIT_BYTES),
        name="ffn",
    )(h, g_ffn, w_up, conv_w, conv_b, w_down)


def kernel(x, norm_mix_g, w_in, conv_w, q_norm_g, k_norm_g, rel_bias_table, sinks, out_norm_conv_g,
           out_norm_attn_g, w_out, norm_ffn_g, w_up, ffn_conv_w, ffn_conv_b, w_down):
    bsz, seq, d = x.shape
    assert d == D_MODEL and seq % ROW_TILE == 0 and norm_mix_g.shape[0] == 1
    tiles_per_seq = seq // ROW_TILE
    n = bsz * seq
    x2 = x.reshape(n, d)
    bkt = jnp.asarray(_bucket_map())
    hm = _head_mean_matrix()

    h, w_up_b, w_down_b = _mix_attn(
        rel_bias_table, sinks[0], x2, norm_mix_g[0][None, :], w_in[0].astype(_BF16), conv_w[0],
        jnp.tile(q_norm_g[0], N_HEADS)[None, :], jnp.tile(k_norm_g[0], N_KV_HEADS)[None, :],
        out_norm_conv_g[0][None, :], hm, bkt, out_norm_attn_g[0][None, :], w_out[0],
        w_up[0], w_down[0], tiles_per_seq)
    out = _ffn(h, norm_ffn_g[0][None, :], w_up_b, ffn_conv_w[0], ffn_conv_b[0][None, :],
               w_down_b, tiles_per_seq)
    return out.reshape(bsz, seq, d)
```

```python
import functools
import math

import jax
import jax.numpy as jnp
import numpy as np
from jax import lax
from jax.experimental import pallas as pl
from jax.experimental.pallas import tpu as pltpu

D_MODEL = 1024
CONV_WIDTH = 512
HEAD_DIM = 64
N_HEADS = 8
N_KV_HEADS = 2
GQA_GROUP = N_HEADS // N_KV_HEADS
ATTN_WIDTH = N_HEADS * HEAD_DIM
KV_WIDTH = N_KV_HEADS * HEAD_DIM
WINDOW = 128
BLK = 128
NUM_BUCKETS = 32
MAX_DISTANCE = 128
MAX_EXACT = NUM_BUCKETS // 2
D_FF = 2816
EPS = 1e-6
NEG_INF = -1e30
LOG2E = math.log2(math.e)
IN_WIDTH = 3 * CONV_WIDTH + ATTN_WIDTH + 2 * KV_WIDTH

LANES = 128
SUBLANES = 8
MXU_TILE = 256
PAIR = 2 * HEAD_DIM
assert PAIR == LANES and KV_WIDTH == LANES and GQA_GROUP == 4
KX_WIDTH = 2 * N_KV_HEADS * LANES
VX_WIDTH = 2 * KX_WIDTH

ROW_TILE = 1024
FF_CHUNK = MXU_TILE
N_FF_CHUNKS = D_FF // FF_CHUNK
HALO = 8
LOOKAHEAD = 2
VMEM_LIMIT_BYTES = 60 * 1024 * 1024

_BF16 = jnp.bfloat16
_F32 = jnp.float32


def _dot(a, b):
    return jnp.dot(a, b, preferred_element_type=_F32)


def _dot_nt(a, b):
    return lax.dot_general(a, b, (((1,), (1,)), ((), ())), preferred_element_type=_F32)


def _rms(xf, g):
    return xf * lax.rsqrt(jnp.mean(xf * xf, axis=-1, keepdims=True) + EPS) * g


def _bucket_map():
    q = np.arange(BLK, dtype=np.int32)[:, None]
    j = np.arange(2 * BLK, dtype=np.int32)[None, :]
    d = q + BLK - j
    n = np.maximum(d, 0)
    nf = np.maximum(n, 1).astype(np.float32)
    large = MAX_EXACT + (np.log(nf / MAX_EXACT) / math.log(MAX_DISTANCE / MAX_EXACT)
                         * (NUM_BUCKETS - MAX_EXACT)).astype(np.int32)
    large = np.minimum(large, NUM_BUCKETS - 1)
    bucket = np.where(n < MAX_EXACT, n, large).astype(np.int32)
    within = (d >= 0) & (d < WINDOW)
    return np.where(within, bucket, -1).astype(np.int32)


def _head_mean_matrix():
    i = np.arange(ATTN_WIDTH)
    m = (i[:, None] // HEAD_DIM == i[None, :] // HEAD_DIM).astype(np.float32) / HEAD_DIM
    return jnp.asarray(m, dtype=_BF16)


def _const_spec(shape):
    return pl.BlockSpec(shape, lambda i: (0,) * len(shape), pipeline_mode=pl.Buffered(1))


def _store_interleaved(slab_ref, val, first_slab):
    ts = val.shape[0]
    seg = ts // SUBLANES
    for l in range(val.shape[1] // LANES):
        for s in range(SUBLANES):
            slab_ref[first_slab + l, pl.ds(s, seg, stride=SUBLANES), :] = (
                val[s * seg:(s + 1) * seg, l * LANES:(l + 1) * LANES])


def _load_deinterleaved(slab_ref, out_ref):
    ts = out_ref.shape[0]
    seg = ts // SUBLANES
    for l in range(D_MODEL // LANES):
        for s in range(SUBLANES):
            out_ref[s * seg:(s + 1) * seg, l * LANES:(l + 1) * LANES] = slab_ref[l, pl.ds(s, seg, stride=SUBLANES), :]


def _lo_hi_layout(a, lo):
    ar = pltpu.roll(a, HEAD_DIM, axis=1)
    zero = jnp.zeros_like(a)
    return [jnp.where(lo, a, zero), jnp.where(lo, zero, ar), jnp.where(lo, ar, zero), jnp.where(lo, zero, a)]


def _mix_attn_kernel(tiles_per_seq, tab_ref, sink_ref, x_ref, g_ref, w_f32, cw_ref, gq_head, gk_head, gc_ref,
                     hm_ref, bkt_ref, ga_ref, wo_f32, wu_f32, wd_f32,
                     o_ref, wu_bf16, wd_bf16,
                     bias_ref, w_ref, wo_ref, gq_ref, gk_ref, q_ref, kx_ref, vx_ref, kb0_ref, vb0_ref, gatec_ref, gateb_ref, zbuf,
                     yan_ref, p_ref, st_ref):
    ts = x_ref.shape[0]
    nblk = ts // BLK
    i = pl.program_id(0)
    seq_start = (i % tiles_per_seq == 0)

    wu_bf16[...] = wu_f32[...].astype(_BF16)
    wd_bf16[...] = wd_f32[...].astype(_BF16)

    @pl.when(i == 0)
    def _():
        bkt = bkt_ref[...]
        col = lax.broadcasted_iota(jnp.int32, (BLK, 2 * BLK), 1)

        def head_body(h, carry):
            b = jnp.full((BLK, 2 * BLK), NEG_INF, _F32)
            for t in range(NUM_BUCKETS):
                b = jnp.where(bkt == t, tab_ref[t, h] * LOG2E, b)
            bias_ref[0, h] = b
            bias_ref[1, h] = jnp.where(col >= BLK, b, NEG_INF)
            return carry

        lax.fori_loop(0, N_HEADS, head_body, 0)
        w_ref[...] = w_f32[...].astype(_BF16)
        wo_ref[...] = wo_f32[...].astype(_BF16)
        for h in range(N_HEADS):
            gq_ref[:, h * HEAD_DIM:(h + 1) * HEAD_DIM] = gq_head[...]
        for h in range(N_KV_HEADS):
            gk_ref[:, h * HEAD_DIM:(h + 1) * HEAD_DIM] = gk_head[...]
        kx_ref[ts - BLK:ts, :] = jnp.zeros((BLK, KX_WIDTH), _BF16)
        vx_ref[ts - BLK:ts, :] = jnp.zeros((BLK, VX_WIDTH), _BF16)

    @pl.when(seq_start)
    def _():
        zbuf[0:HALO, :] = jnp.zeros((HALO, CONV_WIDTH), _F32)

    kb0_ref[0:BLK, :] = kx_ref[ts - BLK:ts, :]
    vb0_ref[0:BLK, :] = vx_ref[ts - BLK:ts, :]

    u = _rms(x_ref[...], g_ref[...]).astype(_BF16)
    c0 = 3 * CONV_WIDTH
    c1 = c0 + ATTN_WIDTH
    q = _dot(u, w_ref[:, c0:c0 + ATTN_WIDTH])
    kv = _dot(u, w_ref[:, c1:c1 + 2 * KV_WIDTH])
    k = kv[:, 0:KV_WIDTH]
    msq = _dot((q * q).astype(_BF16), hm_ref[...])
    msk = _dot((k * k).astype(_BF16), hm_ref[0:KV_WIDTH, 0:KV_WIDTH])

    q_ref[...] = (q * lax.rsqrt(msq + EPS) * gq_ref[...] * (HEAD_DIM ** -0.5 * LOG2E)).astype(_BF16)
    kn = k * lax.rsqrt(msk + EPS) * gk_ref[...]
    lo = lax.broadcasted_iota(jnp.int32, (ts, LANES), 1) < HEAD_DIM
    kx_ref[...] = jnp.concatenate(_lo_hi_layout(kn, lo), axis=1).astype(_BF16)
    ones = [jnp.where(lo, 1.0, 0.0).astype(_F32), jnp.where(lo, 0.0, 1.0).astype(_F32)]
    v_pieces = _lo_hi_layout(kv[:, KV_WIDTH:2 * KV_WIDTH], lo)
    vx_ref[...] = jnp.concatenate([piece for s, v in enumerate(v_pieces) for piece in (v, ones[s % 2])],
                                  axis=1).astype(_BF16)
    kb0_ref[BLK:2 * BLK, :] = kx_ref[0:BLK, :]
    vb0_ref[BLK:2 * BLK, :] = vx_ref[0:BLK, :]

    lane_lo = lax.broadcasted_iota(jnp.int32, (BLK, LANES), 1) < HEAD_DIM

    def band(tile_ref, band0_ref, j, cols):
        if j == 0:
            return band0_ref[:, cols]
        return tile_ref[(j - 1) * BLK:(j + 1) * BLK, cols]

    def scores_stage(j):
        r0 = j * BLK
        slot = j % (LOOKAHEAD + 1)
        first = seq_start.astype(jnp.int32) if j == 0 else 0
        for g in range(N_KV_HEADS):
            qq = q_ref[pl.ds(r0, BLK), g * 2 * LANES:(g + 1) * 2 * LANES]
            lhs = jnp.concatenate([qq[:, 0:LANES], qq[:, LANES:2 * LANES]], axis=0)
            sink_terms = [[None, None], [None, None]]
            for par in range(2):
                kb = band(kx_ref, kb0_ref, j, slice((2 * g + par) * LANES, (2 * g + par + 1) * LANES))
                logits = _dot_nt(lhs, kb)
                for pair in range(2):
                    h = g * GQA_GROUP + 2 * pair + par
                    lg = logits[pair * BLK:(pair + 1) * BLK, :] + bias_ref[first, h]
                    sink = sink_ref[h] * LOG2E
                    m = jnp.maximum(jnp.max(lg, axis=-1, keepdims=True), sink)
                    p_ref[slot, 2 * g + par, pair * BLK:(pair + 1) * BLK, :] = jnp.exp2(lg - m).astype(_BF16)
                    sink_terms[pair][par] = jnp.exp2(sink - m)
            for pair in range(2):
                st_ref[slot, 2 * g + pair] = jnp.where(lane_lo, sink_terms[pair][0], sink_terms[pair][1])

    def values_stage(j):
        r0 = j * BLK
        slot = j % (LOOKAHEAD + 1)
        pairs = []
        for g in range(N_KV_HEADS):
            out = None
            for par in range(2):
                c = (4 * g + 2 * par) * LANES
                o = _dot(p_ref[slot, 2 * g + par],
                         band(vx_ref, vb0_ref, j, slice(c, c + 2 * LANES)))
                out = o if out is None else out + o
            for pair in range(2):
                blk = out[pair * BLK:(pair + 1) * BLK, :]
                den = blk[:, LANES:2 * LANES] + st_ref[slot, 2 * g + pair]
                pairs.append(blk[:, 0:LANES] / den)
        ya = jnp.concatenate(pairs, axis=1)
        yan_ref[r0:r0 + BLK, :] = _rms(ya, ga_ref[...]).astype(_BF16)

    def conv_piece(piece):
        half = piece % 2
        cols = slice(half * MXU_TILE, (half + 1) * MXU_TILE)
        w_cols = lambda base: w_ref[:, base + half * MXU_TILE:base + (half + 1) * MXU_TILE]
        if piece < 2:
            gatec_ref[:, cols] = _dot(u, w_cols(CONV_WIDTH))
        elif piece < 4:
            zbuf[HALO:HALO + ts, cols] = gatec_ref[:, cols] * _dot(u, w_cols(2 * CONV_WIDTH))
        else:
            gateb_ref[:, cols] = _dot(u, w_cols(0))

    piece_after_block = {0: 1, 1: 2, 3: 3, 5: 4, 6: 5}
    conv_piece(0)
    for j in range(LOOKAHEAD):
        scores_stage(j)
    for j in range(nblk):
        if j + LOOKAHEAD < nblk:
            scores_stage(j + LOOKAHEAD)
        if j in piece_after_block:
            conv_piece(piece_after_block[j])
        values_stage(j)

    conv = (cw_ref[0:1, :] * zbuf[HALO - 2:HALO - 2 + ts, :]
            + cw_ref[1:2, :] * zbuf[HALO - 1:HALO - 1 + ts, :]
            + cw_ref[2:3, :] * zbuf[HALO:HALO + ts, :])
    yc = _rms(gateb_ref[...] * conv, gc_ref[...]).astype(_BF16)
    zbuf[0:HALO, :] = zbuf[ts:ts + HALO, :]

    h = (x_ref[...] + _dot(yan_ref[...], wo_ref[CONV_WIDTH:CONV_WIDTH + ATTN_WIDTH, :])
         + _dot(yc, wo_ref[0:CONV_WIDTH, :]))
    _store_interleaved(o_ref, h, 0)


def _mix_attn(tab, sinks, x2, g_mix, w_in, conv_w, gq, gk, g_conv, hm, bkt, g_attn, w_out, w_up, w_down,
              tiles_per_seq):
    n = x2.shape[0]
    ts = ROW_TILE
    steps = n // ts
    assert ts // BLK >= 8 and CONV_WIDTH == 2 * MXU_TILE
    smem = pl.BlockSpec(memory_space=pltpu.SMEM)

    def slice_spec(w):
        rows, cols = w.shape
        assert rows % (steps * 2 * SUBLANES) == 0
        return pl.BlockSpec((rows // steps, cols), lambda i: (i, 0))

    cast_weights = (w_up, w_down)
    return pl.pallas_call(
        functools.partial(_mix_attn_kernel, tiles_per_seq),
        grid=(steps,),
        in_specs=[
            smem,
            smem,
            pl.BlockSpec((ts, D_MODEL), lambda i: (i, 0)),
            _const_spec((1, D_MODEL)),
            _const_spec((D_MODEL, IN_WIDTH)),
            _const_spec((3, CONV_WIDTH)),
            _const_spec((1, HEAD_DIM)),
            _const_spec((1, HEAD_DIM)),
            _const_spec((1, CONV_WIDTH)),
            _const_spec((ATTN_WIDTH, ATTN_WIDTH)),
            _const_spec((BLK, 2 * BLK)),
            _const_spec((1, ATTN_WIDTH)),
            _const_spec((CONV_WIDTH + ATTN_WIDTH, D_MODEL)),
        ] + [slice_spec(w) for w in cast_weights],
        out_specs=[pl.BlockSpec((D_MODEL // LANES, ts, LANES), lambda i: (0, i, 0))]
        + [slice_spec(w) for w in cast_weights],
        out_shape=[jax.ShapeDtypeStruct((D_MODEL // LANES, n, LANES), _F32)]
        + [jax.ShapeDtypeStruct(w.shape, _BF16) for w in cast_weights],
        scratch_shapes=[
            pltpu.VMEM((2, N_HEADS, BLK, 2 * BLK), _F32),
            pltpu.VMEM((D_MODEL, IN_WIDTH), _BF16),
            pltpu.VMEM((CONV_WIDTH + ATTN_WIDTH, D_MODEL), _BF16),
            pltpu.VMEM((1, ATTN_WIDTH), _F32),
            pltpu.VMEM((1, KV_WIDTH), _F32),
            pltpu.VMEM((ts, ATTN_WIDTH), _BF16),
            pltpu.VMEM((ts, KX_WIDTH), _BF16),
            pltpu.VMEM((ts, VX_WIDTH), _BF16),
            pltpu.VMEM((2 * BLK, KX_WIDTH), _BF16),
            pltpu.VMEM((2 * BLK, VX_WIDTH), _BF16),
            pltpu.VMEM((ts, CONV_WIDTH), _F32),
            pltpu.VMEM((ts, CONV_WIDTH), _F32),
            pltpu.VMEM((ts + HALO, CONV_WIDTH), _F32),
            pltpu.VMEM((ts, ATTN_WIDTH), _BF16),
            pltpu.VMEM((LOOKAHEAD + 1, 2 * N_KV_HEADS, 2 * BLK, 2 * BLK), _BF16),
            pltpu.VMEM((LOOKAHEAD + 1, 2 * N_KV_HEADS, BLK, LANES), _F32),
        ],
        compiler_params=pltpu.CompilerParams(
            dimension_semantics=("arbitrary",), vmem_limit_bytes=VMEM_LIMIT_BYTES),
        name="mix_attn",
    )(tab, sinks, x2, g_mix, w_in, conv_w, gq, gk, g_conv, hm, bkt, g_attn, w_out, *cast_weights)


def _ffn_kernel(tiles_per_seq, h_ref, g_ref, wu_ref, cw_ref, cb_ref, wd_ref, o_ref,
                u_ref, ubuf, carry_ref, act_ref, res_ref):
    ts = o_ref.shape[0]
    i = pl.program_id(0)
    g2 = 2 * SUBLANES

    @pl.when(i % tiles_per_seq == 0)
    def _():
        carry_ref[...] = jnp.zeros(carry_ref.shape, _F32)

    h = jnp.concatenate([h_ref[l] for l in range(D_MODEL // LANES)], axis=1)
    u_ref[...] = _rms(h, g_ref[...]).astype(_BF16)
    first_sublane = lax.broadcasted_iota(jnp.int32, (SUBLANES, FF_CHUNK), 0) == 0

    def conv_half(c, part):
        slot = 2 * c + part
        buf = ubuf.at[slot % ubuf.shape[0]]
        cols = slice(part * D_FF + c * FF_CHUNK, part * D_FF + (c + 1) * FF_CHUNK)
        buf[g2:g2 + ts, :] = _dot(u_ref[...], wu_ref[:, cols])
        for k in range(2):
            cur = pltpu.roll(buf[ts + k * SUBLANES:ts + (k + 1) * SUBLANES, :], 1, axis=0)
            prev = pltpu.roll(carry_ref[slot, k * SUBLANES:(k + 1) * SUBLANES, :], 1, axis=0)
            buf[k * SUBLANES:(k + 1) * SUBLANES, :] = jnp.where(first_sublane, prev, cur)
        a = (cw_ref[0:1, cols] * buf[0:ts, :]
             + cw_ref[1:2, cols] * buf[SUBLANES:SUBLANES + ts, :]
             + cw_ref[2:3, cols] * buf[g2:g2 + ts, :] + cb_ref[:, cols])
        carry_ref[slot] = buf[ts:ts + g2, :]
        return a

    for c in range(N_FF_CHUNKS):
        gate = conv_half(c, 0)
        val = conv_half(c, 1)
        act_ref[:, c * FF_CHUNK:(c + 1) * FF_CHUNK] = (gate / (1.0 + jnp.exp(-gate)) * val).astype(_BF16)

    res = h + _dot(act_ref[...], wd_ref[...])
    for l in range(D_MODEL // LANES):
        res_ref[l] = res[:, l * LANES:(l + 1) * LANES]
    _load_deinterleaved(res_ref, o_ref)


def _ffn(h, g_ffn, w_up, conv_w, conv_b, w_down, tiles_per_seq):
    n = h.shape[1]
    ts = ROW_TILE
    g2 = 2 * SUBLANES
    return pl.pallas_call(
        functools.partial(_ffn_kernel, tiles_per_seq),
        grid=(n // ts,),
        in_specs=[
            pl.BlockSpec((D_MODEL // LANES, ts, LANES), lambda i: (0, i, 0)),
            _const_spec((1, D_MODEL)),
            _const_spec((D_MODEL, 2 * D_FF)),
            _const_spec((3, 2 * D_FF)),
            _const_spec((1, 2 * D_FF)),
            _const_spec((D_FF, D_MODEL)),
        ],
        out_specs=pl.BlockSpec((ts, D_MODEL), lambda i: (i, 0)),
        out_shape=jax.ShapeDtypeStruct((n, D_MODEL), _F32),
        scratch_shapes=[
            pltpu.VMEM((ts, D_MODEL), _BF16),
            pltpu.VMEM((4, ts + g2, FF_CHUNK), _F32),
            pltpu.VMEM((2 * N_FF_CHUNKS, g2, FF_CHUNK), _F32),
            pltpu.VMEM((ts, D_FF), _BF16),
            pltpu.VMEM((D_MODEL // LANES, ts, LANES), _F32),
        ],
        compiler_params=pltpu.CompilerParams(
            dimension_semantics=("arbitrary",), vmem_limit_bytes=VMEM_LIMIT_BYTES),
        name="ffn",
    )(h, g_ffn, w_up, conv_w, conv_b, w_down)


def kernel(x, norm_mix_g, w_in, conv_w, q_norm_g, k_norm_g, rel_bias_table, sinks, out_norm_conv_g,
           out_norm_attn_g, w_out, norm_ffn_g, w_up, ffn_conv_w, ffn_conv_b, w_down):
    bsz, seq, d = x.shape
    assert d == D_MODEL and seq % ROW_TILE == 0 and norm_mix_g.shape[0] == 1
    tiles_per_seq = seq // ROW_TILE
    n = bsz * seq
    x2 = x.reshape(n, d)
    bkt = jnp.asarray(_bucket_map())
    hm = _head_mean_matrix()

    h, w_up_b, w_down_b = _mix_attn(
        rel_bias_table, sinks[0], x2, norm_mix_g[0][None, :], w_in[0], conv_w[0], q_norm_g, k_norm_g,
        out_norm_conv_g[0][None, :], hm, bkt, out_norm_attn_g[0][None, :], w_out[0],
        w_up[0], w_down[0], tiles_per_seq)
    out = _ffn(h, norm_ffn_g[0][None, :], w_up_b, ffn_conv_w[0], ffn_conv_b[0][None, :],
               w_down_b, tiles_per_seq)
    return out.reshape(bsz, seq, d)
```

```python
import functools
import math

import jax
import jax.numpy as jnp
import numpy as np
from jax import lax
from jax.experimental import pallas as pl
from jax.experimental.pallas import tpu as pltpu

D_MODEL = 1024
CONV_WIDTH = 512
HEAD_DIM = 64
N_HEADS = 8
N_KV_HEADS = 2
GQA_GROUP = N_HEADS // N_KV_HEADS
ATTN_WIDTH = N_HEADS * HEAD_DIM
KV_WIDTH = N_KV_HEADS * HEAD_DIM
WINDOW = 128
BLK = 128
NUM_BUCKETS = 32
MAX_DISTANCE = 128
MAX_EXACT = NUM_BUCKETS // 2
D_FF = 2816
EPS = 1e-6
NEG_INF = -1e30
LOG2E = math.log2(math.e)
IN_WIDTH = 3 * CONV_WIDTH + ATTN_WIDTH + 2 * KV_WIDTH

LANES = 128
SUBLANES = 8
MXU_TILE = 256
PAIR = 2 * HEAD_DIM
assert PAIR == LANES and KV_WIDTH == LANES and GQA_GROUP == 4
KX_WIDTH = 2 * N_KV_HEADS * LANES
VX_WIDTH = 2 * KX_WIDTH

ROW_TILE = 1024
FF_CHUNK = MXU_TILE
N_FF_CHUNKS = D_FF // FF_CHUNK
HALO = 8
LOOKAHEAD = 2
VMEM_LIMIT_BYTES = 60 * 1024 * 1024

_BF16 = jnp.bfloat16
_F32 = jnp.float32


def _dot(a, b):
    return jnp.dot(a, b, preferred_element_type=_F32)


def _dot_nt(a, b):
    return lax.dot_general(a, b, (((1,), (1,)), ((), ())), preferred_element_type=_F32)


def _rms(xf, g):
    return xf * lax.rsqrt(jnp.mean(xf * xf, axis=-1, keepdims=True) + EPS) * g


def _bucket_map():
    q = np.arange(BLK, dtype=np.int32)[:, None]
    j = np.arange(2 * BLK, dtype=np.int32)[None, :]
    d = q + BLK - j
    n = np.maximum(d, 0)
    nf = np.maximum(n, 1).astype(np.float32)
    large = MAX_EXACT + (np.log(nf / MAX_EXACT) / math.log(MAX_DISTANCE / MAX_EXACT)
                         * (NUM_BUCKETS - MAX_EXACT)).astype(np.int32)
    large = np.minimum(large, NUM_BUCKETS - 1)
    bucket = np.where(n < MAX_EXACT, n, large).astype(np.int32)
    within = (d >= 0) & (d < WINDOW)
    return np.where(within, bucket, -1).astype(np.int32)


def _head_mean_matrix():
    i = np.arange(ATTN_WIDTH)
    m = (i[:, None] // HEAD_DIM == i[None, :] // HEAD_DIM).astype(np.float32) / HEAD_DIM
    return jnp.asarray(m, dtype=_BF16)


def _const_spec(shape):
    return pl.BlockSpec(shape, lambda i: (0,) * len(shape), pipeline_mode=pl.Buffered(1))


def _store_interleaved(slab_ref, val, first_slab):
    ts = val.shape[0]
    seg = ts // SUBLANES
    for l in range(val.shape[1] // LANES):
        for s in range(SUBLANES):
            slab_ref[first_slab + l, pl.ds(s, seg, stride=SUBLANES), :] = (
                val[s * seg:(s + 1) * seg, l * LANES:(l + 1) * LANES])


def _load_deinterleaved(slab_ref, out_ref):
    ts = out_ref.shape[0]
    seg = ts // SUBLANES
    for l in range(D_MODEL // LANES):
        for s in range(SUBLANES):
            out_ref[s * seg:(s + 1) * seg, l * LANES:(l + 1) * LANES] = slab_ref[l, pl.ds(s, seg, stride=SUBLANES), :]


def _lo_hi_layout(a, lo):
    ar = pltpu.roll(a, HEAD_DIM, axis=1)
    zero = jnp.zeros_like(a)
    return [jnp.where(lo, a, zero), jnp.where(lo, zero, ar), jnp.where(lo, ar, zero), jnp.where(lo, zero, a)]


def _mix_attn_kernel(tiles_per_seq, tab_ref, sink_ref, x_ref, g_ref, w_f32, cw_ref, gq_head, gk_head, gc_ref,
                     hm_ref, bkt_ref, ga_ref, wo_f32, wu_f32, wd_f32,
                     o_ref, wu_bf16, wd_bf16,
                     bias_ref, w_ref, wo_ref, gq_ref, gk_ref, q_ref, kx_ref, vx_ref, kb0_ref, vb0_ref, gatec_ref, gateb_ref, zbuf,
                     yan_ref, p_ref, st_ref):
    ts = x_ref.shape[0]
    nblk = ts // BLK
    i = pl.program_id(0)
    seq_start = (i % tiles_per_seq == 0)

    wu_bf16[...] = wu_f32[...].astype(_BF16)
    wd_bf16[...] = wd_f32[...].astype(_BF16)

    @pl.when(i == 0)
    def _():
        bkt = bkt_ref[...]
        col = lax.broadcasted_iota(jnp.int32, (BLK, 2 * BLK), 1)

        def head_body(h, carry):
            b = jnp.full((BLK, 2 * BLK), NEG_INF, _F32)
            for t in range(NUM_BUCKETS):
                b = jnp.where(bkt == t, tab_ref[t, h] * LOG2E, b)
            bias_ref[0, h] = b
            bias_ref[1, h] = jnp.where(col >= BLK, b, NEG_INF)
            return carry

        lax.fori_loop(0, N_HEADS, head_body, 0)
        w_ref[...] = w_f32[...].astype(_BF16)
        wo_ref[...] = wo_f32[...].astype(_BF16)
        for h in range(N_HEADS):
            gq_ref[:, h * HEAD_DIM:(h + 1) * HEAD_DIM] = gq_head[...]
        for h in range(N_KV_HEADS):
            gk_ref[:, h * HEAD_DIM:(h + 1) * HEAD_DIM] = gk_head[...]
        kx_ref[ts - BLK:ts, :] = jnp.zeros((BLK, KX_WIDTH), _BF16)
        vx_ref[ts - BLK:ts, :] = jnp.zeros((BLK, VX_WIDTH), _BF16)

    @pl.when(seq_start)
    def _():
        zbuf[0:HALO, :] = jnp.zeros((HALO, CONV_WIDTH), _F32)

    kb0_ref[0:BLK, :] = kx_ref[ts - BLK:ts, :]
    vb0_ref[0:BLK, :] = vx_ref[ts - BLK:ts, :]

    u = _rms(x_ref[...], g_ref[...]).astype(_BF16)
    c0 = 3 * CONV_WIDTH
    c1 = c0 + ATTN_WIDTH
    q = _dot(u, w_ref[:, c0:c0 + ATTN_WIDTH])
    kv = _dot(u, w_ref[:, c1:c1 + 2 * KV_WIDTH])
    k = kv[:, 0:KV_WIDTH]
    msq = _dot((q * q).astype(_BF16), hm_ref[...])
    msk = _dot((k * k).astype(_BF16), hm_ref[0:KV_WIDTH, 0:KV_WIDTH])

    q_ref[...] = (q * lax.rsqrt(msq + EPS) * gq_ref[...] * (HEAD_DIM ** -0.5 * LOG2E)).astype(_BF16)
    kn = k * lax.rsqrt(msk + EPS) * gk_ref[...]
    lo = lax.broadcasted_iota(jnp.int32, (ts, LANES), 1) < HEAD_DIM
    kx_ref[...] = jnp.concatenate(_lo_hi_layout(kn, lo), axis=1).astype(_BF16)
    ones = [jnp.where(lo, 1.0, 0.0).astype(_F32), jnp.where(lo, 0.0, 1.0).astype(_F32)]
    v_pieces = _lo_hi_layout(kv[:, KV_WIDTH:2 * KV_WIDTH], lo)
    vx_ref[...] = jnp.concatenate([piece for s, v in enumerate(v_pieces) for piece in (v, ones[s % 2])],
                                  axis=1).astype(_BF16)
    kb0_ref[BLK:2 * BLK, :] = kx_ref[0:BLK, :]
    vb0_ref[BLK:2 * BLK, :] = vx_ref[0:BLK, :]

    lane_lo = lax.broadcasted_iota(jnp.int32, (BLK, LANES), 1) < HEAD_DIM

    def band(tile_ref, band0_ref, j, cols):
        if j == 0:
            return band0_ref[:, cols]
        return tile_ref[(j - 1) * BLK:(j + 1) * BLK, cols]

    def scores_stage(j):
        r0 = j * BLK
        slot = j % (LOOKAHEAD + 1)
        first = seq_start.astype(jnp.int32) if j == 0 else 0
        for g in range(N_KV_HEADS):
            qq = q_ref[pl.ds(r0, BLK), g * 2 * LANES:(g + 1) * 2 * LANES]
            lhs = jnp.concatenate([qq[:, 0:LANES], qq[:, LANES:2 * LANES]], axis=0)
            sink_terms = [[None, None], [None, None]]
            for par in range(2):
                kb = band(kx_ref, kb0_ref, j, slice((2 * g + par) * LANES, (2 * g + par + 1) * LANES))
                logits = _dot_nt(lhs, kb)
                for pair in range(2):
                    h = g * GQA_GROUP + 2 * pair + par
                    lg = logits[pair * BLK:(pair + 1) * BLK, :] + bias_ref[first, h]
                    sink = sink_ref[h] * LOG2E
                    m = jnp.maximum(jnp.max(lg, axis=-1, keepdims=True), sink)
                    p_ref[slot, 2 * g + par, pair * BLK:(pair + 1) * BLK, :] = jnp.exp2(lg - m).astype(_BF16)
                    sink_terms[pair][par] = jnp.exp2(sink - m)
            for pair in range(2):
                st_ref[slot, 2 * g + pair] = jnp.where(lane_lo, sink_terms[pair][0], sink_terms[pair][1])

    def values_stage(j):
        r0 = j * BLK
        slot = j % (LOOKAHEAD + 1)
        pairs = []
        for g in range(N_KV_HEADS):
            out = None
            for par in range(2):
                c = (4 * g + 2 * par) * LANES
                o = _dot(p_ref[slot, 2 * g + par],
                         band(vx_ref, vb0_ref, j, slice(c, c + 2 * LANES)))
                out = o if out is None else out + o
            for pair in range(2):
                blk = out[pair * BLK:(pair + 1) * BLK, :]
                den = blk[:, LANES:2 * LANES] + st_ref[slot, 2 * g + pair]
                pairs.append(blk[:, 0:LANES] / den)
        ya = jnp.concatenate(pairs, axis=1)
        yan_ref[r0:r0 + BLK, :] = _rms(ya, ga_ref[...]).astype(_BF16)

    def conv_piece(piece):
        half = piece % 2
        cols = slice(half * MXU_TILE, (half + 1) * MXU_TILE)
        w_cols = lambda base: w_ref[:, base + half * MXU_TILE:base + (half + 1) * MXU_TILE]
        if piece < 2:
            gatec_ref[:, cols] = _dot(u, w_cols(CONV_WIDTH))
        elif piece < 4:
            zbuf[HALO:HALO + ts, cols] = gatec_ref[:, cols] * _dot(u, w_cols(2 * CONV_WIDTH))
        else:
            gateb_ref[:, cols] = _dot(u, w_cols(0))

    piece_after_block = {0: 1, 1: 2, 3: 3, 5: 4, 6: 5}
    conv_piece(0)
    for j in range(LOOKAHEAD):
        scores_stage(j)
    for j in range(nblk):
        if j + LOOKAHEAD < nblk:
            scores_stage(j + LOOKAHEAD)
        if j in piece_after_block:
            conv_piece(piece_after_block[j])
        values_stage(j)

    conv = (cw_ref[0:1, :] * zbuf[HALO - 2:HALO - 2 + ts, :]
            + cw_ref[1:2, :] * zbuf[HALO - 1:HALO - 1 + ts, :]
            + cw_ref[2:3, :] * zbuf[HALO:HALO + ts, :])
    yc = _rms(gateb_ref[...] * conv, gc_ref[...]).astype(_BF16)
    zbuf[0:HALO, :] = zbuf[ts:ts + HALO, :]

    h = (x_ref[...] + _dot(yan_ref[...], wo_ref[CONV_WIDTH:CONV_WIDTH + ATTN_WIDTH, :])
         + _dot(yc, wo_ref[0:CONV_WIDTH, :]))
    _store_interleaved(o_ref, h, 0)


def _mix_attn(tab, sinks, x2, g_mix, w_in, conv_w, gq, gk, g_conv, hm, bkt, g_attn, w_out, w_up, w_down,
              tiles_per_seq):
    n = x2.shape[0]
    ts = ROW_TILE
    steps = n // ts
    assert ts // BLK >= 8 and CONV_WIDTH == 2 * MXU_TILE
    smem = pl.BlockSpec(memory_space=pltpu.SMEM)

    def slice_spec(w):
        rows, cols = w.shape
        assert rows % (steps * 2 * SUBLANES) == 0
        return pl.BlockSpec((rows // steps, cols), lambda i: (i, 0))

    cast_weights = (w_up, w_down)
    return pl.pallas_call(
        functools.partial(_mix_attn_kernel, tiles_per_seq),
        grid=(steps,),
        in_specs=[
            smem,
            smem,
            pl.BlockSpec((ts, D_MODEL), lambda i: (i, 0)),
            _const_spec((1, D_MODEL)),
            _const_spec((D_MODEL, IN_WIDTH)),
            _const_spec((3, CONV_WIDTH)),
            _const_spec((1, HEAD_DIM)),
            _const_spec((1, HEAD_DIM)),
            _const_spec((1, CONV_WIDTH)),
            _const_spec((ATTN_WIDTH, ATTN_WIDTH)),
            _const_spec((BLK, 2 * BLK)),
            _const_spec((1, ATTN_WIDTH)),
            _const_spec((CONV_WIDTH + ATTN_WIDTH, D_MODEL)),
        ] + [slice_spec(w) for w in cast_weights],
        out_specs=[pl.BlockSpec((D_MODEL // LANES, ts, LANES), lambda i: (0, i, 0))]
        + [slice_spec(w) for w in cast_weights],
        out_shape=[jax.ShapeDtypeStruct((D_MODEL // LANES, n, LANES), _F32)]
        + [jax.ShapeDtypeStruct(w.shape, _BF16) for w in cast_weights],
        scratch_shapes=[
            pltpu.VMEM((2, N_HEADS, BLK, 2 * BLK), _F32),
            pltpu.VMEM((D_MODEL, IN_WIDTH), _BF16),
            pltpu.VMEM((CONV_WIDTH + ATTN_WIDTH, D_MODEL), _BF16),
            pltpu.VMEM((1, ATTN_WIDTH), _F32),
            pltpu.VMEM((1, KV_WIDTH), _F32),
            pltpu.VMEM((ts, ATTN_WIDTH), _BF16),
            pltpu.VMEM((ts, KX_WIDTH), _BF16),
            pltpu.VMEM((ts, VX_WIDTH), _BF16),
            pltpu.VMEM((2 * BLK, KX_WIDTH), _BF16),
            pltpu.VMEM((2 * BLK, VX_WIDTH), _BF16),
            pltpu.VMEM((ts, CONV_WIDTH), _F32),
            pltpu.VMEM((ts, CONV_WIDTH), _F32),
            pltpu.VMEM((ts + HALO, CONV_WIDTH), _F32),
            pltpu.VMEM((ts, ATTN_WIDTH), _BF16),
            pltpu.VMEM((LOOKAHEAD + 1, 2 * N_KV_HEADS, 2 * BLK, 2 * BLK), _BF16),
            pltpu.VMEM((LOOKAHEAD + 1, 2 * N_KV_HEADS, BLK, LANES), _F32),
        ],
        compiler_params=pltpu.CompilerParams(
            dimension_semantics=("arbitrary",), vmem_limit_bytes=VMEM_LIMIT_BYTES),
        name="mix_attn",
    )(tab, sinks, x2, g_mix, w_in, conv_w, gq, gk, g_conv, hm, bkt, g_attn, w_out, *cast_weights)


def _ffn_kernel(tiles_per_seq, h_ref, g_ref, wu_hbm, cw_ref, cb_ref, wd_hbm, o_ref,
                wu_ref, wd_ref, sem_up, sem_down, u_ref, ubuf, carry_ref, act_ref, res_ref):
    ts = o_ref.shape[0]
    i = pl.program_id(0)
    g2 = 2 * SUBLANES

    def up_cols(c, part):
        return slice(part * D_FF + c * FF_CHUNK, part * D_FF + (c + 1) * FF_CHUNK)

    def up_copy(c, part):
        cols = up_cols(c, part)
        return pltpu.make_async_copy(wu_hbm.at[:, cols], wu_ref.at[:, cols], sem_up.at[2 * c + part])

    def down_copy():
        return pltpu.make_async_copy(wd_hbm, wd_ref, sem_down.at[0])

    @pl.when(i % tiles_per_seq == 0)
    def _():
        carry_ref[...] = jnp.zeros(carry_ref.shape, _F32)

    def step_body(fetch_weights):
        if fetch_weights:
            for c in range(N_FF_CHUNKS):
                up_copy(c, 0).start()
                up_copy(c, 1).start()
            down_copy().start()

        h = jnp.concatenate([h_ref[l] for l in range(D_MODEL // LANES)], axis=1)
        u_ref[...] = _rms(h, g_ref[...]).astype(_BF16)
        first_sublane = lax.broadcasted_iota(jnp.int32, (SUBLANES, FF_CHUNK), 0) == 0

        def conv_half(c, part):
            slot = 2 * c + part
            buf = ubuf.at[slot % ubuf.shape[0]]
            cols = up_cols(c, part)
            if fetch_weights:
                up_copy(c, part).wait()
            buf[g2:g2 + ts, :] = _dot(u_ref[...], wu_ref[:, cols])
            for k in range(2):
                cur = pltpu.roll(buf[ts + k * SUBLANES:ts + (k + 1) * SUBLANES, :], 1, axis=0)
                prev = pltpu.roll(carry_ref[slot, k * SUBLANES:(k + 1) * SUBLANES, :], 1, axis=0)
                buf[k * SUBLANES:(k + 1) * SUBLANES, :] = jnp.where(first_sublane, prev, cur)
            a = (cw_ref[0:1, cols] * buf[0:ts, :]
                 + cw_ref[1:2, cols] * buf[SUBLANES:SUBLANES + ts, :]
                 + cw_ref[2:3, cols] * buf[g2:g2 + ts, :] + cb_ref[:, cols])
            carry_ref[slot] = buf[ts:ts + g2, :]
            return a

        for c in range(N_FF_CHUNKS):
            gate = conv_half(c, 0)
            val = conv_half(c, 1)
            act_ref[:, c * FF_CHUNK:(c + 1) * FF_CHUNK] = (gate / (1.0 + jnp.exp(-gate)) * val).astype(_BF16)

        if fetch_weights:
            down_copy().wait()
        res = h + _dot(act_ref[...], wd_ref[...])
        for l in range(D_MODEL // LANES):
            res_ref[l] = res[:, l * LANES:(l + 1) * LANES]
        _load_deinterleaved(res_ref, o_ref)

    @pl.when(i == 0)
    def _():
        step_body(True)

    @pl.when(i > 0)
    def _():
        step_body(False)


def _ffn(h, g_ffn, w_up, conv_w, conv_b, w_down, tiles_per_seq):
    n = h.shape[1]
    ts = ROW_TILE
    g2 = 2 * SUBLANES
    hbm = pl.BlockSpec(memory_space=pl.ANY)
    return pl.pallas_call(
        functools.partial(_ffn_kernel, tiles_per_seq),
        grid=(n // ts,),
        in_specs=[
            pl.BlockSpec((D_MODEL // LANES, ts, LANES), lambda i: (0, i, 0)),
            _const_spec((1, D_MODEL)),
            hbm,
            _const_spec((3, 2 * D_FF)),
            _const_spec((1, 2 * D_FF)),
            hbm,
        ],
        out_specs=pl.BlockSpec((ts, D_MODEL), lambda i: (i, 0)),
        out_shape=jax.ShapeDtypeStruct((n, D_MODEL), _F32),
        scratch_shapes=[
            pltpu.VMEM((D_MODEL, 2 * D_FF), _BF16),
            pltpu.VMEM((D_FF, D_MODEL), _BF16),
            pltpu.SemaphoreType.DMA((2 * N_FF_CHUNKS,)),
            pltpu.SemaphoreType.DMA((1,)),
            pltpu.VMEM((ts, D_MODEL), _BF16),
            pltpu.VMEM((4, ts + g2, FF_CHUNK), _F32),
            pltpu.VMEM((2 * N_FF_CHUNKS, g2, FF_CHUNK), _F32),
            pltpu.VMEM((ts, D_FF), _BF16),
            pltpu.VMEM((D_MODEL // LANES, ts, LANES), _F32),
        ],
        compiler_params=pltpu.CompilerParams(
            dimension_semantics=("arbitrary",), vmem_limit_bytes=VMEM_LIMIT_BYTES),
        name="ffn",
    )(h, g_ffn, w_up, conv_w, conv_b, w_down)


def kernel(x, norm_mix_g, w_in, conv_w, q_norm_g, k_norm_g, rel_bias_table, sinks, out_norm_conv_g,
           out_norm_attn_g, w_out, norm_ffn_g, w_up, ffn_conv_w, ffn_conv_b, w_down):
    bsz, seq, d = x.shape
    assert d == D_MODEL and seq % ROW_TILE == 0 and norm_mix_g.shape[0] == 1
    tiles_per_seq = seq // ROW_TILE
    n = bsz * seq
    x2 = x.reshape(n, d)
    bkt = jnp.asarray(_bucket_map())
    hm = _head_mean_matrix()

    h, w_up_b, w_down_b = _mix_attn(
        rel_bias_table, sinks[0], x2, norm_mix_g[0][None, :], w_in[0], conv_w[0], q_norm_g, k_norm_g,
        out_norm_conv_g[0][None, :], hm, bkt, out_norm_attn_g[0][None, :], w_out[0],
        w_up[0], w_down[0], tiles_per_seq)
    out = _ffn(h, norm_ffn_g[0][None, :], w_up_b, ffn_conv_w[0], ffn_conv_b[0][None, :],
               w_down_b, tiles_per_seq)
    return out.reshape(bsz, seq, d)
```

```python
import functools
import math

import jax
import jax.numpy as jnp
import numpy as np
from jax import lax
from jax.experimental import pallas as pl
from jax.experimental.pallas import tpu as pltpu

D_MODEL = 1024
CONV_WIDTH = 512
HEAD_DIM = 64
N_HEADS = 8
N_KV_HEADS = 2
GQA_GROUP = N_HEADS // N_KV_HEADS
ATTN_WIDTH = N_HEADS * HEAD_DIM
KV_WIDTH = N_KV_HEADS * HEAD_DIM
WINDOW = 128
BLK = 128
NUM_BUCKETS = 32
MAX_DISTANCE = 128
MAX_EXACT = NUM_BUCKETS // 2
D_FF = 2816
EPS = 1e-6
NEG_INF = -1e30
LOG2E = math.log2(math.e)
IN_WIDTH = 3 * CONV_WIDTH + ATTN_WIDTH + 2 * KV_WIDTH

LANES = 128
SUBLANES = 8
MXU_TILE = 256
PAIR = 2 * HEAD_DIM
assert PAIR == LANES and KV_WIDTH == LANES and GQA_GROUP == 4
KX_WIDTH = 2 * N_KV_HEADS * LANES
VX_WIDTH = 2 * KX_WIDTH

ROW_TILE = 1024
FF_CHUNK = MXU_TILE
N_FF_CHUNKS = D_FF // FF_CHUNK
HALO = 8
LOOKAHEAD = 2
VMEM_LIMIT_BYTES = 60 * 1024 * 1024

_BF16 = jnp.bfloat16
_F32 = jnp.float32


def _dot(a, b):
    return jnp.dot(a, b, preferred_element_type=_F32)


def _dot_nt(a, b):
    return lax.dot_general(a, b, (((1,), (1,)), ((), ())), preferred_element_type=_F32)


def _rms(xf, g):
    return xf * lax.rsqrt(jnp.mean(xf * xf, axis=-1, keepdims=True) + EPS) * g


def _bucket_map():
    q = np.arange(BLK, dtype=np.int32)[:, None]
    j = np.arange(2 * BLK, dtype=np.int32)[None, :]
    d = q + BLK - j
    n = np.maximum(d, 0)
    nf = np.maximum(n, 1).astype(np.float32)
    large = MAX_EXACT + (np.log(nf / MAX_EXACT) / math.log(MAX_DISTANCE / MAX_EXACT)
                         * (NUM_BUCKETS - MAX_EXACT)).astype(np.int32)
    large = np.minimum(large, NUM_BUCKETS - 1)
    bucket = np.where(n < MAX_EXACT, n, large).astype(np.int32)
    within = (d >= 0) & (d < WINDOW)
    return np.where(within, bucket, -1).astype(np.int32)


def _head_mean_matrix():
    i = np.arange(ATTN_WIDTH)
    m = (i[:, None] // HEAD_DIM == i[None, :] // HEAD_DIM).astype(np.float32) / HEAD_DIM
    return jnp.asarray(m, dtype=_BF16)


def _const_spec(shape):
    return pl.BlockSpec(shape, lambda i: (0,) * len(shape), pipeline_mode=pl.Buffered(1))


def _store_interleaved(slab_ref, val):
    ts = val.shape[0]
    seg = ts // SUBLANES
    for l in range(D_MODEL // LANES):
        for s in range(SUBLANES):
            slab_ref[l, pl.ds(s, seg, stride=SUBLANES), :] = val[s * seg:(s + 1) * seg, l * LANES:(l + 1) * LANES]


def _load_deinterleaved(slab_ref, out_ref):
    ts = out_ref.shape[0]
    seg = ts // SUBLANES
    for l in range(D_MODEL // LANES):
        for s in range(SUBLANES):
            out_ref[s * seg:(s + 1) * seg, l * LANES:(l + 1) * LANES] = slab_ref[l, pl.ds(s, seg, stride=SUBLANES), :]


def _lo_hi_layout(a, lo):
    ar = pltpu.roll(a, HEAD_DIM, axis=1)
    zero = jnp.zeros_like(a)
    return [jnp.where(lo, a, zero), jnp.where(lo, zero, ar), jnp.where(lo, ar, zero), jnp.where(lo, zero, a)]


def _mix_attn_kernel(tiles_per_seq, tab_ref, sink_ref, x_ref, g_ref, w_f32, cw_ref, gq_head, gk_head, gc_ref,
                     hm_ref, bkt_ref, ga_ref, wo_f32, wu_f32, wd_f32,
                     o_ref, wu_bf16, wd_bf16,
                     bias_ref, w_ref, wo_ref, gq_ref, gk_ref, q_ref, kx_ref, vx_ref, kb0_ref, vb0_ref, gatec_ref, gateb_ref, zbuf,
                     yan_ref, p_ref, st_ref):
    ts = x_ref.shape[0]
    nblk = ts // BLK
    i = pl.program_id(0)
    seq_start = (i % tiles_per_seq == 0)

    wu_bf16[...] = wu_f32[...].astype(_BF16)
    wd_bf16[...] = wd_f32[...].astype(_BF16)

    @pl.when(i == 0)
    def _():
        bkt = bkt_ref[...]
        col = lax.broadcasted_iota(jnp.int32, (BLK, 2 * BLK), 1)

        def head_body(h, carry):
            b = jnp.full((BLK, 2 * BLK), NEG_INF, _F32)
            for t in range(NUM_BUCKETS):
                b = jnp.where(bkt == t, tab_ref[t, h] * LOG2E, b)
            bias_ref[0, h] = b
            bias_ref[1, h] = jnp.where(col >= BLK, b, NEG_INF)
            return carry

        lax.fori_loop(0, N_HEADS, head_body, 0)
        w_ref[...] = w_f32[...].astype(_BF16)
        wo_ref[...] = wo_f32[...].astype(_BF16)
        for h in range(N_HEADS):
            gq_ref[:, h * HEAD_DIM:(h + 1) * HEAD_DIM] = gq_head[...]
        for h in range(N_KV_HEADS):
            gk_ref[:, h * HEAD_DIM:(h + 1) * HEAD_DIM] = gk_head[...]
        kx_ref[ts - BLK:ts, :] = jnp.zeros((BLK, KX_WIDTH), _BF16)
        vx_ref[ts - BLK:ts, :] = jnp.zeros((BLK, VX_WIDTH), _BF16)

    @pl.when(seq_start)
    def _():
        zbuf[0:HALO, :] = jnp.zeros((HALO, CONV_WIDTH), _F32)

    kb0_ref[0:BLK, :] = kx_ref[ts - BLK:ts, :]
    vb0_ref[0:BLK, :] = vx_ref[ts - BLK:ts, :]

    u = _rms(x_ref[...], g_ref[...]).astype(_BF16)
    c0 = 3 * CONV_WIDTH
    c1 = c0 + ATTN_WIDTH
    q = _dot(u, w_ref[:, c0:c0 + ATTN_WIDTH])
    kv = _dot(u, w_ref[:, c1:c1 + 2 * KV_WIDTH])
    k = kv[:, 0:KV_WIDTH]
    msq = _dot((q * q).astype(_BF16), hm_ref[...])
    msk = _dot((k * k).astype(_BF16), hm_ref[0:KV_WIDTH, 0:KV_WIDTH])

    q_ref[...] = (q * lax.rsqrt(msq + EPS) * gq_ref[...] * (HEAD_DIM ** -0.5 * LOG2E)).astype(_BF16)
    kn = k * lax.rsqrt(msk + EPS) * gk_ref[...]
    lo = lax.broadcasted_iota(jnp.int32, (ts, LANES), 1) < HEAD_DIM
    kx_ref[...] = jnp.concatenate(_lo_hi_layout(kn, lo), axis=1).astype(_BF16)
    ones = [jnp.where(lo, 1.0, 0.0).astype(_F32), jnp.where(lo, 0.0, 1.0).astype(_F32)]
    v_pieces = _lo_hi_layout(kv[:, KV_WIDTH:2 * KV_WIDTH], lo)
    vx_ref[...] = jnp.concatenate([piece for s, v in enumerate(v_pieces) for piece in (v, ones[s % 2])],
                                  axis=1).astype(_BF16)
    kb0_ref[BLK:2 * BLK, :] = kx_ref[0:BLK, :]
    vb0_ref[BLK:2 * BLK, :] = vx_ref[0:BLK, :]

    lane_lo = lax.broadcasted_iota(jnp.int32, (BLK, LANES), 1) < HEAD_DIM

    def band(tile_ref, band0_ref, j, cols):
        if j == 0:
            return band0_ref[:, cols]
        return tile_ref[(j - 1) * BLK:(j + 1) * BLK, cols]

    def scores_stage(j):
        r0 = j * BLK
        slot = j % (LOOKAHEAD + 1)
        first = seq_start.astype(jnp.int32) if j == 0 else 0
        for g in range(N_KV_HEADS):
            qq = q_ref[pl.ds(r0, BLK), g * 2 * LANES:(g + 1) * 2 * LANES]
            lhs = jnp.concatenate([qq[:, 0:LANES], qq[:, LANES:2 * LANES]], axis=0)
            sink_terms = [[None, None], [None, None]]
            for par in range(2):
                kb = band(kx_ref, kb0_ref, j, slice((2 * g + par) * LANES, (2 * g + par + 1) * LANES))
                logits = _dot_nt(lhs, kb)
                for pair in range(2):
                    h = g * GQA_GROUP + 2 * pair + par
                    lg = logits[pair * BLK:(pair + 1) * BLK, :] + bias_ref[first, h]
                    sink = sink_ref[h] * LOG2E
                    m = jnp.maximum(jnp.max(lg, axis=-1, keepdims=True), sink)
                    p_ref[slot, 2 * g + par, pair * BLK:(pair + 1) * BLK, :] = jnp.exp2(lg - m).astype(_BF16)
                    sink_terms[pair][par] = jnp.exp2(sink - m)
            for pair in range(2):
                st_ref[slot, 2 * g + pair] = jnp.where(lane_lo, sink_terms[pair][0], sink_terms[pair][1])

    def values_stage(j):
        r0 = j * BLK
        slot = j % (LOOKAHEAD + 1)
        pairs = []
        for g in range(N_KV_HEADS):
            out = None
            for par in range(2):
                c = (4 * g + 2 * par) * LANES
                o = _dot(p_ref[slot, 2 * g + par],
                         band(vx_ref, vb0_ref, j, slice(c, c + 2 * LANES)))
                out = o if out is None else out + o
            for pair in range(2):
                blk = out[pair * BLK:(pair + 1) * BLK, :]
                den = blk[:, LANES:2 * LANES] + st_ref[slot, 2 * g + pair]
                pairs.append(blk[:, 0:LANES] / den)
        ya = jnp.concatenate(pairs, axis=1)
        yan_ref[r0:r0 + BLK, :] = _rms(ya, ga_ref[...]).astype(_BF16)

    def conv_piece(piece):
        half = piece % 2
        cols = slice(half * MXU_TILE, (half + 1) * MXU_TILE)
        w_cols = lambda base: w_ref[:, base + half * MXU_TILE:base + (half + 1) * MXU_TILE]
        if piece < 2:
            gatec_ref[:, cols] = _dot(u, w_cols(CONV_WIDTH))
        elif piece < 4:
            zbuf[HALO:HALO + ts, cols] = gatec_ref[:, cols] * _dot(u, w_cols(2 * CONV_WIDTH))
        else:
            gateb_ref[:, cols] = _dot(u, w_cols(0))

    piece_after_block = {0: 1, 1: 2, 3: 3, 5: 4, 6: 5}
    conv_piece(0)
    for j in range(LOOKAHEAD):
        scores_stage(j)
    for j in range(nblk):
        if j + LOOKAHEAD < nblk:
            scores_stage(j + LOOKAHEAD)
        if j in piece_after_block:
            conv_piece(piece_after_block[j])
        values_stage(j)

    conv = (cw_ref[0:1, :] * zbuf[HALO - 2:HALO - 2 + ts, :]
            + cw_ref[1:2, :] * zbuf[HALO - 1:HALO - 1 + ts, :]
            + cw_ref[2:3, :] * zbuf[HALO:HALO + ts, :])
    yc = _rms(gateb_ref[...] * conv, gc_ref[...]).astype(_BF16)
    zbuf[0:HALO, :] = zbuf[ts:ts + HALO, :]

    h = (x_ref[...] + _dot(yan_ref[...], wo_ref[CONV_WIDTH:CONV_WIDTH + ATTN_WIDTH, :])
         + _dot(yc, wo_ref[0:CONV_WIDTH, :]))
    _store_interleaved(o_ref, h)


def _mix_attn(tab, sinks, x2, g_mix, w_in, conv_w, gq, gk, g_conv, hm, bkt, g_attn, w_out, w_up, w_down,
              tiles_per_seq):
    n = x2.shape[0]
    ts = ROW_TILE
    steps = n // ts
    assert ts // BLK >= 8 and CONV_WIDTH == 2 * MXU_TILE
    smem = pl.BlockSpec(memory_space=pltpu.SMEM)

    def slice_spec(w):
        rows, cols = w.shape
        assert rows % (steps * 2 * SUBLANES) == 0
        return pl.BlockSpec((rows // steps, cols), lambda i: (i, 0))

    cast_weights = (w_up, w_down)
    return pl.pallas_call(
        functools.partial(_mix_attn_kernel, tiles_per_seq),
        grid=(steps,),
        in_specs=[
            smem,
            smem,
            pl.BlockSpec((ts, D_MODEL), lambda i: (i, 0)),
            _const_spec((1, D_MODEL)),
            _const_spec((D_MODEL, IN_WIDTH)),
            _const_spec((3, CONV_WIDTH)),
            _const_spec((1, HEAD_DIM)),
            _const_spec((1, HEAD_DIM)),
            _const_spec((1, CONV_WIDTH)),
            _const_spec((ATTN_WIDTH, ATTN_WIDTH)),
            _const_spec((BLK, 2 * BLK)),
            _const_spec((1, ATTN_WIDTH)),
            _const_spec((CONV_WIDTH + ATTN_WIDTH, D_MODEL)),
        ] + [slice_spec(w) for w in cast_weights],
        out_specs=[pl.BlockSpec((D_MODEL // LANES, ts, LANES), lambda i: (0, i, 0))]
        + [slice_spec(w) for w in cast_weights],
        out_shape=[jax.ShapeDtypeStruct((D_MODEL // LANES, n, LANES), _F32)]
        + [jax.ShapeDtypeStruct(w.shape, _BF16) for w in cast_weights],
        scratch_shapes=[
            pltpu.VMEM((2, N_HEADS, BLK, 2 * BLK), _F32),
            pltpu.VMEM((D_MODEL, IN_WIDTH), _BF16),
            pltpu.VMEM((CONV_WIDTH + ATTN_WIDTH, D_MODEL), _BF16),
            pltpu.VMEM((1, ATTN_WIDTH), _F32),
            pltpu.VMEM((1, KV_WIDTH), _F32),
            pltpu.VMEM((ts, ATTN_WIDTH), _BF16),
            pltpu.VMEM((ts, KX_WIDTH), _BF16),
            pltpu.VMEM((ts, VX_WIDTH), _BF16),
            pltpu.VMEM((2 * BLK, KX_WIDTH), _BF16),
            pltpu.VMEM((2 * BLK, VX_WIDTH), _BF16),
            pltpu.VMEM((ts, CONV_WIDTH), _F32),
            pltpu.VMEM((ts, CONV_WIDTH), _F32),
            pltpu.VMEM((ts + HALO, CONV_WIDTH), _F32),
            pltpu.VMEM((ts, ATTN_WIDTH), _BF16),
            pltpu.VMEM((LOOKAHEAD + 1, 2 * N_KV_HEADS, 2 * BLK, 2 * BLK), _BF16),
            pltpu.VMEM((LOOKAHEAD + 1, 2 * N_KV_HEADS, BLK, LANES), _F32),
        ],
        compiler_params=pltpu.CompilerParams(
            dimension_semantics=("arbitrary",), vmem_limit_bytes=VMEM_LIMIT_BYTES),
        name="mix_attn",
    )(tab, sinks, x2, g_mix, w_in, conv_w, gq, gk, g_conv, hm, bkt, g_attn, w_out, *cast_weights)


def _ffn_kernel(tiles_per_seq, h_ref, g_ref, wu_ref, cw_ref, cb_ref, wd_ref, o_ref,
                u_ref, ubuf, carry_ref, act_ref, res_ref):
    ts = o_ref.shape[0]
    i = pl.program_id(0)
    g2 = 2 * SUBLANES

    @pl.when(i % tiles_per_seq == 0)
    def _():
        carry_ref[...] = jnp.zeros(carry_ref.shape, _F32)

    h = jnp.concatenate([h_ref[l] for l in range(D_MODEL // LANES)], axis=1)
    u_ref[...] = _rms(h, g_ref[...]).astype(_BF16)
    first_sublane = lax.broadcasted_iota(jnp.int32, (SUBLANES, FF_CHUNK), 0) == 0

    def conv_half(c, part):
        slot = 2 * c + part
        buf = ubuf.at[slot % ubuf.shape[0]]
        cols = slice(part * D_FF + c * FF_CHUNK, part * D_FF + (c + 1) * FF_CHUNK)
        buf[g2:g2 + ts, :] = _dot(u_ref[...], wu_ref[:, cols])
        for k in range(2):
            cur = pltpu.roll(buf[ts + k * SUBLANES:ts + (k + 1) * SUBLANES, :], 1, axis=0)
            prev = pltpu.roll(carry_ref[slot, k * SUBLANES:(k + 1) * SUBLANES, :], 1, axis=0)
            buf[k * SUBLANES:(k + 1) * SUBLANES, :] = jnp.where(first_sublane, prev, cur)
        a = (cw_ref[0:1, cols] * buf[0:ts, :]
             + cw_ref[1:2, cols] * buf[SUBLANES:SUBLANES + ts, :]
             + cw_ref[2:3, cols] * buf[g2:g2 + ts, :] + cb_ref[:, cols])
        carry_ref[slot] = buf[ts:ts + g2, :]
        return a

    for c in range(N_FF_CHUNKS):
        gate = conv_half(c, 0)
        val = conv_half(c, 1)
        act_ref[:, c * FF_CHUNK:(c + 1) * FF_CHUNK] = (gate / (1.0 + jnp.exp(-gate)) * val).astype(_BF16)

    res = h + _dot(act_ref[...], wd_ref[...])
    for l in range(D_MODEL // LANES):
        res_ref[l] = res[:, l * LANES:(l + 1) * LANES]
    _load_deinterleaved(res_ref, o_ref)


def _ffn(h, g_ffn, w_up, conv_w, conv_b, w_down, tiles_per_seq):
    n = h.shape[1]
    ts = ROW_TILE
    g2 = 2 * SUBLANES
    return pl.pallas_call(
        functools.partial(_ffn_kernel, tiles_per_seq),
        grid=(n // ts,),
        in_specs=[
            pl.BlockSpec((D_MODEL // LANES, ts, LANES), lambda i: (0, i, 0)),
            _const_spec((1, D_MODEL)),
            _const_spec((D_MODEL, 2 * D_FF)),
            _const_spec((3, 2 * D_FF)),
            _const_spec((1, 2 * D_FF)),
            _const_spec((D_FF, D_MODEL)),
        ],
        out_specs=pl.BlockSpec((ts, D_MODEL), lambda i: (i, 0)),
        out_shape=jax.ShapeDtypeStruct((n, D_MODEL), _F32),
        scratch_shapes=[
            pltpu.VMEM((ts, D_MODEL), _BF16),
            pltpu.VMEM((4, ts + g2, FF_CHUNK), _F32),
            pltpu.VMEM((2 * N_FF_CHUNKS, g2, FF_CHUNK), _F32),
            pltpu.VMEM((ts, D_FF), _BF16),
            pltpu.VMEM((D_MODEL // LANES, ts, LANES), _F32),
        ],
        compiler_params=pltpu.CompilerParams(
            dimension_semantics=("arbitrary",), vmem_limit_bytes=VMEM_LIMIT_BYTES),
        name="ffn",
    )(h, g_ffn, w_up, conv_w, conv_b, w_down)


def kernel(x, norm_mix_g, w_in, conv_w, q_norm_g, k_norm_g, rel_bias_table, sinks, out_norm_conv_g,
           out_norm_attn_g, w_out, norm_ffn_g, w_up, ffn_conv_w, ffn_conv_b, w_down):
    bsz, seq, d = x.shape
    assert d == D_MODEL and seq % ROW_TILE == 0 and norm_mix_g.shape[0] == 1
    tiles_per_seq = seq // ROW_TILE
    n = bsz * seq
    x2 = x.reshape(n, d)
    bkt = jnp.asarray(_bucket_map())
    hm = _head_mean_matrix()

    h, w_up_b, w_down_b = _mix_attn(
        rel_bias_table, sinks[0], x2, norm_mix_g[0][None, :], w_in[0], conv_w[0], q_norm_g, k_norm_g,
        out_norm_conv_g[0][None, :], hm, bkt, out_norm_attn_g[0][None, :], w_out[0],
        w_up[0], w_down[0], tiles_per_seq)
    out = _ffn(h, norm_ffn_g[0][None, :], w_up_b, ffn_conv_w[0], ffn_conv_b[0][None, :],
               w_down_b, tiles_per_seq)
    return out.reshape(bsz, seq, d)
```

```python
import functools
import math

import jax
import jax.numpy as jnp
import numpy as np
from jax import lax
from jax.experimental import pallas as pl
from jax.experimental.pallas import tpu as pltpu

D_MODEL = 1024
CONV_WIDTH = 512
HEAD_DIM = 64
N_HEADS = 8
N_KV_HEADS = 2
GQA_GROUP = N_HEADS // N_KV_HEADS
ATTN_WIDTH = N_HEADS * HEAD_DIM
KV_WIDTH = N_KV_HEADS * HEAD_DIM
WINDOW = 128
BLK = 128
NUM_BUCKETS = 32
MAX_DISTANCE = 128
MAX_EXACT = NUM_BUCKETS // 2
D_FF = 2816
EPS = 1e-6
NEG_INF = -1e30
LOG2E = math.log2(math.e)
IN_WIDTH = 3 * CONV_WIDTH + ATTN_WIDTH + 2 * KV_WIDTH

LANES = 128
SUBLANES = 8
MXU_TILE = 256
PAIR = 2 * HEAD_DIM
assert PAIR == LANES and KV_WIDTH == LANES and GQA_GROUP == 4
KX_WIDTH = 2 * N_KV_HEADS * LANES
VX_WIDTH = 2 * KX_WIDTH

ROW_TILE = 1024
FF_CHUNK = MXU_TILE
N_FF_CHUNKS = D_FF // FF_CHUNK
HALO = 8
LOOKAHEAD = 2
VMEM_LIMIT_BYTES = 60 * 1024 * 1024

_BF16 = jnp.bfloat16
_F32 = jnp.float32


def _dot(a, b):
    return jnp.dot(a, b, preferred_element_type=_F32)


def _dot_nt(a, b):
    return lax.dot_general(a, b, (((1,), (1,)), ((), ())), preferred_element_type=_F32)


def _rms(xf, g):
    return xf * lax.rsqrt(jnp.mean(xf * xf, axis=-1, keepdims=True) + EPS) * g


def _bucket_map():
    q = np.arange(BLK, dtype=np.int32)[:, None]
    j = np.arange(2 * BLK, dtype=np.int32)[None, :]
    d = q + BLK - j
    n = np.maximum(d, 0)
    nf = np.maximum(n, 1).astype(np.float32)
    large = MAX_EXACT + (np.log(nf / MAX_EXACT) / math.log(MAX_DISTANCE / MAX_EXACT)
                         * (NUM_BUCKETS - MAX_EXACT)).astype(np.int32)
    large = np.minimum(large, NUM_BUCKETS - 1)
    bucket = np.where(n < MAX_EXACT, n, large).astype(np.int32)
    within = (d >= 0) & (d < WINDOW)
    return np.where(within, bucket, -1).astype(np.int32)


def _const_spec(shape):
    return pl.BlockSpec(shape, lambda i: (0,) * len(shape), pipeline_mode=pl.Buffered(1))


def _store_interleaved(slab_ref, val):
    ts = val.shape[0]
    seg = ts // SUBLANES
    for l in range(D_MODEL // LANES):
        for s in range(SUBLANES):
            slab_ref[l, pl.ds(s, seg, stride=SUBLANES), :] = val[s * seg:(s + 1) * seg, l * LANES:(l + 1) * LANES]


def _load_deinterleaved(slab_ref, out_ref):
    ts = out_ref.shape[0]
    seg = ts // SUBLANES
    for l in range(D_MODEL // LANES):
        for s in range(SUBLANES):
            out_ref[s * seg:(s + 1) * seg, l * LANES:(l + 1) * LANES] = slab_ref[l, pl.ds(s, seg, stride=SUBLANES), :]


def _lo_hi_layout(a, lo):
    ar = pltpu.roll(a, HEAD_DIM, axis=1)
    zero = jnp.zeros_like(a)
    return [jnp.where(lo, a, zero), jnp.where(lo, zero, ar), jnp.where(lo, ar, zero), jnp.where(lo, zero, a)]


def _head_inv_rms(a, lo):
    cols = []
    for c in range(a.shape[1] // LANES):
        sq = a[:, c * LANES:(c + 1) * LANES]
        sq = sq * sq
        ms_lo = jnp.sum(jnp.where(lo, sq, 0.0), axis=-1, keepdims=True) * (1.0 / HEAD_DIM)
        ms_hi = jnp.sum(jnp.where(lo, 0.0, sq), axis=-1, keepdims=True) * (1.0 / HEAD_DIM)
        cols.append(jnp.where(lo, lax.rsqrt(ms_lo + EPS), lax.rsqrt(ms_hi + EPS)))
    return jnp.concatenate(cols, axis=1)


def _mix_attn_kernel(tiles_per_seq, tab_ref, sink_ref, x_ref, g_ref, w_f32, cw_ref, gq_head, gk_head, gc_ref,
                     bkt_ref, ga_ref, wo_f32, wu_f32, wd_f32,
                     o_ref, wu_bf16, wd_bf16,
                     bias_ref, w_ref, wo_ref, gq_ref, gk_ref, q_ref, kx_ref, vx_ref, kb0_ref, vb0_ref, gatec_ref, gateb_ref, zbuf,
                     yan_ref, p_ref, st_ref):
    ts = x_ref.shape[0]
    nblk = ts // BLK
    i = pl.program_id(0)
    seq_start = (i % tiles_per_seq == 0)

    wu_bf16[...] = wu_f32[...].astype(_BF16)
    wd_bf16[...] = wd_f32[...].astype(_BF16)

    @pl.when(i == 0)
    def _():
        bkt = bkt_ref[...]
        col = lax.broadcasted_iota(jnp.int32, (BLK, 2 * BLK), 1)

        def head_body(h, carry):
            b = jnp.full((BLK, 2 * BLK), NEG_INF, _F32)
            for t in range(NUM_BUCKETS):
                b = jnp.where(bkt == t, tab_ref[t, h] * LOG2E, b)
            bias_ref[0, h] = b
            bias_ref[1, h] = jnp.where(col >= BLK, b, NEG_INF)
            return carry

        lax.fori_loop(0, N_HEADS, head_body, 0)
        w_ref[...] = w_f32[...].astype(_BF16)
        wo_ref[...] = wo_f32[...].astype(_BF16)
        for h in range(N_HEADS):
            gq_ref[:, h * HEAD_DIM:(h + 1) * HEAD_DIM] = gq_head[...]
        for h in range(N_KV_HEADS):
            gk_ref[:, h * HEAD_DIM:(h + 1) * HEAD_DIM] = gk_head[...]
        kx_ref[ts - BLK:ts, :] = jnp.zeros((BLK, KX_WIDTH), _BF16)
        vx_ref[ts - BLK:ts, :] = jnp.zeros((BLK, VX_WIDTH), _BF16)

    @pl.when(seq_start)
    def _():
        zbuf[0:HALO, :] = jnp.zeros((HALO, CONV_WIDTH), _F32)

    kb0_ref[0:BLK, :] = kx_ref[ts - BLK:ts, :]
    vb0_ref[0:BLK, :] = vx_ref[ts - BLK:ts, :]

    u = _rms(x_ref[...], g_ref[...]).astype(_BF16)
    c0 = 3 * CONV_WIDTH
    c1 = c0 + ATTN_WIDTH
    q = _dot(u, w_ref[:, c0:c0 + ATTN_WIDTH])
    kv = _dot(u, w_ref[:, c1:c1 + 2 * KV_WIDTH])
    k = kv[:, 0:KV_WIDTH]
    lo = lax.broadcasted_iota(jnp.int32, (ts, LANES), 1) < HEAD_DIM
    q_ref[...] = (q * _head_inv_rms(q, lo) * gq_ref[...] * (HEAD_DIM ** -0.5 * LOG2E)).astype(_BF16)
    kn = k * _head_inv_rms(k, lo) * gk_ref[...]
    kx_ref[...] = jnp.concatenate(_lo_hi_layout(kn, lo), axis=1).astype(_BF16)
    ones = [jnp.where(lo, 1.0, 0.0).astype(_F32), jnp.where(lo, 0.0, 1.0).astype(_F32)]
    v_pieces = _lo_hi_layout(kv[:, KV_WIDTH:2 * KV_WIDTH], lo)
    vx_ref[...] = jnp.concatenate([piece for s, v in enumerate(v_pieces) for piece in (v, ones[s % 2])],
                                  axis=1).astype(_BF16)
    kb0_ref[BLK:2 * BLK, :] = kx_ref[0:BLK, :]
    vb0_ref[BLK:2 * BLK, :] = vx_ref[0:BLK, :]

    lane_lo = lax.broadcasted_iota(jnp.int32, (BLK, LANES), 1) < HEAD_DIM

    def band(tile_ref, band0_ref, j, cols):
        if j == 0:
            return band0_ref[:, cols]
        return tile_ref[(j - 1) * BLK:(j + 1) * BLK, cols]

    def scores_stage(j):
        r0 = j * BLK
        slot = j % (LOOKAHEAD + 1)
        first = seq_start.astype(jnp.int32) if j == 0 else 0
        for g in range(N_KV_HEADS):
            qq = q_ref[pl.ds(r0, BLK), g * 2 * LANES:(g + 1) * 2 * LANES]
            lhs = jnp.concatenate([qq[:, 0:LANES], qq[:, LANES:2 * LANES]], axis=0)
            sink_terms = [[None, None], [None, None]]
            for par in range(2):
                kb = band(kx_ref, kb0_ref, j, slice((2 * g + par) * LANES, (2 * g + par + 1) * LANES))
                logits = _dot_nt(lhs, kb)
                for pair in range(2):
                    h = g * GQA_GROUP + 2 * pair + par
                    lg = logits[pair * BLK:(pair + 1) * BLK, :] + bias_ref[first, h]
                    sink = sink_ref[h] * LOG2E
                    m = jnp.maximum(jnp.max(lg, axis=-1, keepdims=True), sink)
                    p_ref[slot, 2 * g + par, pair * BLK:(pair + 1) * BLK, :] = jnp.exp2(lg - m).astype(_BF16)
                    sink_terms[pair][par] = jnp.exp2(sink - m)
            for pair in range(2):
                st_ref[slot, 2 * g + pair] = jnp.where(lane_lo, sink_terms[pair][0], sink_terms[pair][1])

    def values_stage(j):
        r0 = j * BLK
        slot = j % (LOOKAHEAD + 1)
        pairs = []
        for g in range(N_KV_HEADS):
            out = None
            for par in range(2):
                c = (4 * g + 2 * par) * LANES
                o = _dot(p_ref[slot, 2 * g + par],
                         band(vx_ref, vb0_ref, j, slice(c, c + 2 * LANES)))
                out = o if out is None else out + o
            for pair in range(2):
                blk = out[pair * BLK:(pair + 1) * BLK, :]
                den = blk[:, LANES:2 * LANES] + st_ref[slot, 2 * g + pair]
                pairs.append(blk[:, 0:LANES] / den)
        ya = jnp.concatenate(pairs, axis=1)
        yan_ref[r0:r0 + BLK, :] = _rms(ya, ga_ref[...]).astype(_BF16)

    def conv_piece(piece):
        half = piece % 2
        cols = slice(half * MXU_TILE, (half + 1) * MXU_TILE)
        w_cols = lambda base: w_ref[:, base + half * MXU_TILE:base + (half + 1) * MXU_TILE]
        if piece < 2:
            gatec_ref[:, cols] = _dot(u, w_cols(CONV_WIDTH))
        elif piece < 4:
            zbuf[HALO:HALO + ts, cols] = gatec_ref[:, cols] * _dot(u, w_cols(2 * CONV_WIDTH))
        else:
            gateb_ref[:, cols] = _dot(u, w_cols(0))

    piece_after_block = {0: 1, 1: 2, 3: 3, 5: 4, 6: 5}
    conv_piece(0)
    for j in range(LOOKAHEAD):
        scores_stage(j)
    for j in range(nblk):
        if j + LOOKAHEAD < nblk:
            scores_stage(j + LOOKAHEAD)
        if j in piece_after_block:
            conv_piece(piece_after_block[j])
        values_stage(j)

    conv = (cw_ref[0:1, :] * zbuf[HALO - 2:HALO - 2 + ts, :]
            + cw_ref[1:2, :] * zbuf[HALO - 1:HALO - 1 + ts, :]
            + cw_ref[2:3, :] * zbuf[HALO:HALO + ts, :])
    yc = _rms(gateb_ref[...] * conv, gc_ref[...]).astype(_BF16)
    zbuf[0:HALO, :] = zbuf[ts:ts + HALO, :]

    h = (x_ref[...] + _dot(yan_ref[...], wo_ref[CONV_WIDTH:CONV_WIDTH + ATTN_WIDTH, :])
         + _dot(yc, wo_ref[0:CONV_WIDTH, :]))
    _store_interleaved(o_ref, h)


def _mix_attn(tab, sinks, x2, g_mix, w_in, conv_w, gq, gk, g_conv, bkt, g_attn, w_out, w_up, w_down,
              tiles_per_seq):
    n = x2.shape[0]
    ts = ROW_TILE
    steps = n // ts
    assert ts // BLK >= 8 and CONV_WIDTH == 2 * MXU_TILE
    smem = pl.BlockSpec(memory_space=pltpu.SMEM)

    def slice_spec(w):
        rows, cols = w.shape
        assert rows % (steps * 2 * SUBLANES) == 0
        return pl.BlockSpec((rows // steps, cols), lambda i: (i, 0))

    cast_weights = (w_up, w_down)
    return pl.pallas_call(
        functools.partial(_mix_attn_kernel, tiles_per_seq),
        grid=(steps,),
        in_specs=[
            smem,
            smem,
            pl.BlockSpec((ts, D_MODEL), lambda i: (i, 0)),
            _const_spec((1, D_MODEL)),
            _const_spec((D_MODEL, IN_WIDTH)),
            _const_spec((3, CONV_WIDTH)),
            _const_spec((1, HEAD_DIM)),
            _const_spec((1, HEAD_DIM)),
            _const_spec((1, CONV_WIDTH)),
            _const_spec((BLK, 2 * BLK)),
            _const_spec((1, ATTN_WIDTH)),
            _const_spec((CONV_WIDTH + ATTN_WIDTH, D_MODEL)),
        ] + [slice_spec(w) for w in cast_weights],
        out_specs=[pl.BlockSpec((D_MODEL // LANES, ts, LANES), lambda i: (0, i, 0))]
        + [slice_spec(w) for w in cast_weights],
        out_shape=[jax.ShapeDtypeStruct((D_MODEL // LANES, n, LANES), _F32)]
        + [jax.ShapeDtypeStruct(w.shape, _BF16) for w in cast_weights],
        scratch_shapes=[
            pltpu.VMEM((2, N_HEADS, BLK, 2 * BLK), _F32),
            pltpu.VMEM((D_MODEL, IN_WIDTH), _BF16),
            pltpu.VMEM((CONV_WIDTH + ATTN_WIDTH, D_MODEL), _BF16),
            pltpu.VMEM((1, ATTN_WIDTH), _F32),
            pltpu.VMEM((1, KV_WIDTH), _F32),
            pltpu.VMEM((ts, ATTN_WIDTH), _BF16),
            pltpu.VMEM((ts, KX_WIDTH), _BF16),
            pltpu.VMEM((ts, VX_WIDTH), _BF16),
            pltpu.VMEM((2 * BLK, KX_WIDTH), _BF16),
            pltpu.VMEM((2 * BLK, VX_WIDTH), _BF16),
            pltpu.VMEM((ts, CONV_WIDTH), _F32),
            pltpu.VMEM((ts, CONV_WIDTH), _F32),
            pltpu.VMEM((ts + HALO, CONV_WIDTH), _F32),
            pltpu.VMEM((ts, ATTN_WIDTH), _BF16),
            pltpu.VMEM((LOOKAHEAD + 1, 2 * N_KV_HEADS, 2 * BLK, 2 * BLK), _BF16),
            pltpu.VMEM((LOOKAHEAD + 1, 2 * N_KV_HEADS, BLK, LANES), _F32),
        ],
        compiler_params=pltpu.CompilerParams(
            dimension_semantics=("arbitrary",), vmem_limit_bytes=VMEM_LIMIT_BYTES),
        name="mix_attn",
    )(tab, sinks, x2, g_mix, w_in, conv_w, gq, gk, g_conv, bkt, g_attn, w_out, *cast_weights)


def _ffn_kernel(tiles_per_seq, h_ref, g_ref, wu_ref, cw_ref, cb_ref, wd_ref, o_ref,
                u_ref, ubuf, carry_ref, act_ref, res_ref):
    ts = o_ref.shape[0]
    i = pl.program_id(0)
    g2 = 2 * SUBLANES

    @pl.when(i % tiles_per_seq == 0)
    def _():
        carry_ref[...] = jnp.zeros(carry_ref.shape, _F32)

    h = jnp.concatenate([h_ref[l] for l in range(D_MODEL // LANES)], axis=1)
    u_ref[...] = _rms(h, g_ref[...]).astype(_BF16)
    first_sublane = lax.broadcasted_iota(jnp.int32, (SUBLANES, FF_CHUNK), 0) == 0

    def conv_half(c, part):
        slot = 2 * c + part
        buf = ubuf.at[slot % ubuf.shape[0]]
        cols = slice(part * D_FF + c * FF_CHUNK, part * D_FF + (c + 1) * FF_CHUNK)
        buf[g2:g2 + ts, :] = _dot(u_ref[...], wu_ref[:, cols])
        for k in range(2):
            cur = pltpu.roll(buf[ts + k * SUBLANES:ts + (k + 1) * SUBLANES, :], 1, axis=0)
            prev = pltpu.roll(carry_ref[slot, k * SUBLANES:(k + 1) * SUBLANES, :], 1, axis=0)
            buf[k * SUBLANES:(k + 1) * SUBLANES, :] = jnp.where(first_sublane, prev, cur)
        a = (cw_ref[0:1, cols] * buf[0:ts, :]
             + cw_ref[1:2, cols] * buf[SUBLANES:SUBLANES + ts, :]
             + cw_ref[2:3, cols] * buf[g2:g2 + ts, :] + cb_ref[:, cols])
        carry_ref[slot] = buf[ts:ts + g2, :]
        return a

    for c in range(N_FF_CHUNKS):
        gate = conv_half(c, 0)
        val = conv_half(c, 1)
        act_ref[:, c * FF_CHUNK:(c + 1) * FF_CHUNK] = (gate / (1.0 + jnp.exp(-gate)) * val).astype(_BF16)

    res = h + _dot(act_ref[...], wd_ref[...])
    for l in range(D_MODEL // LANES):
        res_ref[l] = res[:, l * LANES:(l + 1) * LANES]
    _load_deinterleaved(res_ref, o_ref)


def _ffn(h, g_ffn, w_up, conv_w, conv_b, w_down, tiles_per_seq):
    n = h.shape[1]
    ts = ROW_TILE
    g2 = 2 * SUBLANES
    return pl.pallas_call(
        functools.partial(_ffn_kernel, tiles_per_seq),
        grid=(n // ts,),
        in_specs=[
            pl.BlockSpec((D_MODEL // LANES, ts, LANES), lambda i: (0, i, 0)),
            _const_spec((1, D_MODEL)),
            _const_spec((D_MODEL, 2 * D_FF)),
            _const_spec((3, 2 * D_FF)),
            _const_spec((1, 2 * D_FF)),
            _const_spec((D_FF, D_MODEL)),
        ],
        out_specs=pl.BlockSpec((ts, D_MODEL), lambda i: (i, 0)),
        out_shape=jax.ShapeDtypeStruct((n, D_MODEL), _F32),
        scratch_shapes=[
            pltpu.VMEM((ts, D_MODEL), _BF16),
            pltpu.VMEM((4, ts + g2, FF_CHUNK), _F32),
            pltpu.VMEM((2 * N_FF_CHUNKS, g2, FF_CHUNK), _F32),
            pltpu.VMEM((ts, D_FF), _BF16),
            pltpu.VMEM((D_MODEL // LANES, ts, LANES), _F32),
        ],
        compiler_params=pltpu.CompilerParams(
            dimension_semantics=("arbitrary",), vmem_limit_bytes=VMEM_LIMIT_BYTES),
        name="ffn",
    )(h, g_ffn, w_up, conv_w, conv_b, w_down)


def kernel(x, norm_mix_g, w_in, conv_w, q_norm_g, k_norm_g, rel_bias_table, sinks, out_norm_conv_g,
           out_norm_attn_g, w_out, norm_ffn_g, w_up, ffn_conv_w, ffn_conv_b, w_down):
    bsz, seq, d = x.shape
    assert d == D_MODEL and seq % ROW_TILE == 0 and norm_mix_g.shape[0] == 1
    tiles_per_seq = seq // ROW_TILE
    n = bsz * seq
    x2 = x.reshape(n, d)
    bkt = jnp.asarray(_bucket_map())

    h, w_up_b, w_down_b = _mix_attn(
        rel_bias_table, sinks[0], x2, norm_mix_g[0][None, :], w_in[0], conv_w[0], q_norm_g, k_norm_g,
        out_norm_conv_g[0][None, :], bkt, out_norm_attn_g[0][None, :], w_out[0],
        w_up[0], w_down[0], tiles_per_seq)
    out = _ffn(h, norm_ffn_g[0][None, :], w_up_b, ffn_conv_w[0], ffn_conv_b[0][None, :],
               w_down_b, tiles_per_seq)
    return out.reshape(bsz, seq, d)
```

```python
import functools
import math

import jax
import jax.numpy as jnp
import numpy as np
from jax import lax
from jax.experimental import pallas as pl
from jax.experimental.pallas import tpu as pltpu

D_MODEL = 1024
CONV_WIDTH = 512
HEAD_DIM = 64
N_HEADS = 8
N_KV_HEADS = 2
GQA_GROUP = N_HEADS // N_KV_HEADS
ATTN_WIDTH = N_HEADS * HEAD_DIM
KV_WIDTH = N_KV_HEADS * HEAD_DIM
WINDOW = 128
BLK = 128
NUM_BUCKETS = 32
MAX_DISTANCE = 128
MAX_EXACT = NUM_BUCKETS // 2
D_FF = 2816
EPS = 1e-6
NEG_INF = -1e30
LOG2E = math.log2(math.e)
IN_WIDTH = 3 * CONV_WIDTH + ATTN_WIDTH + 2 * KV_WIDTH

LANES = 128
SUBLANES = 8
MXU_TILE = 256
PAIR = 2 * HEAD_DIM
assert PAIR == LANES and KV_WIDTH == LANES and GQA_GROUP == 4
KX_WIDTH = 2 * N_KV_HEADS * LANES
VX_WIDTH = 2 * KX_WIDTH

ROW_TILE = 1024
FF_CHUNK = MXU_TILE
N_FF_CHUNKS = D_FF // FF_CHUNK
HALO = 8
LOOKAHEAD = 1
VMEM_LIMIT_BYTES = 60 * 1024 * 1024

_BF16 = jnp.bfloat16
_F32 = jnp.float32


def _dot(a, b):
    return jnp.dot(a, b, preferred_element_type=_F32)


def _dot_nt(a, b):
    return lax.dot_general(a, b, (((1,), (1,)), ((), ())), preferred_element_type=_F32)


def _rms(xf, g):
    return xf * lax.rsqrt(jnp.mean(xf * xf, axis=-1, keepdims=True) + EPS) * g


def _bucket_map():
    q = np.arange(BLK, dtype=np.int32)[:, None]
    j = np.arange(2 * BLK, dtype=np.int32)[None, :]
    d = q + BLK - j
    n = np.maximum(d, 0)
    nf = np.maximum(n, 1).astype(np.float32)
    large = MAX_EXACT + (np.log(nf / MAX_EXACT) / math.log(MAX_DISTANCE / MAX_EXACT)
                         * (NUM_BUCKETS - MAX_EXACT)).astype(np.int32)
    large = np.minimum(large, NUM_BUCKETS - 1)
    bucket = np.where(n < MAX_EXACT, n, large).astype(np.int32)
    within = (d >= 0) & (d < WINDOW)
    return np.where(within, bucket, -1).astype(np.int32)


def _head_mean_matrix():
    i = np.arange(ATTN_WIDTH)
    m = (i[:, None] // HEAD_DIM == i[None, :] // HEAD_DIM).astype(np.float32) / HEAD_DIM
    return jnp.asarray(m, dtype=_BF16)


def _const_spec(shape):
    return pl.BlockSpec(shape, lambda i: (0,) * len(shape), pipeline_mode=pl.Buffered(1))


def _store_interleaved(slab_ref, val):
    ts = val.shape[0]
    seg = ts // SUBLANES
    for l in range(D_MODEL // LANES):
        for s in range(SUBLANES):
            slab_ref[l, pl.ds(s, seg, stride=SUBLANES), :] = val[s * seg:(s + 1) * seg, l * LANES:(l + 1) * LANES]


def _load_deinterleaved(slab_ref, out_ref):
    ts = out_ref.shape[0]
    seg = ts // SUBLANES
    for l in range(D_MODEL // LANES):
        for s in range(SUBLANES):
            out_ref[s * seg:(s + 1) * seg, l * LANES:(l + 1) * LANES] = slab_ref[l, pl.ds(s, seg, stride=SUBLANES), :]


def _lo_hi_layout(a, lo):
    ar = pltpu.roll(a, HEAD_DIM, axis=1)
    zero = jnp.zeros_like(a)
    return [jnp.where(lo, a, zero), jnp.where(lo, zero, ar), jnp.where(lo, ar, zero), jnp.where(lo, zero, a)]


def _mix_attn_kernel(tiles_per_seq, tab_ref, sink_ref, x_ref, g_ref, w_f32, cw_ref, gq_head, gk_head, gc_ref,
                     hm_ref, bkt_ref, ga_ref, wo_f32, wu_f32, wd_f32,
                     o_ref, wu_bf16, wd_bf16,
                     bias_ref, w_ref, wo_ref, gq_ref, gk_ref, q_ref, kx_ref, vx_ref, kb0_ref, vb0_ref, gatec_ref, gateb_ref, zbuf,
                     yan_ref, p_ref, st_ref):
    ts = x_ref.shape[0]
    nblk = ts // BLK
    i = pl.program_id(0)
    seq_start = (i % tiles_per_seq == 0)

    wu_bf16[...] = wu_f32[...].astype(_BF16)
    wd_bf16[...] = wd_f32[...].astype(_BF16)

    @pl.when(i == 0)
    def _():
        bkt = bkt_ref[...]
        col = lax.broadcasted_iota(jnp.int32, (BLK, 2 * BLK), 1)

        def head_body(h, carry):
            b = jnp.full((BLK, 2 * BLK), NEG_INF, _F32)
            for t in range(NUM_BUCKETS):
                b = jnp.where(bkt == t, tab_ref[t, h] * LOG2E, b)
            bias_ref[0, h] = b
            bias_ref[1, h] = jnp.where(col >= BLK, b, NEG_INF)
            return carry

        lax.fori_loop(0, N_HEADS, head_body, 0)
        w_ref[...] = w_f32[...].astype(_BF16)
        wo_ref[...] = wo_f32[...].astype(_BF16)
        for h in range(N_HEADS):
            gq_ref[:, h * HEAD_DIM:(h + 1) * HEAD_DIM] = gq_head[...]
        for h in range(N_KV_HEADS):
            gk_ref[:, h * HEAD_DIM:(h + 1) * HEAD_DIM] = gk_head[...]
        kx_ref[ts - BLK:ts, :] = jnp.zeros((BLK, KX_WIDTH), _BF16)
        vx_ref[ts - BLK:ts, :] = jnp.zeros((BLK, VX_WIDTH), _BF16)

    @pl.when(seq_start)
    def _():
        zbuf[0:HALO, :] = jnp.zeros((HALO, CONV_WIDTH), _F32)

    kb0_ref[0:BLK, :] = kx_ref[ts - BLK:ts, :]
    vb0_ref[0:BLK, :] = vx_ref[ts - BLK:ts, :]

    u = _rms(x_ref[...], g_ref[...]).astype(_BF16)
    c0 = 3 * CONV_WIDTH
    c1 = c0 + ATTN_WIDTH
    q = _dot(u, w_ref[:, c0:c0 + ATTN_WIDTH])
    kv = _dot(u, w_ref[:, c1:c1 + 2 * KV_WIDTH])
    k = kv[:, 0:KV_WIDTH]
    msq = _dot((q * q).astype(_BF16), hm_ref[...])
    msk = _dot((k * k).astype(_BF16), hm_ref[0:KV_WIDTH, 0:KV_WIDTH])

    q_ref[...] = (q * lax.rsqrt(msq + EPS) * gq_ref[...] * (HEAD_DIM ** -0.5 * LOG2E)).astype(_BF16)
    kn = k * lax.rsqrt(msk + EPS) * gk_ref[...]
    lo = lax.broadcasted_iota(jnp.int32, (ts, LANES), 1) < HEAD_DIM
    kx_ref[...] = jnp.concatenate(_lo_hi_layout(kn, lo), axis=1).astype(_BF16)
    ones = [jnp.where(lo, 1.0, 0.0).astype(_F32), jnp.where(lo, 0.0, 1.0).astype(_F32)]
    v_pieces = _lo_hi_layout(kv[:, KV_WIDTH:2 * KV_WIDTH], lo)
    vx_ref[...] = jnp.concatenate([piece for s, v in enumerate(v_pieces) for piece in (v, ones[s % 2])],
                                  axis=1).astype(_BF16)
    kb0_ref[BLK:2 * BLK, :] = kx_ref[0:BLK, :]
    vb0_ref[BLK:2 * BLK, :] = vx_ref[0:BLK, :]

    lane_lo = lax.broadcasted_iota(jnp.int32, (BLK, LANES), 1) < HEAD_DIM

    def band(tile_ref, band0_ref, j, cols):
        if j == 0:
            return band0_ref[:, cols]
        return tile_ref[(j - 1) * BLK:(j + 1) * BLK, cols]

    def scores_stage(j):
        r0 = j * BLK
        slot = j % (LOOKAHEAD + 1)
        first = seq_start.astype(jnp.int32) if j == 0 else 0
        for g in range(N_KV_HEADS):
            qq = q_ref[pl.ds(r0, BLK), g * 2 * LANES:(g + 1) * 2 * LANES]
            lhs = jnp.concatenate([qq[:, 0:LANES], qq[:, LANES:2 * LANES]], axis=0)
            sink_terms = [[None, None], [None, None]]
            for par in range(2):
                kb = band(kx_ref, kb0_ref, j, slice((2 * g + par) * LANES, (2 * g + par + 1) * LANES))
                logits = _dot_nt(lhs, kb)
                for pair in range(2):
                    h = g * GQA_GROUP + 2 * pair + par
                    lg = logits[pair * BLK:(pair + 1) * BLK, :] + bias_ref[first, h]
                    sink = sink_ref[h] * LOG2E
                    m = jnp.maximum(jnp.max(lg, axis=-1, keepdims=True), sink)
                    p_ref[slot, 2 * g + par, pair * BLK:(pair + 1) * BLK, :] = jnp.exp2(lg - m).astype(_BF16)
                    sink_terms[pair][par] = jnp.exp2(sink - m)
            for pair in range(2):
                st_ref[slot, 2 * g + pair] = jnp.where(lane_lo, sink_terms[pair][0], sink_terms[pair][1])

    def values_stage(j):
        r0 = j * BLK
        slot = j % (LOOKAHEAD + 1)
        pairs = []
        for g in range(N_KV_HEADS):
            out = None
            for par in range(2):
                c = (4 * g + 2 * par) * LANES
                o = _dot(p_ref[slot, 2 * g + par],
                         band(vx_ref, vb0_ref, j, slice(c, c + 2 * LANES)))
                out = o if out is None else out + o
            for pair in range(2):
                blk = out[pair * BLK:(pair + 1) * BLK, :]
                den = blk[:, LANES:2 * LANES] + st_ref[slot, 2 * g + pair]
                pairs.append(blk[:, 0:LANES] / den)
        ya = jnp.concatenate(pairs, axis=1)
        yan_ref[r0:r0 + BLK, :] = _rms(ya, ga_ref[...]).astype(_BF16)

    def conv_piece(piece):
        half = piece % 2
        cols = slice(half * MXU_TILE, (half + 1) * MXU_TILE)
        w_cols = lambda base: w_ref[:, base + half * MXU_TILE:base + (half + 1) * MXU_TILE]
        if piece < 2:
            gatec_ref[:, cols] = _dot(u, w_cols(CONV_WIDTH))
        elif piece < 4:
            zbuf[HALO:HALO + ts, cols] = gatec_ref[:, cols] * _dot(u, w_cols(2 * CONV_WIDTH))
        else:
            gateb_ref[:, cols] = _dot(u, w_cols(0))

    piece_after_block = {0: 1, 1: 2, 3: 3, 5: 4, 6: 5}
    conv_piece(0)
    for j in range(LOOKAHEAD):
        scores_stage(j)
    for j in range(nblk):
        if j + LOOKAHEAD < nblk:
            scores_stage(j + LOOKAHEAD)
        if j in piece_after_block:
            conv_piece(piece_after_block[j])
        values_stage(j)

    conv = (cw_ref[0:1, :] * zbuf[HALO - 2:HALO - 2 + ts, :]
            + cw_ref[1:2, :] * zbuf[HALO - 1:HALO - 1 + ts, :]
            + cw_ref[2:3, :] * zbuf[HALO:HALO + ts, :])
    yc = _rms(gateb_ref[...] * conv, gc_ref[...]).astype(_BF16)
    zbuf[0:HALO, :] = zbuf[ts:ts + HALO, :]

    h = (x_ref[...] + _dot(yan_ref[...], wo_ref[CONV_WIDTH:CONV_WIDTH + ATTN_WIDTH, :])
         + _dot(yc, wo_ref[0:CONV_WIDTH, :]))
    _store_interleaved(o_ref, h)


def _mix_attn(tab, sinks, x2, g_mix, w_in, conv_w, gq, gk, g_conv, hm, bkt, g_attn, w_out, w_up, w_down,
              tiles_per_seq):
    n = x2.shape[0]
    ts = ROW_TILE
    steps = n // ts
    assert ts // BLK >= 8 and CONV_WIDTH == 2 * MXU_TILE
    smem = pl.BlockSpec(memory_space=pltpu.SMEM)

    def slice_spec(w):
        rows, cols = w.shape
        assert rows % (steps * 2 * SUBLANES) == 0
        return pl.BlockSpec((rows // steps, cols), lambda i: (i, 0))

    cast_weights = (w_up, w_down)
    return pl.pallas_call(
        functools.partial(_mix_attn_kernel, tiles_per_seq),
        grid=(steps,),
        in_specs=[
            smem,
            smem,
            pl.BlockSpec((ts, D_MODEL), lambda i: (i, 0)),
            _const_spec((1, D_MODEL)),
            _const_spec((D_MODEL, IN_WIDTH)),
            _const_spec((3, CONV_WIDTH)),
            _const_spec((1, HEAD_DIM)),
            _const_spec((1, HEAD_DIM)),
            _const_spec((1, CONV_WIDTH)),
            _const_spec((ATTN_WIDTH, ATTN_WIDTH)),
            _const_spec((BLK, 2 * BLK)),
            _const_spec((1, ATTN_WIDTH)),
            _const_spec((CONV_WIDTH + ATTN_WIDTH, D_MODEL)),
        ] + [slice_spec(w) for w in cast_weights],
        out_specs=[pl.BlockSpec((D_MODEL // LANES, ts, LANES), lambda i: (0, i, 0))]
        + [slice_spec(w) for w in cast_weights],
        out_shape=[jax.ShapeDtypeStruct((D_MODEL // LANES, n, LANES), _F32)]
        + [jax.ShapeDtypeStruct(w.shape, _BF16) for w in cast_weights],
        scratch_shapes=[
            pltpu.VMEM((2, N_HEADS, BLK, 2 * BLK), _F32),
            pltpu.VMEM((D_MODEL, IN_WIDTH), _BF16),
            pltpu.VMEM((CONV_WIDTH + ATTN_WIDTH, D_MODEL), _BF16),
            pltpu.VMEM((1, ATTN_WIDTH), _F32),
            pltpu.VMEM((1, KV_WIDTH), _F32),
            pltpu.VMEM((ts, ATTN_WIDTH), _BF16),
            pltpu.VMEM((ts, KX_WIDTH), _BF16),
            pltpu.VMEM((ts, VX_WIDTH), _BF16),
            pltpu.VMEM((2 * BLK, KX_WIDTH), _BF16),
            pltpu.VMEM((2 * BLK, VX_WIDTH), _BF16),
            pltpu.VMEM((ts, CONV_WIDTH), _F32),
            pltpu.VMEM((ts, CONV_WIDTH), _F32),
            pltpu.VMEM((ts + HALO, CONV_WIDTH), _F32),
            pltpu.VMEM((ts, ATTN_WIDTH), _BF16),
            pltpu.VMEM((LOOKAHEAD + 1, 2 * N_KV_HEADS, 2 * BLK, 2 * BLK), _BF16),
            pltpu.VMEM((LOOKAHEAD + 1, 2 * N_KV_HEADS, BLK, LANES), _F32),
        ],
        compiler_params=pltpu.CompilerParams(
            dimension_semantics=("arbitrary",), vmem_limit_bytes=VMEM_LIMIT_BYTES),
        name="mix_attn",
    )(tab, sinks, x2, g_mix, w_in, conv_w, gq, gk, g_conv, hm, bkt, g_attn, w_out, *cast_weights)


def _ffn_kernel(tiles_per_seq, h_ref, g_ref, wu_ref, cw_ref, cb_ref, wd_ref, o_ref,
                u_ref, ubuf, carry_ref, act_ref, res_ref):
    ts = o_ref.shape[0]
    i = pl.program_id(0)
    g2 = 2 * SUBLANES

    @pl.when(i % tiles_per_seq == 0)
    def _():
        carry_ref[...] = jnp.zeros(carry_ref.shape, _F32)

    h = jnp.concatenate([h_ref[l] for l in range(D_MODEL // LANES)], axis=1)
    u_ref[...] = _rms(h, g_ref[...]).astype(_BF16)
    first_sublane = lax.broadcasted_iota(jnp.int32, (SUBLANES, FF_CHUNK), 0) == 0

    def conv_half(c, part):
        slot = 2 * c + part
        buf = ubuf.at[slot % ubuf.shape[0]]
        cols = slice(part * D_FF + c * FF_CHUNK, part * D_FF + (c + 1) * FF_CHUNK)
        buf[g2:g2 + ts, :] = _dot(u_ref[...], wu_ref[:, cols])
        for k in range(2):
            cur = pltpu.roll(buf[ts + k * SUBLANES:ts + (k + 1) * SUBLANES, :], 1, axis=0)
            prev = pltpu.roll(carry_ref[slot, k * SUBLANES:(k + 1) * SUBLANES, :], 1, axis=0)
            buf[k * SUBLANES:(k + 1) * SUBLANES, :] = jnp.where(first_sublane, prev, cur)
        a = (cw_ref[0:1, cols] * buf[0:ts, :]
             + cw_ref[1:2, cols] * buf[SUBLANES:SUBLANES + ts, :]
             + cw_ref[2:3, cols] * buf[g2:g2 + ts, :] + cb_ref[:, cols])
        carry_ref[slot] = buf[ts:ts + g2, :]
        return a

    for c in range(N_FF_CHUNKS):
        gate = conv_half(c, 0)
        val = conv_half(c, 1)
        act_ref[:, c * FF_CHUNK:(c + 1) * FF_CHUNK] = (gate / (1.0 + jnp.exp(-gate)) * val).astype(_BF16)

    res = h + _dot(act_ref[...], wd_ref[...])
    for l in range(D_MODEL // LANES):
        res_ref[l] = res[:, l * LANES:(l + 1) * LANES]
    _load_deinterleaved(res_ref, o_ref)


def _ffn(h, g_ffn, w_up, conv_w, conv_b, w_down, tiles_per_seq):
    n = h.shape[1]
    ts = ROW_TILE
    g2 = 2 * SUBLANES
    return pl.pallas_call(
        functools.partial(_ffn_kernel, tiles_per_seq),
        grid=(n // ts,),
        in_specs=[
            pl.BlockSpec((D_MODEL // LANES, ts, LANES), lambda i: (0, i, 0)),
            _const_spec((1, D_MODEL)),
            _const_spec((D_MODEL, 2 * D_FF)),
            _const_spec((3, 2 * D_FF)),
            _const_spec((1, 2 * D_FF)),
            _const_spec((D_FF, D_MODEL)),
        ],
        out_specs=pl.BlockSpec((ts, D_MODEL), lambda i: (i, 0)),
        out_shape=jax.ShapeDtypeStruct((n, D_MODEL), _F32),
        scratch_shapes=[
            pltpu.VMEM((ts, D_MODEL), _BF16),
            pltpu.VMEM((4, ts + g2, FF_CHUNK), _F32),
            pltpu.VMEM((2 * N_FF_CHUNKS, g2, FF_CHUNK), _F32),
            pltpu.VMEM((ts, D_FF), _BF16),
            pltpu.VMEM((D_MODEL // LANES, ts, LANES), _F32),
        ],
        compiler_params=pltpu.CompilerParams(
            dimension_semantics=("arbitrary",), vmem_limit_bytes=VMEM_LIMIT_BYTES),
        name="ffn",
    )(h, g_ffn, w_up, conv_w, conv_b, w_down)


def kernel(x, norm_mix_g, w_in, conv_w, q_norm_g, k_norm_g, rel_bias_table, sinks, out_norm_conv_g,
           out_norm_attn_g, w_out, norm_ffn_g, w_up, ffn_conv_w, ffn_conv_b, w_down):
    bsz, seq, d = x.shape
    assert d == D_MODEL and seq % ROW_TILE == 0 and norm_mix_g.shape[0] == 1
    tiles_per_seq = seq // ROW_TILE
    n = bsz * seq
    x2 = x.reshape(n, d)
    bkt = jnp.asarray(_bucket_map())
    hm = _head_mean_matrix()

    h, w_up_b, w_down_b = _mix_attn(
        rel_bias_table, sinks[0], x2, norm_mix_g[0][None, :], w_in[0], conv_w[0], q_norm_g, k_norm_g,
        out_norm_conv_g[0][None, :], hm, bkt, out_norm_attn_g[0][None, :], w_out[0],
        w_up[0], w_down[0], tiles_per_seq)
    out = _ffn(h, norm_ffn_g[0][None, :], w_up_b, ffn_conv_w[0], ffn_conv_b[0][None, :],
               w_down_b, tiles_per_seq)
    return out.reshape(bsz, seq, d)
```

```python
import functools
import math

import jax
import jax.numpy as jnp
import numpy as np
from jax import lax
from jax.experimental import pallas as pl
from jax.experimental.pallas import tpu as pltpu

D_MODEL = 1024
CONV_WIDTH = 512
HEAD_DIM = 64
N_HEADS = 8
N_KV_HEADS = 2
GQA_GROUP = N_HEADS // N_KV_HEADS
ATTN_WIDTH = N_HEADS * HEAD_DIM
KV_WIDTH = N_KV_HEADS * HEAD_DIM
WINDOW = 128
BLK = 128
NUM_BUCKETS = 32
MAX_DISTANCE = 128
MAX_EXACT = NUM_BUCKETS // 2
D_FF = 2816
EPS = 1e-6
NEG_INF = -1e30
LOG2E = math.log2(math.e)
IN_WIDTH = 3 * CONV_WIDTH + ATTN_WIDTH + 2 * KV_WIDTH

LANES = 128
SUBLANES = 8
MXU_TILE = 256
PAIR = 2 * HEAD_DIM
assert PAIR == LANES and KV_WIDTH == LANES and GQA_GROUP == 4
KX_WIDTH = 2 * N_KV_HEADS * LANES
VX_WIDTH = 2 * KX_WIDTH

ROW_TILE = 1024
FF_CHUNK = MXU_TILE
N_FF_CHUNKS = D_FF // FF_CHUNK
HALO = 8
LOOKAHEAD = 1
VMEM_LIMIT_BYTES = 60 * 1024 * 1024

_BF16 = jnp.bfloat16
_F32 = jnp.float32


def _dot(a, b):
    return jnp.dot(a, b, preferred_element_type=_F32)


def _dot_nt(a, b):
    return lax.dot_general(a, b, (((1,), (1,)), ((), ())), preferred_element_type=_F32)


def _rms(xf, g):
    return xf * lax.rsqrt(jnp.mean(xf * xf, axis=-1, keepdims=True) + EPS) * g


def _bucket_map():
    q = np.arange(BLK, dtype=np.int32)[:, None]
    j = np.arange(2 * BLK, dtype=np.int32)[None, :]
    d = q + BLK - j
    n = np.maximum(d, 0)
    nf = np.maximum(n, 1).astype(np.float32)
    large = MAX_EXACT + (np.log(nf / MAX_EXACT) / math.log(MAX_DISTANCE / MAX_EXACT)
                         * (NUM_BUCKETS - MAX_EXACT)).astype(np.int32)
    large = np.minimum(large, NUM_BUCKETS - 1)
    bucket = np.where(n < MAX_EXACT, n, large).astype(np.int32)
    within = (d >= 0) & (d < WINDOW)
    return np.where(within, bucket, -1).astype(np.int32)


def _head_mean_matrix():
    i = np.arange(ATTN_WIDTH)
    m = (i[:, None] // HEAD_DIM == i[None, :] // HEAD_DIM).astype(np.float32) / HEAD_DIM
    return jnp.asarray(m, dtype=_BF16)


def _const_spec(shape):
    return pl.BlockSpec(shape, lambda i: (0,) * len(shape), pipeline_mode=pl.Buffered(1))


def _store_interleaved(slab_ref, val):
    ts = val.shape[0]
    seg = ts // SUBLANES
    for l in range(D_MODEL // LANES):
        for s in range(SUBLANES):
            slab_ref[l, pl.ds(s, seg, stride=SUBLANES), :] = val[s * seg:(s + 1) * seg, l * LANES:(l + 1) * LANES]


def _load_deinterleaved(slab_ref, out_ref):
    ts = out_ref.shape[0]
    seg = ts // SUBLANES
    for l in range(D_MODEL // LANES):
        for s in range(SUBLANES):
            out_ref[s * seg:(s + 1) * seg, l * LANES:(l + 1) * LANES] = slab_ref[l, pl.ds(s, seg, stride=SUBLANES), :]


def _lo_hi_layout(a, lo):
    ar = pltpu.roll(a, HEAD_DIM, axis=1)
    zero = jnp.zeros_like(a)
    return [jnp.where(lo, a, zero), jnp.where(lo, zero, ar), jnp.where(lo, ar, zero), jnp.where(lo, zero, a)]


def _mix_attn_kernel(tiles_per_seq, tab_ref, sink_ref, x_ref, g_ref, w_f32, cw_ref, gq_head, gk_head, gc_ref,
                     hm_ref, bkt_ref, ga_ref, wo_f32, wu_f32, wd_f32,
                     o_ref, wu_bf16, wd_bf16,
                     bias_ref, w_ref, wo_ref, gq_ref, gk_ref, q_ref, kx_ref, vx_ref, kb0_ref, vb0_ref, gatec_ref, gateb_ref, zbuf,
                     yan_ref, p_ref, st_ref):
    ts = x_ref.shape[0]
    nblk = ts // BLK
    i = pl.program_id(0)
    seq_start = (i % tiles_per_seq == 0)

    wu_bf16[...] = wu_f32[...].astype(_BF16)
    wd_bf16[...] = wd_f32[...].astype(_BF16)

    @pl.when(i == 0)
    def _():
        bkt = bkt_ref[...]
        col = lax.broadcasted_iota(jnp.int32, (BLK, 2 * BLK), 1)

        def head_body(h, carry):
            b = jnp.full((BLK, 2 * BLK), NEG_INF, _F32)
            for t in range(NUM_BUCKETS):
                b = jnp.where(bkt == t, tab_ref[t, h] * LOG2E, b)
            bias_ref[0, h] = b
            bias_ref[1, h] = jnp.where(col >= BLK, b, NEG_INF)
            return carry

        lax.fori_loop(0, N_HEADS, head_body, 0)
        w_ref[...] = w_f32[...].astype(_BF16)
        wo_ref[...] = wo_f32[...].astype(_BF16)
        for h in range(N_HEADS):
            gq_ref[:, h * HEAD_DIM:(h + 1) * HEAD_DIM] = gq_head[...]
        for h in range(N_KV_HEADS):
            gk_ref[:, h * HEAD_DIM:(h + 1) * HEAD_DIM] = gk_head[...]
        kx_ref[ts - BLK:ts, :] = jnp.zeros((BLK, KX_WIDTH), _BF16)
        lane = lax.broadcasted_iota(jnp.int32, (ts, LANES), 1)
        for s in range(2 * N_KV_HEADS):
            in_half = (lane < HEAD_DIM) if s % 2 == 0 else (lane >= HEAD_DIM)
            vx_ref[:, (2 * s + 1) * LANES:(2 * s + 2) * LANES] = jnp.where(in_half, 1.0, 0.0).astype(_BF16)
            vx_ref[ts - BLK:ts, 2 * s * LANES:(2 * s + 1) * LANES] = jnp.zeros((BLK, LANES), _BF16)

    @pl.when(seq_start)
    def _():
        zbuf[0:HALO, :] = jnp.zeros((HALO, CONV_WIDTH), _F32)

    kb0_ref[0:BLK, :] = kx_ref[ts - BLK:ts, :]
    vb0_ref[0:BLK, :] = vx_ref[ts - BLK:ts, :]

    u = _rms(x_ref[...], g_ref[...]).astype(_BF16)
    c0 = 3 * CONV_WIDTH
    c1 = c0 + ATTN_WIDTH
    q = _dot(u, w_ref[:, c0:c0 + ATTN_WIDTH])
    kv = _dot(u, w_ref[:, c1:c1 + 2 * KV_WIDTH])
    k = kv[:, 0:KV_WIDTH]
    msq = _dot((q * q).astype(_BF16), hm_ref[...])
    msk = _dot((k * k).astype(_BF16), hm_ref[0:KV_WIDTH, 0:KV_WIDTH])

    q_ref[...] = (q * lax.rsqrt(msq + EPS) * gq_ref[...] * (HEAD_DIM ** -0.5 * LOG2E)).astype(_BF16)
    kn = k * lax.rsqrt(msk + EPS) * gk_ref[...]
    lo = lax.broadcasted_iota(jnp.int32, (ts, LANES), 1) < HEAD_DIM
    kx_ref[...] = jnp.concatenate(_lo_hi_layout(kn, lo), axis=1).astype(_BF16)
    for s, piece in enumerate(_lo_hi_layout(kv[:, KV_WIDTH:2 * KV_WIDTH], lo)):
        vx_ref[:, 2 * s * LANES:(2 * s + 1) * LANES] = piece.astype(_BF16)
    kb0_ref[BLK:2 * BLK, :] = kx_ref[0:BLK, :]
    vb0_ref[BLK:2 * BLK, :] = vx_ref[0:BLK, :]

    lane_lo = lax.broadcasted_iota(jnp.int32, (BLK, LANES), 1) < HEAD_DIM

    def band(tile_ref, band0_ref, j, cols):
        if j == 0:
            return band0_ref[:, cols]
        return tile_ref[(j - 1) * BLK:(j + 1) * BLK, cols]

    def scores_stage(j):
        r0 = j * BLK
        slot = j % (LOOKAHEAD + 1)
        first = seq_start.astype(jnp.int32) if j == 0 else 0
        for g in range(N_KV_HEADS):
            qq = q_ref[pl.ds(r0, BLK), g * 2 * LANES:(g + 1) * 2 * LANES]
            lhs = jnp.concatenate([qq[:, 0:LANES], qq[:, LANES:2 * LANES]], axis=0)
            sink_terms = [[None, None], [None, None]]
            for par in range(2):
                kb = band(kx_ref, kb0_ref, j, slice((2 * g + par) * LANES, (2 * g + par + 1) * LANES))
                logits = _dot_nt(lhs, kb)
                for pair in range(2):
                    h = g * GQA_GROUP + 2 * pair + par
                    lg = logits[pair * BLK:(pair + 1) * BLK, :] + bias_ref[first, h]
                    sink = sink_ref[h] * LOG2E
                    m = jnp.maximum(jnp.max(lg, axis=-1, keepdims=True), sink)
                    p_ref[slot, 2 * g + par, pair * BLK:(pair + 1) * BLK, :] = jnp.exp2(lg - m).astype(_BF16)
                    sink_terms[pair][par] = jnp.exp2(sink - m)
            for pair in range(2):
                st_ref[slot, 2 * g + pair] = jnp.where(lane_lo, sink_terms[pair][0], sink_terms[pair][1])

    def values_stage(j):
        r0 = j * BLK
        slot = j % (LOOKAHEAD + 1)
        pairs = []
        for g in range(N_KV_HEADS):
            out = None
            for par in range(2):
                c = (4 * g + 2 * par) * LANES
                o = _dot(p_ref[slot, 2 * g + par],
                         band(vx_ref, vb0_ref, j, slice(c, c + 2 * LANES)))
                out = o if out is None else out + o
            for pair in range(2):
                blk = out[pair * BLK:(pair + 1) * BLK, :]
                den = blk[:, LANES:2 * LANES] + st_ref[slot, 2 * g + pair]
                pairs.append(blk[:, 0:LANES] / den)
        ya = jnp.concatenate(pairs, axis=1)
        yan_ref[r0:r0 + BLK, :] = _rms(ya, ga_ref[...]).astype(_BF16)

    def conv_piece(piece):
        half = piece % 2
        cols = slice(half * MXU_TILE, (half + 1) * MXU_TILE)
        w_cols = lambda base: w_ref[:, base + half * MXU_TILE:base + (half + 1) * MXU_TILE]
        if piece < 2:
            gatec_ref[:, cols] = _dot(u, w_cols(CONV_WIDTH))
        elif piece < 4:
            zbuf[HALO:HALO + ts, cols] = gatec_ref[:, cols] * _dot(u, w_cols(2 * CONV_WIDTH))
        else:
            gateb_ref[:, cols] = _dot(u, w_cols(0))

    piece_after_block = {0: 1, 1: 2, 3: 3, 5: 4, 6: 5}
    conv_piece(0)
    for j in range(LOOKAHEAD):
        scores_stage(j)
    for j in range(nblk):
        if j + LOOKAHEAD < nblk:
            scores_stage(j + LOOKAHEAD)
        if j in piece_after_block:
            conv_piece(piece_after_block[j])
        values_stage(j)

    conv = (cw_ref[0:1, :] * zbuf[HALO - 2:HALO - 2 + ts, :]
            + cw_ref[1:2, :] * zbuf[HALO - 1:HALO - 1 + ts, :]
            + cw_ref[2:3, :] * zbuf[HALO:HALO + ts, :])
    yc = _rms(gateb_ref[...] * conv, gc_ref[...]).astype(_BF16)
    zbuf[0:HALO, :] = zbuf[ts:ts + HALO, :]

    h = (x_ref[...] + _dot(yan_ref[...], wo_ref[CONV_WIDTH:CONV_WIDTH + ATTN_WIDTH, :])
         + _dot(yc, wo_ref[0:CONV_WIDTH, :]))
    _store_interleaved(o_ref, h)


def _mix_attn(tab, sinks, x2, g_mix, w_in, conv_w, gq, gk, g_conv, hm, bkt, g_attn, w_out, w_up, w_down,
              tiles_per_seq):
    n = x2.shape[0]
    ts = ROW_TILE
    steps = n // ts
    assert ts // BLK >= 8 and CONV_WIDTH == 2 * MXU_TILE
    smem = pl.BlockSpec(memory_space=pltpu.SMEM)

    def slice_spec(w):
        rows, cols = w.shape
        assert rows % (steps * 2 * SUBLANES) == 0
        return pl.BlockSpec((rows // steps, cols), lambda i: (i, 0))

    cast_weights = (w_up, w_down)
    return pl.pallas_call(
        functools.partial(_mix_attn_kernel, tiles_per_seq),
        grid=(steps,),
        in_specs=[
            smem,
            smem,
            pl.BlockSpec((ts, D_MODEL), lambda i: (i, 0)),
            _const_spec((1, D_MODEL)),
            _const_spec((D_MODEL, IN_WIDTH)),
            _const_spec((3, CONV_WIDTH)),
            _const_spec((1, HEAD_DIM)),
            _const_spec((1, HEAD_DIM)),
            _const_spec((1, CONV_WIDTH)),
            _const_spec((ATTN_WIDTH, ATTN_WIDTH)),
            _const_spec((BLK, 2 * BLK)),
            _const_spec((1, ATTN_WIDTH)),
            _const_spec((CONV_WIDTH + ATTN_WIDTH, D_MODEL)),
        ] + [slice_spec(w) for w in cast_weights],
        out_specs=[pl.BlockSpec((D_MODEL // LANES, ts, LANES), lambda i: (0, i, 0))]
        + [slice_spec(w) for w in cast_weights],
        out_shape=[jax.ShapeDtypeStruct((D_MODEL // LANES, n, LANES), _F32)]
        + [jax.ShapeDtypeStruct(w.shape, _BF16) for w in cast_weights],
        scratch_shapes=[
            pltpu.VMEM((2, N_HEADS, BLK, 2 * BLK), _F32),
            pltpu.VMEM((D_MODEL, IN_WIDTH), _BF16),
            pltpu.VMEM((CONV_WIDTH + ATTN_WIDTH, D_MODEL), _BF16),
            pltpu.VMEM((1, ATTN_WIDTH), _F32),
            pltpu.VMEM((1, KV_WIDTH), _F32),
            pltpu.VMEM((ts, ATTN_WIDTH), _BF16),
            pltpu.VMEM((ts, KX_WIDTH), _BF16),
            pltpu.VMEM((ts, VX_WIDTH), _BF16),
            pltpu.VMEM((2 * BLK, KX_WIDTH), _BF16),
            pltpu.VMEM((2 * BLK, VX_WIDTH), _BF16),
            pltpu.VMEM((ts, CONV_WIDTH), _F32),
            pltpu.VMEM((ts, CONV_WIDTH), _F32),
            pltpu.VMEM((ts + HALO, CONV_WIDTH), _F32),
            pltpu.VMEM((ts, ATTN_WIDTH), _BF16),
            pltpu.VMEM((LOOKAHEAD + 1, 2 * N_KV_HEADS, 2 * BLK, 2 * BLK), _BF16),
            pltpu.VMEM((LOOKAHEAD + 1, 2 * N_KV_HEADS, BLK, LANES), _F32),
        ],
        compiler_params=pltpu.CompilerParams(
            dimension_semantics=("arbitrary",), vmem_limit_bytes=VMEM_LIMIT_BYTES),
        name="mix_attn",
    )(tab, sinks, x2, g_mix, w_in, conv_w, gq, gk, g_conv, hm, bkt, g_attn, w_out, *cast_weights)


def _ffn_kernel(tiles_per_seq, h_ref, g_ref, wu_ref, cw_ref, cb_ref, wd_ref, o_ref,
                u_ref, ubuf, carry_ref, act_ref, res_ref):
    ts = o_ref.shape[0]
    i = pl.program_id(0)
    g2 = 2 * SUBLANES

    @pl.when(i % tiles_per_seq == 0)
    def _():
        carry_ref[...] = jnp.zeros(carry_ref.shape, _F32)

    h = jnp.concatenate([h_ref[l] for l in range(D_MODEL // LANES)], axis=1)
    u_ref[...] = _rms(h, g_ref[...]).astype(_BF16)
    first_sublane = lax.broadcasted_iota(jnp.int32, (SUBLANES, FF_CHUNK), 0) == 0

    def conv_half(c, part):
        slot = 2 * c + part
        buf = ubuf.at[slot % ubuf.shape[0]]
        cols = slice(part * D_FF + c * FF_CHUNK, part * D_FF + (c + 1) * FF_CHUNK)
        buf[g2:g2 + ts, :] = _dot(u_ref[...], wu_ref[:, cols])
        for k in range(2):
            cur = pltpu.roll(buf[ts + k * SUBLANES:ts + (k + 1) * SUBLANES, :], 1, axis=0)
            prev = pltpu.roll(carry_ref[slot, k * SUBLANES:(k + 1) * SUBLANES, :], 1, axis=0)
            buf[k * SUBLANES:(k + 1) * SUBLANES, :] = jnp.where(first_sublane, prev, cur)
        a = (cw_ref[0:1, cols] * buf[0:ts, :]
             + cw_ref[1:2, cols] * buf[SUBLANES:SUBLANES + ts, :]
             + cw_ref[2:3, cols] * buf[g2:g2 + ts, :] + cb_ref[:, cols])
        carry_ref[slot] = buf[ts:ts + g2, :]
        return a

    for c in range(N_FF_CHUNKS):
        gate = conv_half(c, 0)
        val = conv_half(c, 1)
        act_ref[:, c * FF_CHUNK:(c + 1) * FF_CHUNK] = (gate / (1.0 + jnp.exp(-gate)) * val).astype(_BF16)

    res = h + _dot(act_ref[...], wd_ref[...])
    for l in range(D_MODEL // LANES):
        res_ref[l] = res[:, l * LANES:(l + 1) * LANES]
    _load_deinterleaved(res_ref, o_ref)


def _ffn(h, g_ffn, w_up, conv_w, conv_b, w_down, tiles_per_seq):
    n = h.shape[1]
    ts = ROW_TILE
    g2 = 2 * SUBLANES
    return pl.pallas_call(
        functools.partial(_ffn_kernel, tiles_per_seq),
        grid=(n // ts,),
        in_specs=[
            pl.BlockSpec((D_MODEL // LANES, ts, LANES), lambda i: (0, i, 0)),
            _const_spec((1, D_MODEL)),
            _const_spec((D_MODEL, 2 * D_FF)),
            _const_spec((3, 2 * D_FF)),
            _const_spec((1, 2 * D_FF)),
            _const_spec((D_FF, D_MODEL)),
        ],
        out_specs=pl.BlockSpec((ts, D_MODEL), lambda i: (i, 0)),
        out_shape=jax.ShapeDtypeStruct((n, D_MODEL), _F32),
        scratch_shapes=[
            pltpu.VMEM((ts, D_MODEL), _BF16),
            pltpu.VMEM((4, ts + g2, FF_CHUNK), _F32),
            pltpu.VMEM((2 * N_FF_CHUNKS, g2, FF_CHUNK), _F32),
            pltpu.VMEM((ts, D_FF), _BF16),
            pltpu.VMEM((D_MODEL // LANES, ts, LANES), _F32),
        ],
        compiler_params=pltpu.CompilerParams(
            dimension_semantics=("arbitrary",), vmem_limit_bytes=VMEM_LIMIT_BYTES),
        name="ffn",
    )(h, g_ffn, w_up, conv_w, conv_b, w_down)


def kernel(x, norm_mix_g, w_in, conv_w, q_norm_g, k_norm_g, rel_bias_table, sinks, out_norm_conv_g,
           out_norm_attn_g, w_out, norm_ffn_g, w_up, ffn_conv_w, ffn_conv_b, w_down):
    bsz, seq, d = x.shape
    assert d == D_MODEL and seq % ROW_TILE == 0 and norm_mix_g.shape[0] == 1
    tiles_per_seq = seq // ROW_TILE
    n = bsz * seq
    x2 = x.reshape(n, d)
    bkt = jnp.asarray(_bucket_map())
    hm = _head_mean_matrix()

    h, w_up_b, w_down_b = _mix_attn(
        rel_bias_table, sinks[0], x2, norm_mix_g[0][None, :], w_in[0], conv_w[0], q_norm_g, k_norm_g,
        out_norm_conv_g[0][None, :], hm, bkt, out_norm_attn_g[0][None, :], w_out[0],
        w_up[0], w_down[0], tiles_per_seq)
    out = _ffn(h, norm_ffn_g[0][None, :], w_up_b, ffn_conv_w[0], ffn_conv_b[0][None, :],
               w_down_b, tiles_per_seq)
    return out.reshape(bsz, seq, d)
```

```python
import functools
import math

import jax
import jax.numpy as jnp
import numpy as np
from jax import lax
from jax.experimental import pallas as pl
from jax.experimental.pallas import tpu as pltpu

D_MODEL = 1024
CONV_WIDTH = 512
HEAD_DIM = 64
N_HEADS = 8
N_KV_HEADS = 2
GQA_GROUP = N_HEADS // N_KV_HEADS
ATTN_WIDTH = N_HEADS * HEAD_DIM
KV_WIDTH = N_KV_HEADS * HEAD_DIM
WINDOW = 128
BLK = 128
NUM_BUCKETS = 32
MAX_DISTANCE = 128
MAX_EXACT = NUM_BUCKETS // 2
D_FF = 2816
EPS = 1e-6
NEG_INF = -1e30
LOG2E = math.log2(math.e)
IN_WIDTH = 3 * CONV_WIDTH + ATTN_WIDTH + 2 * KV_WIDTH

LANES = 128
SUBLANES = 8
MXU_TILE = 256
PAIR = 2 * HEAD_DIM
assert PAIR == LANES and KV_WIDTH == LANES and GQA_GROUP == 4
KX_WIDTH = 2 * N_KV_HEADS * LANES
VX_WIDTH = 2 * KX_WIDTH

ROW_TILE = 1024
FF_CHUNK = MXU_TILE
N_FF_CHUNKS = D_FF // FF_CHUNK
HALO = 8
LOOKAHEAD = 1
VMEM_LIMIT_BYTES = 60 * 1024 * 1024

_BF16 = jnp.bfloat16
_F32 = jnp.float32


def _dot(a, b):
    return jnp.dot(a, b, preferred_element_type=_F32)


def _dot_nt(a, b):
    return lax.dot_general(a, b, (((1,), (1,)), ((), ())), preferred_element_type=_F32)


def _rms(xf, g):
    return xf * lax.rsqrt(jnp.mean(xf * xf, axis=-1, keepdims=True) + EPS) * g


def _bucket_map():
    q = np.arange(BLK, dtype=np.int32)[:, None]
    j = np.arange(2 * BLK, dtype=np.int32)[None, :]
    d = q + BLK - j
    n = np.maximum(d, 0)
    nf = np.maximum(n, 1).astype(np.float32)
    large = MAX_EXACT + (np.log(nf / MAX_EXACT) / math.log(MAX_DISTANCE / MAX_EXACT)
                         * (NUM_BUCKETS - MAX_EXACT)).astype(np.int32)
    large = np.minimum(large, NUM_BUCKETS - 1)
    bucket = np.where(n < MAX_EXACT, n, large).astype(np.int32)
    within = (d >= 0) & (d < WINDOW)
    return np.where(within, bucket, -1).astype(np.int32)


def _head_mean_matrix():
    i = np.arange(ATTN_WIDTH)
    m = (i[:, None] // HEAD_DIM == i[None, :] // HEAD_DIM).astype(np.float32) / HEAD_DIM
    return jnp.asarray(m, dtype=_BF16)


def _const_spec(shape):
    return pl.BlockSpec(shape, lambda i: (0,) * len(shape), pipeline_mode=pl.Buffered(1))


def _store_interleaved(slab_ref, val):
    ts = val.shape[0]
    seg = ts // SUBLANES
    for l in range(D_MODEL // LANES):
        for s in range(SUBLANES):
            slab_ref[l, pl.ds(s, seg, stride=SUBLANES), :] = val[s * seg:(s + 1) * seg, l * LANES:(l + 1) * LANES]


def _load_deinterleaved(slab_ref, out_ref):
    ts = out_ref.shape[0]
    seg = ts // SUBLANES
    for l in range(D_MODEL // LANES):
        for s in range(SUBLANES):
            out_ref[s * seg:(s + 1) * seg, l * LANES:(l + 1) * LANES] = slab_ref[l, pl.ds(s, seg, stride=SUBLANES), :]


def _lo_hi_layout(a, lo):
    ar = pltpu.roll(a, HEAD_DIM, axis=1)
    zero = jnp.zeros_like(a)
    return [jnp.where(lo, a, zero), jnp.where(lo, zero, ar), jnp.where(lo, ar, zero), jnp.where(lo, zero, a)]


def _mix_attn_kernel(tiles_per_seq, tab_ref, sink_ref, x_ref, g_ref, w_f32, cw_ref, gq_head, gk_head, gc_ref,
                     hm_ref, bkt_ref, ga_ref, wo_f32, wu_f32, wd_f32,
                     o_ref, wu_bf16, wd_bf16,
                     bias_ref, w_ref, wo_ref, q_ref, kx_ref, vx_ref, kb0_ref, vb0_ref, gatec_ref, gateb_ref, zbuf,
                     yan_ref, p_ref, st_ref, gq_ref, gk_ref):
    ts = x_ref.shape[0]
    nblk = ts // BLK
    i = pl.program_id(0)
    seq_start = (i % tiles_per_seq == 0)

    wu_bf16[...] = wu_f32[...].astype(_BF16)
    wd_bf16[...] = wd_f32[...].astype(_BF16)

    @pl.when(i == 0)
    def _():
        bkt = bkt_ref[...]
        col = lax.broadcasted_iota(jnp.int32, (BLK, 2 * BLK), 1)

        def head_body(h, carry):
            b = jnp.full((BLK, 2 * BLK), NEG_INF, _F32)
            for t in range(NUM_BUCKETS):
                b = jnp.where(bkt == t, tab_ref[t, h] * LOG2E, b)
            bias_ref[0, h] = b
            bias_ref[1, h] = jnp.where(col >= BLK, b, NEG_INF)
            return carry

        lax.fori_loop(0, N_HEADS, head_body, 0)
        w_ref[...] = w_f32[...].astype(_BF16)
        wo_ref[...] = wo_f32[...].astype(_BF16)
        for h in range(N_HEADS):
            gq_ref[:, h * HEAD_DIM:(h + 1) * HEAD_DIM] = gq_head[...]
        for h in range(N_KV_HEADS):
            gk_ref[:, h * HEAD_DIM:(h + 1) * HEAD_DIM] = gk_head[...]
        kx_ref[ts - BLK:ts, :] = jnp.zeros((BLK, KX_WIDTH), _BF16)
        vx_ref[ts - BLK:ts, :] = jnp.zeros((BLK, VX_WIDTH), _BF16)

    @pl.when(seq_start)
    def _():
        zbuf[0:HALO, :] = jnp.zeros((HALO, CONV_WIDTH), _F32)

    kb0_ref[0:BLK, :] = kx_ref[ts - BLK:ts, :]
    vb0_ref[0:BLK, :] = vx_ref[ts - BLK:ts, :]

    u = _rms(x_ref[...], g_ref[...]).astype(_BF16)
    c0 = 3 * CONV_WIDTH
    c1 = c0 + ATTN_WIDTH
    q = _dot(u, w_ref[:, c0:c0 + ATTN_WIDTH])
    kv = _dot(u, w_ref[:, c1:c1 + 2 * KV_WIDTH])
    k = kv[:, 0:KV_WIDTH]
    msq = _dot((q * q).astype(_BF16), hm_ref[...])
    msk = _dot((k * k).astype(_BF16), hm_ref[0:KV_WIDTH, 0:KV_WIDTH])

    q_ref[...] = (q * lax.rsqrt(msq + EPS) * gq_ref[...] * (HEAD_DIM ** -0.5 * LOG2E)).astype(_BF16)
    kn = k * lax.rsqrt(msk + EPS) * gk_ref[...]
    lo = lax.broadcasted_iota(jnp.int32, (ts, LANES), 1) < HEAD_DIM
    kx_ref[...] = jnp.concatenate(_lo_hi_layout(kn, lo), axis=1).astype(_BF16)
    ones = [jnp.where(lo, 1.0, 0.0).astype(_F32), jnp.where(lo, 0.0, 1.0).astype(_F32)]
    v_pieces = _lo_hi_layout(kv[:, KV_WIDTH:2 * KV_WIDTH], lo)
    vx_ref[...] = jnp.concatenate([piece for s, v in enumerate(v_pieces) for piece in (v, ones[s % 2])],
                                  axis=1).astype(_BF16)
    kb0_ref[BLK:2 * BLK, :] = kx_ref[0:BLK, :]
    vb0_ref[BLK:2 * BLK, :] = vx_ref[0:BLK, :]

    lane_lo = lax.broadcasted_iota(jnp.int32, (BLK, LANES), 1) < HEAD_DIM

    def band(tile_ref, band0_ref, j, cols):
        if j == 0:
            return band0_ref[:, cols]
        return tile_ref[(j - 1) * BLK:(j + 1) * BLK, cols]

    def scores_stage(j):
        r0 = j * BLK
        slot = j % (LOOKAHEAD + 1)
        first = seq_start.astype(jnp.int32) if j == 0 else 0
        for g in range(N_KV_HEADS):
            qq = q_ref[pl.ds(r0, BLK), g * 2 * LANES:(g + 1) * 2 * LANES]
            lhs = jnp.concatenate([qq[:, 0:LANES], qq[:, LANES:2 * LANES]], axis=0)
            sink_terms = [[None, None], [None, None]]
            for par in range(2):
                kb = band(kx_ref, kb0_ref, j, slice((2 * g + par) * LANES, (2 * g + par + 1) * LANES))
                logits = _dot_nt(lhs, kb)
                for pair in range(2):
                    h = g * GQA_GROUP + 2 * pair + par
                    lg = logits[pair * BLK:(pair + 1) * BLK, :] + bias_ref[first, h]
                    sink = sink_ref[h] * LOG2E
                    m = jnp.maximum(jnp.max(lg, axis=-1, keepdims=True), sink)
                    p_ref[slot, 2 * g + par, pair * BLK:(pair + 1) * BLK, :] = jnp.exp2(lg - m).astype(_BF16)
                    sink_terms[pair][par] = jnp.exp2(sink - m)
            for pair in range(2):
                st_ref[slot, 2 * g + pair] = jnp.where(lane_lo, sink_terms[pair][0], sink_terms[pair][1])

    def values_stage(j):
        r0 = j * BLK
        slot = j % (LOOKAHEAD + 1)
        pairs = []
        for g in range(N_KV_HEADS):
            out = None
            for par in range(2):
                c = (4 * g + 2 * par) * LANES
                o = _dot(p_ref[slot, 2 * g + par],
                         band(vx_ref, vb0_ref, j, slice(c, c + 2 * LANES)))
                out = o if out is None else out + o
            for pair in range(2):
                blk = out[pair * BLK:(pair + 1) * BLK, :]
                den = blk[:, LANES:2 * LANES] + st_ref[slot, 2 * g + pair]
                pairs.append(blk[:, 0:LANES] / den)
        ya = jnp.concatenate(pairs, axis=1)
        yan_ref[r0:r0 + BLK, :] = _rms(ya, ga_ref[...]).astype(_BF16)

    def conv_piece(piece):
        half = piece % 2
        cols = slice(half * MXU_TILE, (half + 1) * MXU_TILE)
        w_cols = lambda base: w_ref[:, base + half * MXU_TILE:base + (half + 1) * MXU_TILE]
        if piece < 2:
            gatec_ref[:, cols] = _dot(u, w_cols(CONV_WIDTH))
        elif piece < 4:
            zbuf[HALO:HALO + ts, cols] = gatec_ref[:, cols] * _dot(u, w_cols(2 * CONV_WIDTH))
        else:
            gateb_ref[:, cols] = _dot(u, w_cols(0))

    piece_after_block = {0: 1, 1: 2, 3: 3, 5: 4, 6: 5}
    conv_piece(0)
    for j in range(LOOKAHEAD):
        scores_stage(j)
    for j in range(nblk):
        if j + LOOKAHEAD < nblk:
            scores_stage(j + LOOKAHEAD)
        if j in piece_after_block:
            conv_piece(piece_after_block[j])
        values_stage(j)

    conv = (cw_ref[0:1, :] * zbuf[HALO - 2:HALO - 2 + ts, :]
            + cw_ref[1:2, :] * zbuf[HALO - 1:HALO - 1 + ts, :]
            + cw_ref[2:3, :] * zbuf[HALO:HALO + ts, :])
    yc = _rms(gateb_ref[...] * conv, gc_ref[...]).astype(_BF16)
    zbuf[0:HALO, :] = zbuf[ts:ts + HALO, :]

    h = (x_ref[...] + _dot(yan_ref[...], wo_ref[CONV_WIDTH:CONV_WIDTH + ATTN_WIDTH, :])
         + _dot(yc, wo_ref[0:CONV_WIDTH, :]))
    _store_interleaved(o_ref, h)


def _mix_attn(tab, sinks, x2, g_mix, w_in, conv_w, gq, gk, g_conv, hm, bkt, g_attn, w_out, w_up, w_down,
              tiles_per_seq):
    n = x2.shape[0]
    ts = ROW_TILE
    steps = n // ts
    assert ts // BLK >= 8 and CONV_WIDTH == 2 * MXU_TILE
    smem = pl.BlockSpec(memory_space=pltpu.SMEM)

    def slice_spec(w):
        rows, cols = w.shape
        assert rows % (steps * 2 * SUBLANES) == 0
        return pl.BlockSpec((rows // steps, cols), lambda i: (i, 0))

    cast_weights = (w_up, w_down)
    return pl.pallas_call(
        functools.partial(_mix_attn_kernel, tiles_per_seq),
        grid=(steps,),
        in_specs=[
            smem,
            smem,
            pl.BlockSpec((ts, D_MODEL), lambda i: (i, 0)),
            _const_spec((1, D_MODEL)),
            _const_spec((D_MODEL, IN_WIDTH)),
            _const_spec((3, CONV_WIDTH)),
            _const_spec((1, HEAD_DIM)),
            _const_spec((1, HEAD_DIM)),
            _const_spec((1, CONV_WIDTH)),
            _const_spec((ATTN_WIDTH, ATTN_WIDTH)),
            _const_spec((BLK, 2 * BLK)),
            _const_spec((1, ATTN_WIDTH)),
            _const_spec((CONV_WIDTH + ATTN_WIDTH, D_MODEL)),
        ] + [slice_spec(w) for w in cast_weights],
        out_specs=[pl.BlockSpec((D_MODEL // LANES, ts, LANES), lambda i: (0, i, 0))]
        + [slice_spec(w) for w in cast_weights],
        out_shape=[jax.ShapeDtypeStruct((D_MODEL // LANES, n, LANES), _F32)]
        + [jax.ShapeDtypeStruct(w.shape, _BF16) for w in cast_weights],
        scratch_shapes=[
            pltpu.VMEM((2, N_HEADS, BLK, 2 * BLK), _F32),
            pltpu.VMEM((D_MODEL, IN_WIDTH), _BF16),
            pltpu.VMEM((CONV_WIDTH + ATTN_WIDTH, D_MODEL), _BF16),
            pltpu.VMEM((ts, ATTN_WIDTH), _BF16),
            pltpu.VMEM((ts, KX_WIDTH), _BF16),
            pltpu.VMEM((ts, VX_WIDTH), _BF16),
            pltpu.VMEM((2 * BLK, KX_WIDTH), _BF16),
            pltpu.VMEM((2 * BLK, VX_WIDTH), _BF16),
            pltpu.VMEM((ts, CONV_WIDTH), _F32),
            pltpu.VMEM((ts, CONV_WIDTH), _F32),
            pltpu.VMEM((ts + HALO, CONV_WIDTH), _F32),
            pltpu.VMEM((ts, ATTN_WIDTH), _BF16),
            pltpu.VMEM((LOOKAHEAD + 1, 2 * N_KV_HEADS, 2 * BLK, 2 * BLK), _BF16),
            pltpu.VMEM((LOOKAHEAD + 1, 2 * N_KV_HEADS, BLK, LANES), _F32),
            pltpu.VMEM((1, ATTN_WIDTH), _F32),
            pltpu.VMEM((1, KV_WIDTH), _F32),
        ],
        compiler_params=pltpu.CompilerParams(
            dimension_semantics=("arbitrary",), vmem_limit_bytes=VMEM_LIMIT_BYTES),
        name="mix_attn",
    )(tab, sinks, x2, g_mix, w_in, conv_w, gq, gk, g_conv, hm, bkt, g_attn, w_out, *cast_weights)


def _ffn_kernel(tiles_per_seq, h_ref, g_ref, wu_ref, cw_ref, cb_ref, wd_ref, o_ref,
                u_ref, ubuf, carry_ref, act_ref, res_ref):
    ts = o_ref.shape[0]
    i = pl.program_id(0)
    g2 = 2 * SUBLANES

    @pl.when(i % tiles_per_seq == 0)
    def _():
        carry_ref[...] = jnp.zeros(carry_ref.shape, _F32)

    h = jnp.concatenate([h_ref[l] for l in range(D_MODEL // LANES)], axis=1)
    u_ref[...] = _rms(h, g_ref[...]).astype(_BF16)
    first_sublane = lax.broadcasted_iota(jnp.int32, (SUBLANES, FF_CHUNK), 0) == 0

    def conv_half(c, part):
        slot = 2 * c + part
        buf = ubuf.at[slot % ubuf.shape[0]]
        cols = slice(part * D_FF + c * FF_CHUNK, part * D_FF + (c + 1) * FF_CHUNK)
        buf[g2:g2 + ts, :] = _dot(u_ref[...], wu_ref[:, cols])
        for k in range(2):
            cur = pltpu.roll(buf[ts + k * SUBLANES:ts + (k + 1) * SUBLANES, :], 1, axis=0)
            prev = pltpu.roll(carry_ref[slot, k * SUBLANES:(k + 1) * SUBLANES, :], 1, axis=0)
            buf[k * SUBLANES:(k + 1) * SUBLANES, :] = jnp.where(first_sublane, prev, cur)
        a = (cw_ref[0:1, cols] * buf[0:ts, :]
             + cw_ref[1:2, cols] * buf[SUBLANES:SUBLANES + ts, :]
             + cw_ref[2:3, cols] * buf[g2:g2 + ts, :] + cb_ref[:, cols])
        carry_ref[slot] = buf[ts:ts + g2, :]
        return a

    for c in range(N_FF_CHUNKS):
        gate = conv_half(c, 0)
        val = conv_half(c, 1)
        act_ref[:, c * FF_CHUNK:(c + 1) * FF_CHUNK] = (gate / (1.0 + jnp.exp(-gate)) * val).astype(_BF16)

    res = h + _dot(act_ref[...], wd_ref[...])
    for l in range(D_MODEL // LANES):
        res_ref[l] = res[:, l * LANES:(l + 1) * LANES]
    _load_deinterleaved(res_ref, o_ref)


def _ffn(h, g_ffn, w_up, conv_w, conv_b, w_down, tiles_per_seq):
    n = h.shape[1]
    ts = ROW_TILE
    g2 = 2 * SUBLANES
    return pl.pallas_call(
        functools.partial(_ffn_kernel, tiles_per_seq),
        grid=(n // ts,),
        in_specs=[
            pl.BlockSpec((D_MODEL // LANES, ts, LANES), lambda i: (0, i, 0)),
            _const_spec((1, D_MODEL)),
            _const_spec((D_MODEL, 2 * D_FF)),
            _const_spec((3, 2 * D_FF)),
            _const_spec((1, 2 * D_FF)),
            _const_spec((D_FF, D_MODEL)),
        ],
        out_specs=pl.BlockSpec((ts, D_MODEL), lambda i: (i, 0)),
        out_shape=jax.ShapeDtypeStruct((n, D_MODEL), _F32),
        scratch_shapes=[
            pltpu.VMEM((ts, D_MODEL), _BF16),
            pltpu.VMEM((4, ts + g2, FF_CHUNK), _F32),
            pltpu.VMEM((2 * N_FF_CHUNKS, g2, FF_CHUNK), _F32),
            pltpu.VMEM((ts, D_FF), _BF16),
            pltpu.VMEM((D_MODEL // LANES, ts, LANES), _F32),
        ],
        compiler_params=pltpu.CompilerParams(
            dimension_semantics=("arbitrary",), vmem_limit_bytes=VMEM_LIMIT_BYTES),
        name="ffn",
    )(h, g_ffn, w_up, conv_w, conv_b, w_down)


def kernel(x, norm_mix_g, w_in, conv_w, q_norm_g, k_norm_g, rel_bias_table, sinks, out_norm_conv_g,
           out_norm_attn_g, w_out, norm_ffn_g, w_up, ffn_conv_w, ffn_conv_b, w_down):
    bsz, seq, d = x.shape
    assert d == D_MODEL and seq % ROW_TILE == 0 and norm_mix_g.shape[0] == 1
    tiles_per_seq = seq // ROW_TILE
    n = bsz * seq
    x2 = x.reshape(n, d)
    bkt = jnp.asarray(_bucket_map())
    hm = _head_mean_matrix()

    h, w_up_b, w_down_b = _mix_attn(
        rel_bias_table, sinks[0], x2, norm_mix_g[0][None, :], w_in[0], conv_w[0], q_norm_g, k_norm_g,
        out_norm_conv_g[0][None, :], hm, bkt, out_norm_attn_g[0][None, :], w_out[0],
        w_up[0], w_down[0], tiles_per_seq)
    out = _ffn(h, norm_ffn_g[0][None, :], w_up_b, ffn_conv_w[0], ffn_conv_b[0][None, :],
               w_down_b, tiles_per_seq)
    return out.reshape(bsz, seq, d)
```

```python
import functools
import math

import jax
import jax.numpy as jnp
import numpy as np
from jax import lax
from jax.experimental import pallas as pl
from jax.experimental.pallas import tpu as pltpu

D_MODEL = 1024
CONV_WIDTH = 512
HEAD_DIM = 64
N_HEADS = 8
N_KV_HEADS = 2
GQA_GROUP = N_HEADS // N_KV_HEADS
ATTN_WIDTH = N_HEADS * HEAD_DIM
KV_WIDTH = N_KV_HEADS * HEAD_DIM
WINDOW = 128
BLK = 128
NUM_BUCKETS = 32
MAX_DISTANCE = 128
MAX_EXACT = NUM_BUCKETS // 2
D_FF = 2816
EPS = 1e-6
NEG_INF = -1e30
LOG2E = math.log2(math.e)
IN_WIDTH = 3 * CONV_WIDTH + ATTN_WIDTH + 2 * KV_WIDTH

LANES = 128
SUBLANES = 8
MXU_TILE = 256
PAIR = 2 * HEAD_DIM
assert PAIR == LANES and KV_WIDTH == LANES and GQA_GROUP == 4
KX_WIDTH = 2 * N_KV_HEADS * LANES
VX_WIDTH = 2 * KX_WIDTH

ROW_TILE = 1024
FF_CHUNK = MXU_TILE
N_FF_CHUNKS = D_FF // FF_CHUNK
HALO = 8
LOOKAHEAD = 1
VMEM_LIMIT_BYTES = 60 * 1024 * 1024

_BF16 = jnp.bfloat16
_F32 = jnp.float32


def _dot(a, b):
    return jnp.dot(a, b, preferred_element_type=_F32)


def _dot_nt(a, b):
    return lax.dot_general(a, b, (((1,), (1,)), ((), ())), preferred_element_type=_F32)


def _rms(xf, g):
    return xf * lax.rsqrt(jnp.mean(xf * xf, axis=-1, keepdims=True) + EPS) * g


def _bucket_map():
    q = np.arange(BLK, dtype=np.int32)[:, None]
    j = np.arange(2 * BLK, dtype=np.int32)[None, :]
    d = q + BLK - j
    n = np.maximum(d, 0)
    nf = np.maximum(n, 1).astype(np.float32)
    large = MAX_EXACT + (np.log(nf / MAX_EXACT) / math.log(MAX_DISTANCE / MAX_EXACT)
                         * (NUM_BUCKETS - MAX_EXACT)).astype(np.int32)
    large = np.minimum(large, NUM_BUCKETS - 1)
    bucket = np.where(n < MAX_EXACT, n, large).astype(np.int32)
    within = (d >= 0) & (d < WINDOW)
    return np.where(within, bucket, -1).astype(np.int32)


def _head_mean_matrix():
    i = np.arange(ATTN_WIDTH)
    m = (i[:, None] // HEAD_DIM == i[None, :] // HEAD_DIM).astype(np.float32) / HEAD_DIM
    return jnp.asarray(m, dtype=_BF16)


def _const_spec(shape):
    return pl.BlockSpec(shape, lambda i: (0,) * len(shape), pipeline_mode=pl.Buffered(1))


def _store_interleaved(slab_ref, val):
    ts = val.shape[0]
    seg = ts // SUBLANES
    for l in range(D_MODEL // LANES):
        for s in range(SUBLANES):
            slab_ref[l, pl.ds(s, seg, stride=SUBLANES), :] = val[s * seg:(s + 1) * seg, l * LANES:(l + 1) * LANES]


def _load_deinterleaved(slab_ref, out_ref):
    ts = out_ref.shape[0]
    seg = ts // SUBLANES
    for l in range(D_MODEL // LANES):
        for s in range(SUBLANES):
            out_ref[s * seg:(s + 1) * seg, l * LANES:(l + 1) * LANES] = slab_ref[l, pl.ds(s, seg, stride=SUBLANES), :]


def _lo_hi_layout(a, lo):
    ar = pltpu.roll(a, HEAD_DIM, axis=1)
    zero = jnp.zeros_like(a)
    return [jnp.where(lo, a, zero), jnp.where(lo, zero, ar), jnp.where(lo, ar, zero), jnp.where(lo, zero, a)]


def _mix_attn_kernel(tiles_per_seq, tab_ref, sink_ref, x_ref, g_ref, w_f32, cw_ref, gq_head, gk_head, gc_ref,
                     hm_ref, bkt_ref, ga_ref, wo_f32, wu_f32, wd_f32,
                     o_ref, wu_bf16, wd_bf16,
                     bias_ref, w_ref, wo_ref, gq_ref, gk_ref, q_ref, kx_ref, vx_ref, kb0_ref, vb0_ref, gatec_ref, gateb_ref, zbuf,
                     yan_ref, p_ref, st_ref):
    ts = x_ref.shape[0]
    nblk = ts // BLK
    i = pl.program_id(0)
    seq_start = (i % tiles_per_seq == 0)

    wu_bf16[...] = wu_f32[...].astype(_BF16)
    wd_bf16[...] = wd_f32[...].astype(_BF16)

    @pl.when(i == 0)
    def _():
        bkt = bkt_ref[...]
        col = lax.broadcasted_iota(jnp.int32, (BLK, 2 * BLK), 1)

        def head_body(h, carry):
            b = jnp.full((BLK, 2 * BLK), NEG_INF, _F32)
            for t in range(NUM_BUCKETS):
                b = jnp.where(bkt == t, tab_ref[t, h] * LOG2E, b)
            bias_ref[0, h] = b
            bias_ref[1, h] = jnp.where(col >= BLK, b, NEG_INF)
            return carry

        lax.fori_loop(0, N_HEADS, head_body, 0)
        w_ref[...] = w_f32[...].astype(_BF16)
        wo_ref[...] = wo_f32[...].astype(_BF16)
        for h in range(N_HEADS):
            gq_ref[:, h * HEAD_DIM:(h + 1) * HEAD_DIM] = gq_head[...]
        for h in range(N_KV_HEADS):
            gk_ref[:, h * HEAD_DIM:(h + 1) * HEAD_DIM] = gk_head[...]
        kx_ref[ts - BLK:ts, :] = jnp.zeros((BLK, KX_WIDTH), _BF16)
        vx_ref[ts - BLK:ts, :] = jnp.zeros((BLK, VX_WIDTH), _BF16)

    @pl.when(seq_start)
    def _():
        zbuf[0:HALO, :] = jnp.zeros((HALO, CONV_WIDTH), _F32)

    kb0_ref[0:BLK, :] = kx_ref[ts - BLK:ts, :]
    vb0_ref[0:BLK, :] = vx_ref[ts - BLK:ts, :]

    u = _rms(x_ref[...], g_ref[...]).astype(_BF16)
    c0 = 3 * CONV_WIDTH
    c1 = c0 + ATTN_WIDTH
    q = _dot(u, w_ref[:, c0:c0 + ATTN_WIDTH])
    kv = _dot(u, w_ref[:, c1:c1 + 2 * KV_WIDTH])
    k = kv[:, 0:KV_WIDTH]
    msq = _dot((q * q).astype(_BF16), hm_ref[...])
    msk = _dot((k * k).astype(_BF16), hm_ref[0:KV_WIDTH, 0:KV_WIDTH])

    q_ref[...] = (q * lax.rsqrt(msq + EPS) * gq_ref[...] * (HEAD_DIM ** -0.5 * LOG2E)).astype(_BF16)
    kn = k * lax.rsqrt(msk + EPS) * gk_ref[...]
    lo = lax.broadcasted_iota(jnp.int32, (ts, LANES), 1) < HEAD_DIM
    kx_ref[...] = jnp.concatenate(_lo_hi_layout(kn, lo), axis=1).astype(_BF16)
    ones = [jnp.where(lo, 1.0, 0.0).astype(_F32), jnp.where(lo, 0.0, 1.0).astype(_F32)]
    v_pieces = _lo_hi_layout(kv[:, KV_WIDTH:2 * KV_WIDTH], lo)
    vx_ref[...] = jnp.concatenate([piece for s, v in enumerate(v_pieces) for piece in (v, ones[s % 2])],
                                  axis=1).astype(_BF16)
    kb0_ref[BLK:2 * BLK, :] = kx_ref[0:BLK, :]
    vb0_ref[BLK:2 * BLK, :] = vx_ref[0:BLK, :]

    lane_lo = lax.broadcasted_iota(jnp.int32, (BLK, LANES), 1) < HEAD_DIM

    def band(tile_ref, band0_ref, j, cols):
        if j == 0:
            return band0_ref[:, cols]
        return tile_ref[(j - 1) * BLK:(j + 1) * BLK, cols]

    def scores_stage(j):
        r0 = j * BLK
        slot = j % (LOOKAHEAD + 1)
        first = seq_start.astype(jnp.int32) if j == 0 else 0
        for g in range(N_KV_HEADS):
            qq = q_ref[pl.ds(r0, BLK), g * 2 * LANES:(g + 1) * 2 * LANES]
            lhs = jnp.concatenate([qq[:, 0:LANES], qq[:, LANES:2 * LANES]], axis=0)
            sink_terms = [[None, None], [None, None]]
            for par in range(2):
                kb = band(kx_ref, kb0_ref, j, slice((2 * g + par) * LANES, (2 * g + par + 1) * LANES))
                logits = _dot_nt(lhs, kb)
                for pair in range(2):
                    h = g * GQA_GROUP + 2 * pair + par
                    lg = logits[pair * BLK:(pair + 1) * BLK, :] + bias_ref[first, h]
                    sink = sink_ref[0, h] * LOG2E
                    m = jnp.maximum(jnp.max(lg, axis=-1, keepdims=True), sink)
                    p_ref[slot, 2 * g + par, pair * BLK:(pair + 1) * BLK, :] = jnp.exp2(lg - m).astype(_BF16)
                    sink_terms[pair][par] = jnp.exp2(sink - m)
            for pair in range(2):
                st_ref[slot, 2 * g + pair] = jnp.where(lane_lo, sink_terms[pair][0], sink_terms[pair][1])

    def values_stage(j):
        r0 = j * BLK
        slot = j % (LOOKAHEAD + 1)
        pairs = []
        for g in range(N_KV_HEADS):
            out = None
            for par in range(2):
                c = (4 * g + 2 * par) * LANES
                o = _dot(p_ref[slot, 2 * g + par],
                         band(vx_ref, vb0_ref, j, slice(c, c + 2 * LANES)))
                out = o if out is None else out + o
            for pair in range(2):
                blk = out[pair * BLK:(pair + 1) * BLK, :]
                den = blk[:, LANES:2 * LANES] + st_ref[slot, 2 * g + pair]
                pairs.append(blk[:, 0:LANES] / den)
        ya = jnp.concatenate(pairs, axis=1)
        yan_ref[r0:r0 + BLK, :] = _rms(ya, ga_ref[...]).astype(_BF16)

    def conv_piece(piece):
        half = piece % 2
        cols = slice(half * MXU_TILE, (half + 1) * MXU_TILE)
        w_cols = lambda base: w_ref[:, base + half * MXU_TILE:base + (half + 1) * MXU_TILE]
        if piece < 2:
            gatec_ref[:, cols] = _dot(u, w_cols(CONV_WIDTH))
        elif piece < 4:
            zbuf[HALO:HALO + ts, cols] = gatec_ref[:, cols] * _dot(u, w_cols(2 * CONV_WIDTH))
        else:
            gateb_ref[:, cols] = _dot(u, w_cols(0))

    piece_after_block = {0: 1, 1: 2, 3: 3, 5: 4, 6: 5}
    conv_piece(0)
    for j in range(LOOKAHEAD):
        scores_stage(j)
    for j in range(nblk):
        if j + LOOKAHEAD < nblk:
            scores_stage(j + LOOKAHEAD)
        if j in piece_after_block:
            conv_piece(piece_after_block[j])
        values_stage(j)

    conv = (cw_ref[0, 0:1, :] * zbuf[HALO - 2:HALO - 2 + ts, :]
            + cw_ref[0, 1:2, :] * zbuf[HALO - 1:HALO - 1 + ts, :]
            + cw_ref[0, 2:3, :] * zbuf[HALO:HALO + ts, :])
    yc = _rms(gateb_ref[...] * conv, gc_ref[...]).astype(_BF16)
    zbuf[0:HALO, :] = zbuf[ts:ts + HALO, :]

    h = (x_ref[...] + _dot(yan_ref[...], wo_ref[CONV_WIDTH:CONV_WIDTH + ATTN_WIDTH, :])
         + _dot(yc, wo_ref[0:CONV_WIDTH, :]))
    _store_interleaved(o_ref, h)


def _mix_attn(tab, sinks, x2, g_mix, w_in, conv_w, gq, gk, g_conv, hm, bkt, g_attn, w_out, w_up, w_down,
              tiles_per_seq):
    n = x2.shape[0]
    ts = ROW_TILE
    steps = n // ts
    assert ts // BLK >= 8 and CONV_WIDTH == 2 * MXU_TILE
    smem = pl.BlockSpec(memory_space=pltpu.SMEM)

    def slice_spec(w):
        rows, cols = w.shape
        assert rows % (steps * 2 * SUBLANES) == 0
        return pl.BlockSpec((rows // steps, cols), lambda i: (i, 0))

    cast_weights = (w_up, w_down)
    return pl.pallas_call(
        functools.partial(_mix_attn_kernel, tiles_per_seq),
        grid=(steps,),
        in_specs=[
            smem,
            smem,
            pl.BlockSpec((ts, D_MODEL), lambda i: (i, 0)),
            _const_spec((1, D_MODEL)),
            _const_spec((D_MODEL, IN_WIDTH)),
            _const_spec((1, 3, CONV_WIDTH)),
            _const_spec((1, HEAD_DIM)),
            _const_spec((1, HEAD_DIM)),
            _const_spec((1, CONV_WIDTH)),
            _const_spec((ATTN_WIDTH, ATTN_WIDTH)),
            _const_spec((BLK, 2 * BLK)),
            _const_spec((1, ATTN_WIDTH)),
            _const_spec((CONV_WIDTH + ATTN_WIDTH, D_MODEL)),
        ] + [slice_spec(w) for w in cast_weights],
        out_specs=[pl.BlockSpec((D_MODEL // LANES, ts, LANES), lambda i: (0, i, 0))]
        + [slice_spec(w) for w in cast_weights],
        out_shape=[jax.ShapeDtypeStruct((D_MODEL // LANES, n, LANES), _F32)]
        + [jax.ShapeDtypeStruct(w.shape, _BF16) for w in cast_weights],
        scratch_shapes=[
            pltpu.VMEM((2, N_HEADS, BLK, 2 * BLK), _F32),
            pltpu.VMEM((D_MODEL, IN_WIDTH), _BF16),
            pltpu.VMEM((CONV_WIDTH + ATTN_WIDTH, D_MODEL), _BF16),
            pltpu.VMEM((1, ATTN_WIDTH), _F32),
            pltpu.VMEM((1, KV_WIDTH), _F32),
            pltpu.VMEM((ts, ATTN_WIDTH), _BF16),
            pltpu.VMEM((ts, KX_WIDTH), _BF16),
            pltpu.VMEM((ts, VX_WIDTH), _BF16),
            pltpu.VMEM((2 * BLK, KX_WIDTH), _BF16),
            pltpu.VMEM((2 * BLK, VX_WIDTH), _BF16),
            pltpu.VMEM((ts, CONV_WIDTH), _F32),
            pltpu.VMEM((ts, CONV_WIDTH), _F32),
            pltpu.VMEM((ts + HALO, CONV_WIDTH), _F32),
            pltpu.VMEM((ts, ATTN_WIDTH), _BF16),
            pltpu.VMEM((LOOKAHEAD + 1, 2 * N_KV_HEADS, 2 * BLK, 2 * BLK), _BF16),
            pltpu.VMEM((LOOKAHEAD + 1, 2 * N_KV_HEADS, BLK, LANES), _F32),
        ],
        compiler_params=pltpu.CompilerParams(
            dimension_semantics=("arbitrary",), vmem_limit_bytes=VMEM_LIMIT_BYTES),
        name="mix_attn",
    )(tab, sinks, x2, g_mix, w_in, conv_w, gq, gk, g_conv, hm, bkt, g_attn, w_out, *cast_weights)


def _ffn_kernel(tiles_per_seq, h_ref, g_ref, wu_ref, cw_ref, cb_ref, wd_ref, o_ref,
                u_ref, ubuf, carry_ref, act_ref, res_ref):
    ts = o_ref.shape[0]
    i = pl.program_id(0)
    g2 = 2 * SUBLANES

    @pl.when(i % tiles_per_seq == 0)
    def _():
        carry_ref[...] = jnp.zeros(carry_ref.shape, _F32)

    h = jnp.concatenate([h_ref[l] for l in range(D_MODEL // LANES)], axis=1)
    u_ref[...] = _rms(h, g_ref[...]).astype(_BF16)
    first_sublane = lax.broadcasted_iota(jnp.int32, (SUBLANES, FF_CHUNK), 0) == 0

    def conv_half(c, part):
        slot = 2 * c + part
        buf = ubuf.at[slot % ubuf.shape[0]]
        cols = slice(part * D_FF + c * FF_CHUNK, part * D_FF + (c + 1) * FF_CHUNK)
        buf[g2:g2 + ts, :] = _dot(u_ref[...], wu_ref[:, cols])
        for k in range(2):
            cur = pltpu.roll(buf[ts + k * SUBLANES:ts + (k + 1) * SUBLANES, :], 1, axis=0)
            prev = pltpu.roll(carry_ref[slot, k * SUBLANES:(k + 1) * SUBLANES, :], 1, axis=0)
            buf[k * SUBLANES:(k + 1) * SUBLANES, :] = jnp.where(first_sublane, prev, cur)
        a = (cw_ref[0, 0:1, cols] * buf[0:ts, :]
             + cw_ref[0, 1:2, cols] * buf[SUBLANES:SUBLANES + ts, :]
             + cw_ref[0, 2:3, cols] * buf[g2:g2 + ts, :] + cb_ref[:, cols])
        carry_ref[slot] = buf[ts:ts + g2, :]
        return a

    for c in range(N_FF_CHUNKS):
        gate = conv_half(c, 0)
        val = conv_half(c, 1)
        act_ref[:, c * FF_CHUNK:(c + 1) * FF_CHUNK] = (gate / (1.0 + jnp.exp(-gate)) * val).astype(_BF16)

    res = h + _dot(act_ref[...], wd_ref[...])
    for l in range(D_MODEL // LANES):
        res_ref[l] = res[:, l * LANES:(l + 1) * LANES]
    _load_deinterleaved(res_ref, o_ref)


def _ffn(h, g_ffn, w_up, conv_w, conv_b, w_down, tiles_per_seq):
    n = h.shape[1]
    ts = ROW_TILE
    g2 = 2 * SUBLANES
    return pl.pallas_call(
        functools.partial(_ffn_kernel, tiles_per_seq),
        grid=(n // ts,),
        in_specs=[
            pl.BlockSpec((D_MODEL // LANES, ts, LANES), lambda i: (0, i, 0)),
            _const_spec((1, D_MODEL)),
            _const_spec((D_MODEL, 2 * D_FF)),
            _const_spec((1, 3, 2 * D_FF)),
            _const_spec((1, 2 * D_FF)),
            _const_spec((D_FF, D_MODEL)),
        ],
        out_specs=pl.BlockSpec((ts, D_MODEL), lambda i: (i, 0)),
        out_shape=jax.ShapeDtypeStruct((n, D_MODEL), _F32),
        scratch_shapes=[
            pltpu.VMEM((ts, D_MODEL), _BF16),
            pltpu.VMEM((4, ts + g2, FF_CHUNK), _F32),
            pltpu.VMEM((2 * N_FF_CHUNKS, g2, FF_CHUNK), _F32),
            pltpu.VMEM((ts, D_FF), _BF16),
            pltpu.VMEM((D_MODEL // LANES, ts, LANES), _F32),
        ],
        compiler_params=pltpu.CompilerParams(
            dimension_semantics=("arbitrary",), vmem_limit_bytes=VMEM_LIMIT_BYTES),
        name="ffn",
    )(h, g_ffn, w_up, conv_w, conv_b, w_down)


def kernel(x, norm_mix_g, w_in, conv_w, q_norm_g, k_norm_g, rel_bias_table, sinks, out_norm_conv_g,
           out_norm_attn_g, w_out, norm_ffn_g, w_up, ffn_conv_w, ffn_conv_b, w_down):
    bsz, seq, d = x.shape
    assert d == D_MODEL and seq % ROW_TILE == 0 and norm_mix_g.shape[0] == 1
    tiles_per_seq = seq // ROW_TILE
    n = bsz * seq
    x2 = x.reshape(n, d)
    bkt = jnp.asarray(_bucket_map())
    hm = _head_mean_matrix()

    h, w_up_b, w_down_b = _mix_attn(
        rel_bias_table, sinks, x2, norm_mix_g, w_in[0], conv_w, q_norm_g, k_norm_g,
        out_norm_conv_g, hm, bkt, out_norm_attn_g, w_out[0],
        w_up[0], w_down[0], tiles_per_seq)
    out = _ffn(h, norm_ffn_g, w_up_b, ffn_conv_w, ffn_conv_b, w_down_b, tiles_per_seq)
    return out.reshape(bsz, seq, d)
```

```python
import functools
import math

import jax
import jax.numpy as jnp
import numpy as np
from jax import lax
from jax.experimental import pallas as pl
from jax.experimental.pallas import tpu as pltpu

D_MODEL = 1024
CONV_WIDTH = 512
HEAD_DIM = 64
N_HEADS = 8
N_KV_HEADS = 2
GQA_GROUP = N_HEADS // N_KV_HEADS
ATTN_WIDTH = N_HEADS * HEAD_DIM
KV_WIDTH = N_KV_HEADS * HEAD_DIM
WINDOW = 128
BLK = 128
NUM_BUCKETS = 32
MAX_DISTANCE = 128
MAX_EXACT = NUM_BUCKETS // 2
D_FF = 2816
EPS = 1e-6
NEG_INF = -1e30
LOG2E = math.log2(math.e)
IN_WIDTH = 3 * CONV_WIDTH + ATTN_WIDTH + 2 * KV_WIDTH

LANES = 128
SUBLANES = 8
MXU_TILE = 256
PAIR = 2 * HEAD_DIM
assert PAIR == LANES and KV_WIDTH == LANES and GQA_GROUP == 4
KX_WIDTH = 2 * N_KV_HEADS * LANES
VX_WIDTH = 2 * KX_WIDTH

ROW_TILE = 1024
FF_CHUNK = MXU_TILE
N_FF_CHUNKS = D_FF // FF_CHUNK
HALO = 8
LOOKAHEAD = 1
VMEM_LIMIT_BYTES = 60 * 1024 * 1024
FFN_TAP_ROW, FFN_BIAS_ROW, MIX_TAP_ROW = 0, 3, 4

_BF16 = jnp.bfloat16
_F32 = jnp.float32


def _dot(a, b):
    return jnp.dot(a, b, preferred_element_type=_F32)


def _dot_nt(a, b):
    return lax.dot_general(a, b, (((1,), (1,)), ((), ())), preferred_element_type=_F32)


def _rms(xf, g):
    return xf * lax.rsqrt(jnp.mean(xf * xf, axis=-1, keepdims=True) + EPS) * g


def _bucket_map():
    q = np.arange(BLK, dtype=np.int32)[:, None]
    j = np.arange(2 * BLK, dtype=np.int32)[None, :]
    d = q + BLK - j
    n = np.maximum(d, 0)
    nf = np.maximum(n, 1).astype(np.float32)
    large = MAX_EXACT + (np.log(nf / MAX_EXACT) / math.log(MAX_DISTANCE / MAX_EXACT)
                         * (NUM_BUCKETS - MAX_EXACT)).astype(np.int32)
    large = np.minimum(large, NUM_BUCKETS - 1)
    bucket = np.where(n < MAX_EXACT, n, large).astype(np.int32)
    within = (d >= 0) & (d < WINDOW)
    return np.where(within, bucket, -1).astype(np.int32)


def _const_spec(shape):
    return pl.BlockSpec(shape, lambda i: (0,) * len(shape), pipeline_mode=pl.Buffered(1))


def _store_interleaved(slab_ref, val):
    ts = val.shape[0]
    seg = ts // SUBLANES
    for l in range(D_MODEL // LANES):
        for s in range(SUBLANES):
            slab_ref[l, pl.ds(s, seg, stride=SUBLANES), :] = val[s * seg:(s + 1) * seg, l * LANES:(l + 1) * LANES]


def _load_deinterleaved(slab_ref, out_ref):
    ts = out_ref.shape[0]
    seg = ts // SUBLANES
    for l in range(D_MODEL // LANES):
        for s in range(SUBLANES):
            out_ref[s * seg:(s + 1) * seg, l * LANES:(l + 1) * LANES] = slab_ref[l, pl.ds(s, seg, stride=SUBLANES), :]


def _lo_hi_layout(a, lo):
    ar = pltpu.roll(a, HEAD_DIM, axis=1)
    zero = jnp.zeros_like(a)
    return [jnp.where(lo, a, zero), jnp.where(lo, zero, ar), jnp.where(lo, ar, zero), jnp.where(lo, zero, a)]


def _mix_attn_kernel(tiles_per_seq, tab_ref, sink_ref, x_ref, g_ref, w_f32, cw_ref, gq_head, gk_head, gc_ref,
                     bkt_ref, ga_ref, wo_f32, wu_f32, wd_f32,
                     o_ref, wu_bf16, wd_bf16,
                     bias_ref, hm_ref, w_ref, wo_ref, gq_ref, gk_ref, q_ref, kx_ref, vx_ref, kb0_ref, vb0_ref, gatec_ref, gateb_ref, zbuf,
                     yan_ref, p_ref, st_ref):
    ts = x_ref.shape[0]
    nblk = ts // BLK
    i = pl.program_id(0)
    seq_start = (i % tiles_per_seq == 0)

    wu_bf16[...] = wu_f32[...].astype(_BF16)
    wd_bf16[...] = wd_f32[...].astype(_BF16)

    @pl.when(i == 0)
    def _():
        bkt = bkt_ref[...]
        col = lax.broadcasted_iota(jnp.int32, (BLK, 2 * BLK), 1)

        def head_body(h, carry):
            b = jnp.full((BLK, 2 * BLK), NEG_INF, _F32)
            for t in range(NUM_BUCKETS):
                b = jnp.where(bkt == t, tab_ref[t, h] * LOG2E, b)
            bias_ref[0, h] = b
            bias_ref[1, h] = jnp.where(col >= BLK, b, NEG_INF)
            return carry

        lax.fori_loop(0, N_HEADS, head_body, 0)
        row_head = lax.shift_right_logical(lax.broadcasted_iota(jnp.int32, (ATTN_WIDTH, ATTN_WIDTH), 0), 6)
        col_head = lax.shift_right_logical(lax.broadcasted_iota(jnp.int32, (ATTN_WIDTH, ATTN_WIDTH), 1), 6)
        hm_ref[...] = jnp.where(row_head == col_head, 1.0 / HEAD_DIM, 0.0).astype(_BF16)
        w_ref[...] = w_f32[...].astype(_BF16)
        wo_ref[...] = wo_f32[...].astype(_BF16)
        for h in range(N_HEADS):
            gq_ref[:, h * HEAD_DIM:(h + 1) * HEAD_DIM] = gq_head[...]
        for h in range(N_KV_HEADS):
            gk_ref[:, h * HEAD_DIM:(h + 1) * HEAD_DIM] = gk_head[...]
        kx_ref[ts - BLK:ts, :] = jnp.zeros((BLK, KX_WIDTH), _BF16)
        vx_ref[ts - BLK:ts, :] = jnp.zeros((BLK, VX_WIDTH), _BF16)

    @pl.when(seq_start)
    def _():
        zbuf[0:HALO, :] = jnp.zeros((HALO, CONV_WIDTH), _F32)

    kb0_ref[0:BLK, :] = kx_ref[ts - BLK:ts, :]
    vb0_ref[0:BLK, :] = vx_ref[ts - BLK:ts, :]

    u = _rms(x_ref[...], g_ref[...]).astype(_BF16)
    c0 = 3 * CONV_WIDTH
    c1 = c0 + ATTN_WIDTH
    q = _dot(u, w_ref[:, c0:c0 + ATTN_WIDTH])
    kv = _dot(u, w_ref[:, c1:c1 + 2 * KV_WIDTH])
    k = kv[:, 0:KV_WIDTH]
    msq = _dot((q * q).astype(_BF16), hm_ref[...])
    msk = _dot((k * k).astype(_BF16), hm_ref[0:KV_WIDTH, 0:KV_WIDTH])

    q_ref[...] = (q * lax.rsqrt(msq + EPS) * gq_ref[...] * (HEAD_DIM ** -0.5 * LOG2E)).astype(_BF16)
    kn = k * lax.rsqrt(msk + EPS) * gk_ref[...]
    lo = lax.broadcasted_iota(jnp.int32, (ts, LANES), 1) < HEAD_DIM
    kx_ref[...] = jnp.concatenate(_lo_hi_layout(kn, lo), axis=1).astype(_BF16)
    ones = [jnp.where(lo, 1.0, 0.0).astype(_F32), jnp.where(lo, 0.0, 1.0).astype(_F32)]
    v_pieces = _lo_hi_layout(kv[:, KV_WIDTH:2 * KV_WIDTH], lo)
    vx_ref[...] = jnp.concatenate([piece for s, v in enumerate(v_pieces) for piece in (v, ones[s % 2])],
                                  axis=1).astype(_BF16)
    kb0_ref[BLK:2 * BLK, :] = kx_ref[0:BLK, :]
    vb0_ref[BLK:2 * BLK, :] = vx_ref[0:BLK, :]

    lane_lo = lax.broadcasted_iota(jnp.int32, (BLK, LANES), 1) < HEAD_DIM

    def band(tile_ref, band0_ref, j, cols):
        if j == 0:
            return band0_ref[:, cols]
        return tile_ref[(j - 1) * BLK:(j + 1) * BLK, cols]

    def scores_stage(j):
        r0 = j * BLK
        slot = j % (LOOKAHEAD + 1)
        first = seq_start.astype(jnp.int32) if j == 0 else 0
        for g in range(N_KV_HEADS):
            qq = q_ref[pl.ds(r0, BLK), g * 2 * LANES:(g + 1) * 2 * LANES]
            lhs = jnp.concatenate([qq[:, 0:LANES], qq[:, LANES:2 * LANES]], axis=0)
            sink_terms = [[None, None], [None, None]]
            for par in range(2):
                kb = band(kx_ref, kb0_ref, j, slice((2 * g + par) * LANES, (2 * g + par + 1) * LANES))
                logits = _dot_nt(lhs, kb)
                for pair in range(2):
                    h = g * GQA_GROUP + 2 * pair + par
                    lg = logits[pair * BLK:(pair + 1) * BLK, :] + bias_ref[first, h]
                    sink = sink_ref[0, h] * LOG2E
                    m = jnp.maximum(jnp.max(lg, axis=-1, keepdims=True), sink)
                    p_ref[slot, 2 * g + par, pair * BLK:(pair + 1) * BLK, :] = jnp.exp2(lg - m).astype(_BF16)
                    sink_terms[pair][par] = jnp.exp2(sink - m)
            for pair in range(2):
                st_ref[slot, 2 * g + pair] = jnp.where(lane_lo, sink_terms[pair][0], sink_terms[pair][1])

    def values_stage(j):
        r0 = j * BLK
        slot = j % (LOOKAHEAD + 1)
        pairs = []
        for g in range(N_KV_HEADS):
            out = None
            for par in range(2):
                c = (4 * g + 2 * par) * LANES
                o = _dot(p_ref[slot, 2 * g + par],
                         band(vx_ref, vb0_ref, j, slice(c, c + 2 * LANES)))
                out = o if out is None else out + o
            for pair in range(2):
                blk = out[pair * BLK:(pair + 1) * BLK, :]
                den = blk[:, LANES:2 * LANES] + st_ref[slot, 2 * g + pair]
                pairs.append(blk[:, 0:LANES] / den)
        ya = jnp.concatenate(pairs, axis=1)
        yan_ref[r0:r0 + BLK, :] = _rms(ya, ga_ref[...]).astype(_BF16)

    def conv_piece(piece):
        half = piece % 2
        cols = slice(half * MXU_TILE, (half + 1) * MXU_TILE)
        w_cols = lambda base: w_ref[:, base + half * MXU_TILE:base + (half + 1) * MXU_TILE]
        if piece < 2:
            gatec_ref[:, cols] = _dot(u, w_cols(CONV_WIDTH))
        elif piece < 4:
            zbuf[HALO:HALO + ts, cols] = gatec_ref[:, cols] * _dot(u, w_cols(2 * CONV_WIDTH))
        else:
            gateb_ref[:, cols] = _dot(u, w_cols(0))

    piece_after_block = {0: 1, 1: 2, 3: 3, 5: 4, 6: 5}
    conv_piece(0)
    for j in range(LOOKAHEAD):
        scores_stage(j)
    for j in range(nblk):
        if j + LOOKAHEAD < nblk:
            scores_stage(j + LOOKAHEAD)
        if j in piece_after_block:
            conv_piece(piece_after_block[j])
        values_stage(j)

    taps = [cw_ref[MIX_TAP_ROW + k:MIX_TAP_ROW + k + 1, :] for k in range(3)]
    conv = (taps[0] * zbuf[HALO - 2:HALO - 2 + ts, :] + taps[1] * zbuf[HALO - 1:HALO - 1 + ts, :]
            + taps[2] * zbuf[HALO:HALO + ts, :])
    yc = _rms(gateb_ref[...] * conv, gc_ref[...]).astype(_BF16)
    zbuf[0:HALO, :] = zbuf[ts:ts + HALO, :]

    h = (x_ref[...] + _dot(yan_ref[...], wo_ref[CONV_WIDTH:CONV_WIDTH + ATTN_WIDTH, :])
         + _dot(yc, wo_ref[0:CONV_WIDTH, :]))
    _store_interleaved(o_ref, h)


def _mix_attn(tab, sinks, x2, g_mix, w_in, conv_pack, gq, gk, g_conv, bkt, g_attn, w_out, w_up, w_down,
              tiles_per_seq):
    n = x2.shape[0]
    ts = ROW_TILE
    steps = n // ts
    assert ts // BLK >= 8 and CONV_WIDTH == 2 * MXU_TILE
    smem = pl.BlockSpec(memory_space=pltpu.SMEM)

    def slice_spec(w):
        rows, cols = w.shape
        assert rows % (steps * 2 * SUBLANES) == 0
        return pl.BlockSpec((rows // steps, cols), lambda i: (i, 0))

    cast_weights = (w_up, w_down)
    return pl.pallas_call(
        functools.partial(_mix_attn_kernel, tiles_per_seq),
        grid=(steps,),
        in_specs=[
            smem,
            smem,
            pl.BlockSpec((ts, D_MODEL), lambda i: (i, 0)),
            _const_spec((1, D_MODEL)),
            _const_spec((D_MODEL, IN_WIDTH)),
            _const_spec((SUBLANES, CONV_WIDTH)),
            _const_spec((1, HEAD_DIM)),
            _const_spec((1, HEAD_DIM)),
            _const_spec((1, CONV_WIDTH)),
            _const_spec((BLK, 2 * BLK)),
            _const_spec((1, ATTN_WIDTH)),
            _const_spec((CONV_WIDTH + ATTN_WIDTH, D_MODEL)),
        ] + [slice_spec(w) for w in cast_weights],
        out_specs=[pl.BlockSpec((D_MODEL // LANES, ts, LANES), lambda i: (0, i, 0))]
        + [slice_spec(w) for w in cast_weights],
        out_shape=[jax.ShapeDtypeStruct((D_MODEL // LANES, n, LANES), _F32)]
        + [jax.ShapeDtypeStruct(w.shape, _BF16) for w in cast_weights],
        scratch_shapes=[
            pltpu.VMEM((2, N_HEADS, BLK, 2 * BLK), _F32),
            pltpu.VMEM((ATTN_WIDTH, ATTN_WIDTH), _BF16),
            pltpu.VMEM((D_MODEL, IN_WIDTH), _BF16),
            pltpu.VMEM((CONV_WIDTH + ATTN_WIDTH, D_MODEL), _BF16),
            pltpu.VMEM((1, ATTN_WIDTH), _F32),
            pltpu.VMEM((1, KV_WIDTH), _F32),
            pltpu.VMEM((ts, ATTN_WIDTH), _BF16),
            pltpu.VMEM((ts, KX_WIDTH), _BF16),
            pltpu.VMEM((ts, VX_WIDTH), _BF16),
            pltpu.VMEM((2 * BLK, KX_WIDTH), _BF16),
            pltpu.VMEM((2 * BLK, VX_WIDTH), _BF16),
            pltpu.VMEM((ts, CONV_WIDTH), _F32),
            pltpu.VMEM((ts, CONV_WIDTH), _F32),
            pltpu.VMEM((ts + HALO, CONV_WIDTH), _F32),
            pltpu.VMEM((ts, ATTN_WIDTH), _BF16),
            pltpu.VMEM((LOOKAHEAD + 1, 2 * N_KV_HEADS, 2 * BLK, 2 * BLK), _BF16),
            pltpu.VMEM((LOOKAHEAD + 1, 2 * N_KV_HEADS, BLK, LANES), _F32),
        ],
        compiler_params=pltpu.CompilerParams(
            dimension_semantics=("arbitrary",), vmem_limit_bytes=VMEM_LIMIT_BYTES),
        name="mix_attn",
    )(tab, sinks, x2, g_mix, w_in, conv_pack, gq, gk, g_conv, bkt, g_attn, w_out, *cast_weights)


def _ffn_kernel(tiles_per_seq, h_ref, g_ref, wu_ref, cw_ref, wd_ref, o_ref,
                u_ref, ubuf, carry_ref, act_ref, res_ref):
    ts = o_ref.shape[0]
    i = pl.program_id(0)
    g2 = 2 * SUBLANES

    @pl.when(i % tiles_per_seq == 0)
    def _():
        carry_ref[...] = jnp.zeros(carry_ref.shape, _F32)

    h = jnp.concatenate([h_ref[l] for l in range(D_MODEL // LANES)], axis=1)
    u_ref[...] = _rms(h, g_ref[...]).astype(_BF16)
    first_sublane = lax.broadcasted_iota(jnp.int32, (SUBLANES, FF_CHUNK), 0) == 0

    def conv_half(c, part):
        slot = 2 * c + part
        buf = ubuf.at[slot % ubuf.shape[0]]
        cols = slice(part * D_FF + c * FF_CHUNK, part * D_FF + (c + 1) * FF_CHUNK)
        buf[g2:g2 + ts, :] = _dot(u_ref[...], wu_ref[:, cols])
        for k in range(2):
            cur = pltpu.roll(buf[ts + k * SUBLANES:ts + (k + 1) * SUBLANES, :], 1, axis=0)
            prev = pltpu.roll(carry_ref[slot, k * SUBLANES:(k + 1) * SUBLANES, :], 1, axis=0)
            buf[k * SUBLANES:(k + 1) * SUBLANES, :] = jnp.where(first_sublane, prev, cur)
        taps = [cw_ref[FFN_TAP_ROW + k:FFN_TAP_ROW + k + 1, cols] for k in range(3)]
        a = (taps[0] * buf[0:ts, :] + taps[1] * buf[SUBLANES:SUBLANES + ts, :] + taps[2] * buf[g2:g2 + ts, :]
             + cw_ref[FFN_BIAS_ROW:FFN_BIAS_ROW + 1, cols])
        carry_ref[slot] = buf[ts:ts + g2, :]
        return a

    for c in range(N_FF_CHUNKS):
        gate = conv_half(c, 0)
        val = conv_half(c, 1)
        act_ref[:, c * FF_CHUNK:(c + 1) * FF_CHUNK] = (gate / (1.0 + jnp.exp(-gate)) * val).astype(_BF16)

    res = h + _dot(act_ref[...], wd_ref[...])
    for l in range(D_MODEL // LANES):
        res_ref[l] = res[:, l * LANES:(l + 1) * LANES]
    _load_deinterleaved(res_ref, o_ref)


def _ffn(h, g_ffn, w_up, conv_pack, w_down, tiles_per_seq):
    n = h.shape[1]
    ts = ROW_TILE
    g2 = 2 * SUBLANES
    return pl.pallas_call(
        functools.partial(_ffn_kernel, tiles_per_seq),
        grid=(n // ts,),
        in_specs=[
            pl.BlockSpec((D_MODEL // LANES, ts, LANES), lambda i: (0, i, 0)),
            _const_spec((1, D_MODEL)),
            _const_spec((D_MODEL, 2 * D_FF)),
            _const_spec((SUBLANES, 2 * D_FF)),
            _const_spec((D_FF, D_MODEL)),
        ],
        out_specs=pl.BlockSpec((ts, D_MODEL), lambda i: (i, 0)),
        out_shape=jax.ShapeDtypeStruct((n, D_MODEL), _F32),
        scratch_shapes=[
            pltpu.VMEM((ts, D_MODEL), _BF16),
            pltpu.VMEM((4, ts + g2, FF_CHUNK), _F32),
            pltpu.VMEM((2 * N_FF_CHUNKS, g2, FF_CHUNK), _F32),
            pltpu.VMEM((ts, D_FF), _BF16),
            pltpu.VMEM((D_MODEL // LANES, ts, LANES), _F32),
        ],
        compiler_params=pltpu.CompilerParams(
            dimension_semantics=("arbitrary",), vmem_limit_bytes=VMEM_LIMIT_BYTES),
        name="ffn",
    )(h, g_ffn, w_up, conv_pack, w_down)


def kernel(x, norm_mix_g, w_in, conv_w, q_norm_g, k_norm_g, rel_bias_table, sinks, out_norm_conv_g,
           out_norm_attn_g, w_out, norm_ffn_g, w_up, ffn_conv_w, ffn_conv_b, w_down):
    bsz, seq, d = x.shape
    assert d == D_MODEL and seq % ROW_TILE == 0 and norm_mix_g.shape[0] == 1
    tiles_per_seq = seq // ROW_TILE
    n = bsz * seq
    x2 = x.reshape(n, d)
    bkt = jnp.asarray(_bucket_map())
    conv_pack = jnp.concatenate([
        ffn_conv_w[0], ffn_conv_b, jnp.pad(conv_w[0], ((0, 0), (0, 2 * D_FF - CONV_WIDTH))),
        jnp.zeros((1, 2 * D_FF), _F32)], axis=0)

    h, w_up_b, w_down_b = _mix_attn(
        rel_bias_table, sinks, x2, norm_mix_g, w_in[0], conv_pack, q_norm_g, k_norm_g,
        out_norm_conv_g, bkt, out_norm_attn_g, w_out[0],
        w_up[0], w_down[0], tiles_per_seq)
    out = _ffn(h, norm_ffn_g, w_up_b, conv_pack, w_down_b, tiles_per_seq)
    return out.reshape(bsz, seq, d)
```

```python
import functools
import math

import jax
import jax.numpy as jnp
import numpy as np
from jax import lax
from jax.experimental import pallas as pl
from jax.experimental.pallas import tpu as pltpu

D_MODEL = 1024
CONV_WIDTH = 512
HEAD_DIM = 64
N_HEADS = 8
N_KV_HEADS = 2
GQA_GROUP = N_HEADS // N_KV_HEADS
ATTN_WIDTH = N_HEADS * HEAD_DIM
KV_WIDTH = N_KV_HEADS * HEAD_DIM
WINDOW = 128
BLK = 128
NUM_BUCKETS = 32
MAX_DISTANCE = 128
MAX_EXACT = NUM_BUCKETS // 2
D_FF = 2816
EPS = 1e-6
NEG_INF = -1e30
LOG2E = math.log2(math.e)
IN_WIDTH = 3 * CONV_WIDTH + ATTN_WIDTH + 2 * KV_WIDTH

LANES = 128
SUBLANES = 8
MXU_TILE = 256
PAIR = 2 * HEAD_DIM
assert PAIR == LANES and KV_WIDTH == LANES and GQA_GROUP == 4
KX_WIDTH = 2 * N_KV_HEADS * LANES
VX_WIDTH = 2 * KX_WIDTH

ROW_TILE = 1024
FF_CHUNK = MXU_TILE
N_FF_CHUNKS = D_FF // FF_CHUNK
HALO = 8
LOOKAHEAD = 1
VMEM_LIMIT_BYTES = 60 * 1024 * 1024

_BF16 = jnp.bfloat16
_F32 = jnp.float32


def _dot(a, b):
    return jnp.dot(a, b, preferred_element_type=_F32)


def _dot_nt(a, b):
    return lax.dot_general(a, b, (((1,), (1,)), ((), ())), preferred_element_type=_F32)


def _rms(xf, g):
    return xf * lax.rsqrt(jnp.mean(xf * xf, axis=-1, keepdims=True) + EPS) * g


def _bucket_map():
    assert WINDOW == BLK
    r = np.arange(BLK, dtype=np.int32)[:, None]
    c = np.arange(BLK, dtype=np.int32)[None, :]
    n = np.where(c <= r, r - c, r - c + BLK)
    nf = np.maximum(n, 1).astype(np.float32)
    large = MAX_EXACT + (np.log(nf / MAX_EXACT) / math.log(MAX_DISTANCE / MAX_EXACT)
                         * (NUM_BUCKETS - MAX_EXACT)).astype(np.int32)
    large = np.minimum(large, NUM_BUCKETS - 1)
    return np.where(n < MAX_EXACT, n, large).astype(np.int32)


def _head_mean_matrix():
    i = np.arange(ATTN_WIDTH)
    m = (i[:, None] // HEAD_DIM == i[None, :] // HEAD_DIM).astype(np.float32) / HEAD_DIM
    return jnp.asarray(m, dtype=_BF16)


def _const_spec(shape):
    return pl.BlockSpec(shape, lambda i: (0,) * len(shape), pipeline_mode=pl.Buffered(1))


def _store_interleaved(slab_ref, val):
    ts = val.shape[0]
    seg = ts // SUBLANES
    for l in range(D_MODEL // LANES):
        for s in range(SUBLANES):
            slab_ref[l, pl.ds(s, seg, stride=SUBLANES), :] = val[s * seg:(s + 1) * seg, l * LANES:(l + 1) * LANES]


def _load_deinterleaved(slab_ref, out_ref):
    ts = out_ref.shape[0]
    seg = ts // SUBLANES
    for l in range(D_MODEL // LANES):
        for s in range(SUBLANES):
            out_ref[s * seg:(s + 1) * seg, l * LANES:(l + 1) * LANES] = slab_ref[l, pl.ds(s, seg, stride=SUBLANES), :]


def _lo_hi_layout(a, lo):
    ar = pltpu.roll(a, HEAD_DIM, axis=1)
    zero = jnp.zeros_like(a)
    return [jnp.where(lo, a, zero), jnp.where(lo, zero, ar), jnp.where(lo, ar, zero), jnp.where(lo, zero, a)]


def _mix_attn_kernel(tiles_per_seq, tab_ref, sink_ref, x_ref, g_ref, w_f32, cw_ref, gq_head, gk_head, gc_ref,
                     hm_ref, bkt_ref, ga_ref, wo_f32, wu_f32, wd_f32,
                     o_ref, wu_bf16, wd_bf16,
                     bias_ref, w_ref, wo_ref, gq_ref, gk_ref, q_ref, kx_ref, vx_ref, kb0_ref, vb0_ref, gatec_ref, gateb_ref, zbuf,
                     yan_ref, p_ref, st_ref):
    ts = x_ref.shape[0]
    nblk = ts // BLK
    i = pl.program_id(0)
    seq_start = (i % tiles_per_seq == 0)

    wu_bf16[...] = wu_f32[...].astype(_BF16)
    wd_bf16[...] = wd_f32[...].astype(_BF16)

    @pl.when(i == 0)
    def _():
        bkt = bkt_ref[...]
        in_cur = (lax.broadcasted_iota(jnp.int32, (BLK, BLK), 1)
                  <= lax.broadcasted_iota(jnp.int32, (BLK, BLK), 0))

        def head_body(h, carry):
            b = jnp.zeros((BLK, BLK), _F32)
            for t in range(NUM_BUCKETS):
                b = jnp.where(bkt == t, tab_ref[t, h] * LOG2E, b)
            bias_ref[0, h] = b
            bias_ref[1, h] = jnp.where(in_cur, b, NEG_INF)
            return carry

        lax.fori_loop(0, N_HEADS, head_body, 0)
        w_ref[...] = w_f32[...].astype(_BF16)
        wo_ref[...] = wo_f32[...].astype(_BF16)
        for h in range(N_HEADS):
            gq_ref[:, h * HEAD_DIM:(h + 1) * HEAD_DIM] = gq_head[...]
        for h in range(N_KV_HEADS):
            gk_ref[:, h * HEAD_DIM:(h + 1) * HEAD_DIM] = gk_head[...]
        kx_ref[ts - BLK:ts, :] = jnp.zeros((BLK, KX_WIDTH), _BF16)
        vx_ref[ts - BLK:ts, :] = jnp.zeros((BLK, VX_WIDTH), _BF16)

    @pl.when(seq_start)
    def _():
        zbuf[0:HALO, :] = jnp.zeros((HALO, CONV_WIDTH), _F32)

    kb0_ref[0:BLK, :] = kx_ref[ts - BLK:ts, :]
    vb0_ref[0:BLK, :] = vx_ref[ts - BLK:ts, :]

    u = _rms(x_ref[...], g_ref[...]).astype(_BF16)
    c0 = 3 * CONV_WIDTH
    c1 = c0 + ATTN_WIDTH
    q = _dot(u, w_ref[:, c0:c0 + ATTN_WIDTH])
    kv = _dot(u, w_ref[:, c1:c1 + 2 * KV_WIDTH])
    k = kv[:, 0:KV_WIDTH]
    msq = _dot((q * q).astype(_BF16), hm_ref[...])
    msk = _dot((k * k).astype(_BF16), hm_ref[0:KV_WIDTH, 0:KV_WIDTH])

    q_ref[...] = (q * lax.rsqrt(msq + EPS) * gq_ref[...] * (HEAD_DIM ** -0.5 * LOG2E)).astype(_BF16)
    kn = k * lax.rsqrt(msk + EPS) * gk_ref[...]
    lo = lax.broadcasted_iota(jnp.int32, (ts, LANES), 1) < HEAD_DIM
    kx_ref[...] = jnp.concatenate(_lo_hi_layout(kn, lo), axis=1).astype(_BF16)
    ones = [jnp.where(lo, 1.0, 0.0).astype(_F32), jnp.where(lo, 0.0, 1.0).astype(_F32)]
    v_pieces = _lo_hi_layout(kv[:, KV_WIDTH:2 * KV_WIDTH], lo)
    vx_ref[...] = jnp.concatenate([piece for s, v in enumerate(v_pieces) for piece in (v, ones[s % 2])],
                                  axis=1).astype(_BF16)
    kb0_ref[BLK:2 * BLK, :] = kx_ref[0:BLK, :]
    vb0_ref[BLK:2 * BLK, :] = vx_ref[0:BLK, :]

    lane_lo = lax.broadcasted_iota(jnp.int32, (BLK, LANES), 1) < HEAD_DIM
    in_cur = (lax.broadcasted_iota(jnp.int32, (BLK, BLK), 1) <= lax.broadcasted_iota(jnp.int32, (BLK, BLK), 0))

    def band(tile_ref, band0_ref, j, cols):
        if j == 0:
            return band0_ref[:, cols]
        return tile_ref[(j - 1) * BLK:(j + 1) * BLK, cols]

    def scores_stage(j):
        r0 = j * BLK
        slot = j % (LOOKAHEAD + 1)
        first = seq_start.astype(jnp.int32) if j == 0 else 0
        for g in range(N_KV_HEADS):
            qq = q_ref[pl.ds(r0, BLK), g * 2 * LANES:(g + 1) * 2 * LANES]
            lhs = jnp.concatenate([qq[:, 0:LANES], qq[:, LANES:2 * LANES]], axis=0)
            sink_terms = [[None, None], [None, None]]
            for par in range(2):
                kb = band(kx_ref, kb0_ref, j, slice((2 * g + par) * LANES, (2 * g + par + 1) * LANES))
                logits = _dot_nt(lhs, kb)
                for pair in range(2):
                    h = g * GQA_GROUP + 2 * pair + par
                    rows = slice(pair * BLK, (pair + 1) * BLK)
                    lg = jnp.where(in_cur, logits[rows, BLK:2 * BLK], logits[rows, 0:BLK]) + bias_ref[first, h]
                    sink = sink_ref[0, h] * LOG2E
                    m = jnp.maximum(jnp.max(lg, axis=-1, keepdims=True), sink)
                    p = jnp.exp2(lg - m)
                    zero = jnp.zeros_like(p)
                    p_ref[slot, 2 * g + par, rows, 0:BLK] = jnp.where(in_cur, zero, p).astype(_BF16)
                    p_ref[slot, 2 * g + par, rows, BLK:2 * BLK] = jnp.where(in_cur, p, zero).astype(_BF16)
                    sink_terms[pair][par] = jnp.exp2(sink - m)
            for pair in range(2):
                st_ref[slot, 2 * g + pair] = jnp.where(lane_lo, sink_terms[pair][0], sink_terms[pair][1])

    def values_stage(j):
        r0 = j * BLK
        slot = j % (LOOKAHEAD + 1)
        pairs = []
        for g in range(N_KV_HEADS):
            out = None
            for par in range(2):
                c = (4 * g + 2 * par) * LANES
                o = _dot(p_ref[slot, 2 * g + par],
                         band(vx_ref, vb0_ref, j, slice(c, c + 2 * LANES)))
                out = o if out is None else out + o
            for pair in range(2):
                blk = out[pair * BLK:(pair + 1) * BLK, :]
                den = blk[:, LANES:2 * LANES] + st_ref[slot, 2 * g + pair]
                pairs.append(blk[:, 0:LANES] / den)
        ya = jnp.concatenate(pairs, axis=1)
        yan_ref[r0:r0 + BLK, :] = _rms(ya, ga_ref[...]).astype(_BF16)

    def conv_piece(piece):
        half = piece % 2
        cols = slice(half * MXU_TILE, (half + 1) * MXU_TILE)
        w_cols = lambda base: w_ref[:, base + half * MXU_TILE:base + (half + 1) * MXU_TILE]
        if piece < 2:
            gatec_ref[:, cols] = _dot(u, w_cols(CONV_WIDTH))
        elif piece < 4:
            zbuf[HALO:HALO + ts, cols] = gatec_ref[:, cols] * _dot(u, w_cols(2 * CONV_WIDTH))
        else:
            gateb_ref[:, cols] = _dot(u, w_cols(0))

    piece_after_block = {0: 1, 1: 2, 3: 3, 5: 4, 6: 5}
    conv_piece(0)
    for j in range(LOOKAHEAD):
        scores_stage(j)
    for j in range(nblk):
        if j + LOOKAHEAD < nblk:
            scores_stage(j + LOOKAHEAD)
        if j in piece_after_block:
            conv_piece(piece_after_block[j])
        values_stage(j)

    conv = (cw_ref[0, 0:1, :] * zbuf[HALO - 2:HALO - 2 + ts, :]
            + cw_ref[0, 1:2, :] * zbuf[HALO - 1:HALO - 1 + ts, :]
            + cw_ref[0, 2:3, :] * zbuf[HALO:HALO + ts, :])
    yc = _rms(gateb_ref[...] * conv, gc_ref[...]).astype(_BF16)
    zbuf[0:HALO, :] = zbuf[ts:ts + HALO, :]

    h = (x_ref[...] + _dot(yan_ref[...], wo_ref[CONV_WIDTH:CONV_WIDTH + ATTN_WIDTH, :])
         + _dot(yc, wo_ref[0:CONV_WIDTH, :]))
    _store_interleaved(o_ref, h)


def _mix_attn(tab, sinks, x2, g_mix, w_in, conv_w, gq, gk, g_conv, hm, bkt, g_attn, w_out, w_up, w_down,
              tiles_per_seq):
    n = x2.shape[0]
    ts = ROW_TILE
    steps = n // ts
    assert ts // BLK >= 8 and CONV_WIDTH == 2 * MXU_TILE
    smem = pl.BlockSpec(memory_space=pltpu.SMEM)

    def slice_spec(w):
        rows, cols = w.shape
        assert rows % (steps * 2 * SUBLANES) == 0
        return pl.BlockSpec((rows // steps, cols), lambda i: (i, 0))

    cast_weights = (w_up, w_down)
    return pl.pallas_call(
        functools.partial(_mix_attn_kernel, tiles_per_seq),
        grid=(steps,),
        in_specs=[
            smem,
            smem,
            pl.BlockSpec((ts, D_MODEL), lambda i: (i, 0)),
            _const_spec((1, D_MODEL)),
            _const_spec((D_MODEL, IN_WIDTH)),
            _const_spec((1, 3, CONV_WIDTH)),
            _const_spec((1, HEAD_DIM)),
            _const_spec((1, HEAD_DIM)),
            _const_spec((1, CONV_WIDTH)),
            _const_spec((ATTN_WIDTH, ATTN_WIDTH)),
            _const_spec((BLK, BLK)),
            _const_spec((1, ATTN_WIDTH)),
            _const_spec((CONV_WIDTH + ATTN_WIDTH, D_MODEL)),
        ] + [slice_spec(w) for w in cast_weights],
        out_specs=[pl.BlockSpec((D_MODEL // LANES, ts, LANES), lambda i: (0, i, 0))]
        + [slice_spec(w) for w in cast_weights],
        out_shape=[jax.ShapeDtypeStruct((D_MODEL // LANES, n, LANES), _F32)]
        + [jax.ShapeDtypeStruct(w.shape, _BF16) for w in cast_weights],
        scratch_shapes=[
            pltpu.VMEM((2, N_HEADS, BLK, BLK), _F32),
            pltpu.VMEM((D_MODEL, IN_WIDTH), _BF16),
            pltpu.VMEM((CONV_WIDTH + ATTN_WIDTH, D_MODEL), _BF16),
            pltpu.VMEM((1, ATTN_WIDTH), _F32),
            pltpu.VMEM((1, KV_WIDTH), _F32),
            pltpu.VMEM((ts, ATTN_WIDTH), _BF16),
            pltpu.VMEM((ts, KX_WIDTH), _BF16),
            pltpu.VMEM((ts, VX_WIDTH), _BF16),
            pltpu.VMEM((2 * BLK, KX_WIDTH), _BF16),
            pltpu.VMEM((2 * BLK, VX_WIDTH), _BF16),
            pltpu.VMEM((ts, CONV_WIDTH), _F32),
            pltpu.VMEM((ts, CONV_WIDTH), _F32),
            pltpu.VMEM((ts + HALO, CONV_WIDTH), _F32),
            pltpu.VMEM((ts, ATTN_WIDTH), _BF16),
            pltpu.VMEM((LOOKAHEAD + 1, 2 * N_KV_HEADS, 2 * BLK, 2 * BLK), _BF16),
            pltpu.VMEM((LOOKAHEAD + 1, 2 * N_KV_HEADS, BLK, LANES), _F32),
        ],
        compiler_params=pltpu.CompilerParams(
            dimension_semantics=("arbitrary",), vmem_limit_bytes=VMEM_LIMIT_BYTES),
        name="mix_attn",
    )(tab, sinks, x2, g_mix, w_in, conv_w, gq, gk, g_conv, hm, bkt, g_attn, w_out, *cast_weights)


def _ffn_kernel(tiles_per_seq, h_ref, g_ref, wu_ref, cw_ref, cb_ref, wd_ref, o_ref,
                u_ref, ubuf, carry_ref, act_ref, res_ref):
    ts = o_ref.shape[0]
    i = pl.program_id(0)
    g2 = 2 * SUBLANES

    @pl.when(i % tiles_per_seq == 0)
    def _():
        carry_ref[...] = jnp.zeros(carry_ref.shape, _F32)

    h = jnp.concatenate([h_ref[l] for l in range(D_MODEL // LANES)], axis=1)
    u_ref[...] = _rms(h, g_ref[...]).astype(_BF16)
    first_sublane = lax.broadcasted_iota(jnp.int32, (SUBLANES, FF_CHUNK), 0) == 0

    def conv_half(c, part):
        slot = 2 * c + part
        buf = ubuf.at[slot % ubuf.shape[0]]
        cols = slice(part * D_FF + c * FF_CHUNK, part * D_FF + (c + 1) * FF_CHUNK)
        buf[g2:g2 + ts, :] = _dot(u_ref[...], wu_ref[:, cols])
        for k in range(2):
            cur = pltpu.roll(buf[ts + k * SUBLANES:ts + (k + 1) * SUBLANES, :], 1, axis=0)
            prev = pltpu.roll(carry_ref[slot, k * SUBLANES:(k + 1) * SUBLANES, :], 1, axis=0)
            buf[k * SUBLANES:(k + 1) * SUBLANES, :] = jnp.where(first_sublane, prev, cur)
        a = (cw_ref[0, 0:1, cols] * buf[0:ts, :]
             + cw_ref[0, 1:2, cols] * buf[SUBLANES:SUBLANES + ts, :]
             + cw_ref[0, 2:3, cols] * buf[g2:g2 + ts, :] + cb_ref[:, cols])
        carry_ref[slot] = buf[ts:ts + g2, :]
        return a

    for c in range(N_FF_CHUNKS):
        gate = conv_half(c, 0)
        val = conv_half(c, 1)
        act_ref[:, c * FF_CHUNK:(c + 1) * FF_CHUNK] = (gate / (1.0 + jnp.exp(-gate)) * val).astype(_BF16)

    res = h + _dot(act_ref[...], wd_ref[...])
    for l in range(D_MODEL // LANES):
        res_ref[l] = res[:, l * LANES:(l + 1) * LANES]
    _load_deinterleaved(res_ref, o_ref)


def _ffn(h, g_ffn, w_up, conv_w, conv_b, w_down, tiles_per_seq):
    n = h.shape[1]
    ts = ROW_TILE
    g2 = 2 * SUBLANES
    return pl.pallas_call(
        functools.partial(_ffn_kernel, tiles_per_seq),
        grid=(n // ts,),
        in_specs=[
            pl.BlockSpec((D_MODEL // LANES, ts, LANES), lambda i: (0, i, 0)),
            _const_spec((1, D_MODEL)),
            _const_spec((D_MODEL, 2 * D_FF)),
            _const_spec((1, 3, 2 * D_FF)),
            _const_spec((1, 2 * D_FF)),
            _const_spec((D_FF, D_MODEL)),
        ],
        out_specs=pl.BlockSpec((ts, D_MODEL), lambda i: (i, 0)),
        out_shape=jax.ShapeDtypeStruct((n, D_MODEL), _F32),
        scratch_shapes=[
            pltpu.VMEM((ts, D_MODEL), _BF16),
            pltpu.VMEM((4, ts + g2, FF_CHUNK), _F32),
            pltpu.VMEM((2 * N_FF_CHUNKS, g2, FF_CHUNK), _F32),
            pltpu.VMEM((ts, D_FF), _BF16),
            pltpu.VMEM((D_MODEL // LANES, ts, LANES), _F32),
        ],
        compiler_params=pltpu.CompilerParams(
            dimension_semantics=("arbitrary",), vmem_limit_bytes=VMEM_LIMIT_BYTES),
        name="ffn",
    )(h, g_ffn, w_up, conv_w, conv_b, w_down)


def kernel(x, norm_mix_g, w_in, conv_w, q_norm_g, k_norm_g, rel_bias_table, sinks, out_norm_conv_g,
           out_norm_attn_g, w_out, norm_ffn_g, w_up, ffn_conv_w, ffn_conv_b, w_down):
    bsz, seq, d = x.shape
    assert d == D_MODEL and seq % ROW_TILE == 0 and norm_mix_g.shape[0] == 1
    tiles_per_seq = seq // ROW_TILE
    n = bsz * seq
    x2 = x.reshape(n, d)
    bkt = jnp.asarray(_bucket_map())
    hm = _head_mean_matrix()

    h, w_up_b, w_down_b = _mix_attn(
        rel_bias_table, sinks, x2, norm_mix_g, w_in[0], conv_w, q_norm_g, k_norm_g,
        out_norm_conv_g, hm, bkt, out_norm_attn_g, w_out[0],
        w_up[0], w_down[0], tiles_per_seq)
    out = _ffn(h, norm_ffn_g, w_up_b, ffn_conv_w, ffn_conv_b, w_down_b, tiles_per_seq)
    return out.reshape(bsz, seq, d)
```

```python
import functools
import math

import jax
import jax.numpy as jnp
import numpy as np
from jax import lax
from jax.experimental import pallas as pl
from jax.experimental.pallas import tpu as pltpu

D_MODEL = 1024
CONV_WIDTH = 512
HEAD_DIM = 64
N_HEADS = 8
N_KV_HEADS = 2
GQA_GROUP = N_HEADS // N_KV_HEADS
ATTN_WIDTH = N_HEADS * HEAD_DIM
KV_WIDTH = N_KV_HEADS * HEAD_DIM
WINDOW = 128
BLK = 128
NUM_BUCKETS = 32
MAX_DISTANCE = 128
MAX_EXACT = NUM_BUCKETS // 2
D_FF = 2816
EPS = 1e-6
NEG_INF = -1e30
LOG2E = math.log2(math.e)
IN_WIDTH = 3 * CONV_WIDTH + ATTN_WIDTH + 2 * KV_WIDTH

LANES = 128
SUBLANES = 8
MXU_TILE = 256
PAIR = 2 * HEAD_DIM
assert PAIR == LANES and KV_WIDTH == LANES and GQA_GROUP == 4
KX_WIDTH = 2 * N_KV_HEADS * LANES
VX_WIDTH = 2 * KX_WIDTH

ROW_TILE = 1024
FF_CHUNK = MXU_TILE
N_FF_CHUNKS = D_FF // FF_CHUNK
HALO = 8
LOOKAHEAD = 1
VMEM_LIMIT_BYTES = 60 * 1024 * 1024

_BF16 = jnp.bfloat16
_F32 = jnp.float32


def _dot(a, b):
    return jnp.dot(a, b, preferred_element_type=_F32)


def _dot_nt(a, b):
    return lax.dot_general(a, b, (((1,), (1,)), ((), ())), preferred_element_type=_F32)


def _rms(xf, g):
    return xf * lax.rsqrt(jnp.mean(xf * xf, axis=-1, keepdims=True) + EPS) * g


def _bucket_map():
    q = np.arange(BLK, dtype=np.int32)[:, None]
    j = np.arange(2 * BLK, dtype=np.int32)[None, :]
    d = q + BLK - j
    n = np.maximum(d, 0)
    nf = np.maximum(n, 1).astype(np.float32)
    large = MAX_EXACT + (np.log(nf / MAX_EXACT) / math.log(MAX_DISTANCE / MAX_EXACT)
                         * (NUM_BUCKETS - MAX_EXACT)).astype(np.int32)
    large = np.minimum(large, NUM_BUCKETS - 1)
    bucket = np.where(n < MAX_EXACT, n, large).astype(np.int32)
    within = (d >= 0) & (d < WINDOW)
    return np.where(within, bucket, -1).astype(np.int32)


def _head_mean_matrix():
    i = np.arange(ATTN_WIDTH)
    m = (i[:, None] // HEAD_DIM == i[None, :] // HEAD_DIM).astype(np.float32) / HEAD_DIM
    return jnp.asarray(m, dtype=_BF16)


def _const_spec(shape):
    return pl.BlockSpec(shape, lambda i: (0,) * len(shape), pipeline_mode=pl.Buffered(1))


def _store_interleaved(slab_ref, val):
    ts = val.shape[0]
    seg = ts // SUBLANES
    for l in range(D_MODEL // LANES):
        for s in range(SUBLANES):
            slab_ref[l, pl.ds(s, seg, stride=SUBLANES), :] = val[s * seg:(s + 1) * seg, l * LANES:(l + 1) * LANES]


def _load_deinterleaved(slab_ref, out_ref):
    ts = out_ref.shape[0]
    seg = ts // SUBLANES
    for l in range(D_MODEL // LANES):
        for s in range(SUBLANES):
            out_ref[s * seg:(s + 1) * seg, l * LANES:(l + 1) * LANES] = slab_ref[l, pl.ds(s, seg, stride=SUBLANES), :]


def _lo_hi_layout(a, lo):
    ar = pltpu.roll(a, HEAD_DIM, axis=1)
    zero = jnp.zeros_like(a)
    return [jnp.where(lo, a, zero), jnp.where(lo, zero, ar), jnp.where(lo, ar, zero), jnp.where(lo, zero, a)]


def _mix_attn_kernel(tiles_per_seq, tab_ref, sink_ref, x_ref, g_ref, w_f32, cw_ref, gq_head, gk_head, gc_ref,
                     hm_ref, bkt_ref, ga_ref, wo_f32, wu_f32, wd_f32,
                     o_ref, wu_bf16, wd_bf16,
                     bias_ref, w_ref, wo_ref, gq_ref, gk_ref, q_ref, kx_ref, vx_ref, kb0_ref, vb0_ref, gatec_ref, gateb_ref, zbuf,
                     yan_ref, p_ref, st_ref):
    ts = x_ref.shape[0]
    nblk = ts // BLK
    i = pl.program_id(0)
    seq_start = (i % tiles_per_seq == 0)

    wu_bf16[...] = wu_f32[...].astype(_BF16)
    wd_bf16[...] = wd_f32[...].astype(_BF16)

    @pl.when(i == 0)
    def _():
        bkt = bkt_ref[...]
        col = lax.broadcasted_iota(jnp.int32, (BLK, 2 * BLK), 1)

        def head_body(h, carry):
            b = jnp.full((BLK, 2 * BLK), NEG_INF, _F32)
            for t in range(NUM_BUCKETS):
                b = jnp.where(bkt == t, tab_ref[t, h] * LOG2E, b)
            bias_ref[0, h] = b
            bias_ref[1, h] = jnp.where(col >= BLK, b, NEG_INF)
            return carry

        lax.fori_loop(0, N_HEADS, head_body, 0)
        w_ref[...] = w_f32[...].astype(_BF16)
        wo_ref[...] = wo_f32[...].astype(_BF16)
        for h in range(N_HEADS):
            gq_ref[:, h * HEAD_DIM:(h + 1) * HEAD_DIM] = gq_head[...]
        for h in range(N_KV_HEADS):
            gk_ref[:, h * HEAD_DIM:(h + 1) * HEAD_DIM] = gk_head[...]
        kx_ref[ts - BLK:ts, :] = jnp.zeros((BLK, KX_WIDTH), _BF16)
        vx_ref[ts - BLK:ts, :] = jnp.zeros((BLK, VX_WIDTH), _BF16)

    @pl.when(seq_start)
    def _():
        zbuf[0:HALO, :] = jnp.zeros((HALO, CONV_WIDTH), _F32)

    kb0_ref[0:BLK, :] = kx_ref[ts - BLK:ts, :]
    vb0_ref[0:BLK, :] = vx_ref[ts - BLK:ts, :]

    u = _rms(x_ref[...], g_ref[...]).astype(_BF16)
    c0 = 3 * CONV_WIDTH
    c1 = c0 + ATTN_WIDTH
    q = _dot(u, w_ref[:, c0:c0 + ATTN_WIDTH])
    kv = _dot(u, w_ref[:, c1:c1 + 2 * KV_WIDTH])
    k = kv[:, 0:KV_WIDTH]
    msq = _dot((q * q).astype(_BF16), hm_ref[...])
    msk = _dot((k * k).astype(_BF16), hm_ref[0:KV_WIDTH, 0:KV_WIDTH])

    q_ref[...] = (q * lax.rsqrt(msq + EPS) * gq_ref[...] * (HEAD_DIM ** -0.5 * LOG2E)).astype(_BF16)
    kn = k * lax.rsqrt(msk + EPS) * gk_ref[...]
    lo = lax.broadcasted_iota(jnp.int32, (ts, LANES), 1) < HEAD_DIM
    kx_ref[...] = jnp.concatenate(_lo_hi_layout(kn, lo), axis=1).astype(_BF16)
    ones = [jnp.where(lo, 1.0, 0.0).astype(_F32), jnp.where(lo, 0.0, 1.0).astype(_F32)]
    v_pieces = _lo_hi_layout(kv[:, KV_WIDTH:2 * KV_WIDTH], lo)
    vx_ref[...] = jnp.concatenate([piece for s, v in enumerate(v_pieces) for piece in (v, ones[s % 2])],
                                  axis=1).astype(_BF16)
    kb0_ref[BLK:2 * BLK, :] = kx_ref[0:BLK, :]
    vb0_ref[BLK:2 * BLK, :] = vx_ref[0:BLK, :]

    lane_lo = lax.broadcasted_iota(jnp.int32, (BLK, LANES), 1) < HEAD_DIM

    def band(tile_ref, band0_ref, j, cols):
        if j == 0:
            return band0_ref[:, cols]
        return tile_ref[(j - 1) * BLK:(j + 1) * BLK, cols]

    def scores_stage(j):
        r0 = j * BLK
        slot = j % (LOOKAHEAD + 1)
        first = seq_start.astype(jnp.int32) if j == 0 else 0
        for g in range(N_KV_HEADS):
            qq = q_ref[pl.ds(r0, BLK), g * 2 * LANES:(g + 1) * 2 * LANES]
            lhs = jnp.concatenate([qq[:, 0:LANES], qq[:, LANES:2 * LANES]], axis=0)
            sink_terms = [[None, None], [None, None]]
            for par in range(2):
                kb = band(kx_ref, kb0_ref, j, slice((2 * g + par) * LANES, (2 * g + par + 1) * LANES))
                logits = _dot_nt(lhs, kb)
                for pair in range(2):
                    h = g * GQA_GROUP + 2 * pair + par
                    lg = logits[pair * BLK:(pair + 1) * BLK, :] + bias_ref[first, h]
                    sink = sink_ref[0, h] * LOG2E
                    m = jnp.maximum(jnp.max(lg, axis=-1, keepdims=True), sink)
                    p_ref[slot, 2 * g + par, pair * BLK:(pair + 1) * BLK, :] = jnp.exp2(lg - m).astype(_BF16)
                    sink_terms[pair][par] = jnp.exp2(sink - m)
            for pair in range(2):
                st_ref[slot, 2 * g + pair] = jnp.where(lane_lo, sink_terms[pair][0], sink_terms[pair][1])

    def values_stage(j):
        r0 = j * BLK
        slot = j % (LOOKAHEAD + 1)
        pairs = []
        for g in range(N_KV_HEADS):
            out = None
            for par in range(2):
                c = (4 * g + 2 * par) * LANES
                o = _dot(p_ref[slot, 2 * g + par],
                         band(vx_ref, vb0_ref, j, slice(c, c + 2 * LANES)))
                out = o if out is None else out + o
            for pair in range(2):
                blk = out[pair * BLK:(pair + 1) * BLK, :]
                den = blk[:, LANES:2 * LANES] + st_ref[slot, 2 * g + pair]
                pairs.append(blk[:, 0:LANES] / den)
        ya = jnp.concatenate(pairs, axis=1)
        yan_ref[r0:r0 + BLK, :] = _rms(ya, ga_ref[...]).astype(_BF16)

    def conv_piece(piece):
        half = piece % 2
        cols = slice(half * MXU_TILE, (half + 1) * MXU_TILE)
        w_cols = lambda base: w_ref[:, base + half * MXU_TILE:base + (half + 1) * MXU_TILE]
        if piece < 2:
            gatec_ref[:, cols] = _dot(u, w_cols(CONV_WIDTH))
        elif piece < 4:
            zbuf[HALO:HALO + ts, cols] = gatec_ref[:, cols] * _dot(u, w_cols(2 * CONV_WIDTH))
        else:
            gateb_ref[:, cols] = _dot(u, w_cols(0))

    piece_after_block = {0: 1, 1: 2, 3: 3, 5: 4, 6: 5}
    conv_piece(0)
    for j in range(LOOKAHEAD):
        scores_stage(j)
    for j in range(nblk):
        if j + LOOKAHEAD < nblk:
            scores_stage(j + LOOKAHEAD)
        if j in piece_after_block:
            conv_piece(piece_after_block[j])
        values_stage(j)

    conv = (cw_ref[0, 0:1, :] * zbuf[HALO - 2:HALO - 2 + ts, :]
            + cw_ref[0, 1:2, :] * zbuf[HALO - 1:HALO - 1 + ts, :]
            + cw_ref[0, 2:3, :] * zbuf[HALO:HALO + ts, :])
    yc = _rms(gateb_ref[...] * conv, gc_ref[...]).astype(_BF16)
    zbuf[0:HALO, :] = zbuf[ts:ts + HALO, :]

    h = (x_ref[...] + _dot(yan_ref[...], wo_ref[CONV_WIDTH:CONV_WIDTH + ATTN_WIDTH, :])
         + _dot(yc, wo_ref[0:CONV_WIDTH, :]))
    o_ref[...] = h


def _mix_attn(tab, sinks, x2, g_mix, w_in, conv_w, gq, gk, g_conv, hm, bkt, g_attn, w_out, w_up, w_down,
              tiles_per_seq):
    n = x2.shape[0]
    ts = ROW_TILE
    steps = n // ts
    assert ts // BLK >= 8 and CONV_WIDTH == 2 * MXU_TILE
    smem = pl.BlockSpec(memory_space=pltpu.SMEM)

    def slice_spec(w):
        rows, cols = w.shape
        assert rows % (steps * 2 * SUBLANES) == 0
        return pl.BlockSpec((rows // steps, cols), lambda i: (i, 0))

    cast_weights = (w_up, w_down)
    return pl.pallas_call(
        functools.partial(_mix_attn_kernel, tiles_per_seq),
        grid=(steps,),
        in_specs=[
            smem,
            smem,
            pl.BlockSpec((ts, D_MODEL), lambda i: (i, 0)),
            _const_spec((1, D_MODEL)),
            _const_spec((D_MODEL, IN_WIDTH)),
            _const_spec((1, 3, CONV_WIDTH)),
            _const_spec((1, HEAD_DIM)),
            _const_spec((1, HEAD_DIM)),
            _const_spec((1, CONV_WIDTH)),
            _const_spec((ATTN_WIDTH, ATTN_WIDTH)),
            _const_spec((BLK, 2 * BLK)),
            _const_spec((1, ATTN_WIDTH)),
            _const_spec((CONV_WIDTH + ATTN_WIDTH, D_MODEL)),
        ] + [slice_spec(w) for w in cast_weights],
        out_specs=[pl.BlockSpec((ts, D_MODEL), lambda i: (i, 0))]
        + [slice_spec(w) for w in cast_weights],
        out_shape=[jax.ShapeDtypeStruct((n, D_MODEL), _F32)]
        + [jax.ShapeDtypeStruct(w.shape, _BF16) for w in cast_weights],
        scratch_shapes=[
            pltpu.VMEM((2, N_HEADS, BLK, 2 * BLK), _F32),
            pltpu.VMEM((D_MODEL, IN_WIDTH), _BF16),
            pltpu.VMEM((CONV_WIDTH + ATTN_WIDTH, D_MODEL), _BF16),
            pltpu.VMEM((1, ATTN_WIDTH), _F32),
            pltpu.VMEM((1, KV_WIDTH), _F32),
            pltpu.VMEM((ts, ATTN_WIDTH), _BF16),
            pltpu.VMEM((ts, KX_WIDTH), _BF16),
            pltpu.VMEM((ts, VX_WIDTH), _BF16),
            pltpu.VMEM((2 * BLK, KX_WIDTH), _BF16),
            pltpu.VMEM((2 * BLK, VX_WIDTH), _BF16),
            pltpu.VMEM((ts, CONV_WIDTH), _F32),
            pltpu.VMEM((ts, CONV_WIDTH), _F32),
            pltpu.VMEM((ts + HALO, CONV_WIDTH), _F32),
            pltpu.VMEM((ts, ATTN_WIDTH), _BF16),
            pltpu.VMEM((LOOKAHEAD + 1, 2 * N_KV_HEADS, 2 * BLK, 2 * BLK), _BF16),
            pltpu.VMEM((LOOKAHEAD + 1, 2 * N_KV_HEADS, BLK, LANES), _F32),
        ],
        compiler_params=pltpu.CompilerParams(
            dimension_semantics=("arbitrary",), vmem_limit_bytes=VMEM_LIMIT_BYTES),
        name="mix_attn",
    )(tab, sinks, x2, g_mix, w_in, conv_w, gq, gk, g_conv, hm, bkt, g_attn, w_out, *cast_weights)


def _ffn_kernel(tiles_per_seq, h_ref, g_ref, wu_ref, cw_ref, cb_ref, wd_ref, o_ref,
                hs_ref, u_ref, ubuf, carry_ref, act_ref, res_ref):
    ts = o_ref.shape[0]
    i = pl.program_id(0)
    g2 = 2 * SUBLANES

    @pl.when(i % tiles_per_seq == 0)
    def _():
        carry_ref[...] = jnp.zeros(carry_ref.shape, _F32)

    _store_interleaved(hs_ref, h_ref[...])
    h = jnp.concatenate([hs_ref[l] for l in range(D_MODEL // LANES)], axis=1)
    u_ref[...] = _rms(h, g_ref[...]).astype(_BF16)
    first_sublane = lax.broadcasted_iota(jnp.int32, (SUBLANES, FF_CHUNK), 0) == 0

    def conv_half(c, part):
        slot = 2 * c + part
        buf = ubuf.at[slot % ubuf.shape[0]]
        cols = slice(part * D_FF + c * FF_CHUNK, part * D_FF + (c + 1) * FF_CHUNK)
        buf[g2:g2 + ts, :] = _dot(u_ref[...], wu_ref[:, cols])
        for k in range(2):
            cur = pltpu.roll(buf[ts + k * SUBLANES:ts + (k + 1) * SUBLANES, :], 1, axis=0)
            prev = pltpu.roll(carry_ref[slot, k * SUBLANES:(k + 1) * SUBLANES, :], 1, axis=0)
            buf[k * SUBLANES:(k + 1) * SUBLANES, :] = jnp.where(first_sublane, prev, cur)
        a = (cw_ref[0, 0:1, cols] * buf[0:ts, :]
             + cw_ref[0, 1:2, cols] * buf[SUBLANES:SUBLANES + ts, :]
             + cw_ref[0, 2:3, cols] * buf[g2:g2 + ts, :] + cb_ref[:, cols])
        carry_ref[slot] = buf[ts:ts + g2, :]
        return a

    for c in range(N_FF_CHUNKS):
        gate = conv_half(c, 0)
        val = conv_half(c, 1)
        act_ref[:, c * FF_CHUNK:(c + 1) * FF_CHUNK] = (gate / (1.0 + jnp.exp(-gate)) * val).astype(_BF16)

    res = h + _dot(act_ref[...], wd_ref[...])
    for l in range(D_MODEL // LANES):
        res_ref[l] = res[:, l * LANES:(l + 1) * LANES]
    _load_deinterleaved(res_ref, o_ref)


def _ffn(h, g_ffn, w_up, conv_w, conv_b, w_down, tiles_per_seq):
    n = h.shape[0]
    ts = ROW_TILE
    g2 = 2 * SUBLANES
    return pl.pallas_call(
        functools.partial(_ffn_kernel, tiles_per_seq),
        grid=(n // ts,),
        in_specs=[
            pl.BlockSpec((ts, D_MODEL), lambda i: (i, 0)),
            _const_spec((1, D_MODEL)),
            _const_spec((D_MODEL, 2 * D_FF)),
            _const_spec((1, 3, 2 * D_FF)),
            _const_spec((1, 2 * D_FF)),
            _const_spec((D_FF, D_MODEL)),
        ],
        out_specs=pl.BlockSpec((ts, D_MODEL), lambda i: (i, 0)),
        out_shape=jax.ShapeDtypeStruct((n, D_MODEL), _F32),
        scratch_shapes=[
            pltpu.VMEM((D_MODEL // LANES, ts, LANES), _F32),
            pltpu.VMEM((ts, D_MODEL), _BF16),
            pltpu.VMEM((4, ts + g2, FF_CHUNK), _F32),
            pltpu.VMEM((2 * N_FF_CHUNKS, g2, FF_CHUNK), _F32),
            pltpu.VMEM((ts, D_FF), _BF16),
            pltpu.VMEM((D_MODEL // LANES, ts, LANES), _F32),
        ],
        compiler_params=pltpu.CompilerParams(
            dimension_semantics=("arbitrary",), vmem_limit_bytes=VMEM_LIMIT_BYTES),
        name="ffn",
    )(h, g_ffn, w_up, conv_w, conv_b, w_down)


def kernel(x, norm_mix_g, w_in, conv_w, q_norm_g, k_norm_g, rel_bias_table, sinks, out_norm_conv_g,
           out_norm_attn_g, w_out, norm_ffn_g, w_up, ffn_conv_w, ffn_conv_b, w_down):
    bsz, seq, d = x.shape
    assert d == D_MODEL and seq % ROW_TILE == 0 and norm_mix_g.shape[0] == 1
    tiles_per_seq = seq // ROW_TILE
    n = bsz * seq
    x2 = x.reshape(n, d)
    bkt = jnp.asarray(_bucket_map())
    hm = _head_mean_matrix()

    h, w_up_b, w_down_b = _mix_attn(
        rel_bias_table, sinks, x2, norm_mix_g, w_in[0], conv_w, q_norm_g, k_norm_g,
        out_norm_conv_g, hm, bkt, out_norm_attn_g, w_out[0],
        w_up[0], w_down[0], tiles_per_seq)
    out = _ffn(h, norm_ffn_g, w_up_b, ffn_conv_w, ffn_conv_b, w_down_b, tiles_per_seq)
    return out.reshape(bsz, seq, d)
```

```python
import functools
import math

import jax
import jax.numpy as jnp
import numpy as np
from jax import lax
from jax.experimental import pallas as pl
from jax.experimental.pallas import tpu as pltpu

D_MODEL = 1024
CONV_WIDTH = 512
HEAD_DIM = 64
N_HEADS = 8
N_KV_HEADS = 2
GQA_GROUP = N_HEADS // N_KV_HEADS
ATTN_WIDTH = N_HEADS * HEAD_DIM
KV_WIDTH = N_KV_HEADS * HEAD_DIM
WINDOW = 128
BLK = 128
NUM_BUCKETS = 32
MAX_DISTANCE = 128
MAX_EXACT = NUM_BUCKETS // 2
D_FF = 2816
EPS = 1e-6
NEG_INF = -1e30
LOG2E = math.log2(math.e)
IN_WIDTH = 3 * CONV_WIDTH + ATTN_WIDTH + 2 * KV_WIDTH

LANES = 128
SUBLANES = 8
MXU_TILE = 256
PAIR = 2 * HEAD_DIM
assert PAIR == LANES and KV_WIDTH == LANES and GQA_GROUP == 4
KX_WIDTH = 2 * N_KV_HEADS * LANES
VX_WIDTH = 2 * KX_WIDTH

ROW_TILE = 1024
FF_CHUNK = MXU_TILE
N_FF_CHUNKS = D_FF // FF_CHUNK
HALO = 8
LOOKAHEAD = 1
VMEM_LIMIT_BYTES = 60 * 1024 * 1024

_BF16 = jnp.bfloat16
_F32 = jnp.float32


def _dot(a, b):
    return jnp.dot(a, b, preferred_element_type=_F32)


def _dot_nt(a, b):
    return lax.dot_general(a, b, (((1,), (1,)), ((), ())), preferred_element_type=_F32)


def _rms(xf, g):
    return xf * lax.rsqrt(jnp.mean(xf * xf, axis=-1, keepdims=True) + EPS) * g


def _bucket_map():
    q = np.arange(BLK, dtype=np.int32)[:, None]
    j = np.arange(2 * BLK, dtype=np.int32)[None, :]
    d = q + BLK - j
    n = np.maximum(d, 0)
    nf = np.maximum(n, 1).astype(np.float32)
    large = MAX_EXACT + (np.log(nf / MAX_EXACT) / math.log(MAX_DISTANCE / MAX_EXACT)
                         * (NUM_BUCKETS - MAX_EXACT)).astype(np.int32)
    large = np.minimum(large, NUM_BUCKETS - 1)
    bucket = np.where(n < MAX_EXACT, n, large).astype(np.int32)
    within = (d >= 0) & (d < WINDOW)
    return np.where(within, bucket, -1).astype(np.int32)


def _head_mean_matrix():
    i = np.arange(ATTN_WIDTH)
    m = (i[:, None] // HEAD_DIM == i[None, :] // HEAD_DIM).astype(np.float32) / HEAD_DIM
    return jnp.asarray(m, dtype=_BF16)


def _const_spec(shape):
    return pl.BlockSpec(shape, lambda i: (0,) * len(shape), pipeline_mode=pl.Buffered(1))


def _store_interleaved(slab_ref, val):
    ts = val.shape[0]
    seg = ts // SUBLANES
    for l in range(D_MODEL // LANES):
        for s in range(SUBLANES):
            slab_ref[l, pl.ds(s, seg, stride=SUBLANES), :] = val[s * seg:(s + 1) * seg, l * LANES:(l + 1) * LANES]


def _load_deinterleaved(slab_ref, out_ref):
    ts = out_ref.shape[0]
    seg = ts // SUBLANES
    for l in range(D_MODEL // LANES):
        for s in range(SUBLANES):
            out_ref[s * seg:(s + 1) * seg, l * LANES:(l + 1) * LANES] = slab_ref[l, pl.ds(s, seg, stride=SUBLANES), :]


def _lo_hi_layout(a, lo):
    ar = pltpu.roll(a, HEAD_DIM, axis=1)
    zero = jnp.zeros_like(a)
    return [jnp.where(lo, a, zero), jnp.where(lo, zero, ar), jnp.where(lo, ar, zero), jnp.where(lo, zero, a)]


def _mix_attn_kernel(tiles_per_seq, tab_ref, sink_ref, x_ref, g_ref, w_f32, cw_ref, gq_head, gk_head, gc_ref,
                     hm_ref, bkt_ref, ga_ref, wo_f32, wu_f32, wd_f32,
                     o_ref, wu_bf16, wd_bf16,
                     bias_ref, w_ref, wo_ref, gq_ref, gk_ref, q_ref, kx_ref, vx_ref, kb0_ref, vb0_ref, gatec_ref, gateb_ref, zbuf,
                     yan_ref, p_ref, st_ref):
    ts = x_ref.shape[0]
    nblk = ts // BLK
    i = pl.program_id(0)
    seq_start = (i % tiles_per_seq == 0)

    wu_bf16[...] = wu_f32[...].astype(_BF16)
    wd_bf16[...] = wd_f32[...].astype(_BF16)

    @pl.when(i == 0)
    def _():
        bkt = bkt_ref[...]
        col = lax.broadcasted_iota(jnp.int32, (BLK, 2 * BLK), 1)

        def head_body(h, carry):
            b = jnp.full((BLK, 2 * BLK), NEG_INF, _F32)
            for t in range(NUM_BUCKETS):
                b = jnp.where(bkt == t, tab_ref[t, h] * LOG2E, b)
            bias_ref[0, h] = b
            bias_ref[1, h] = jnp.where(col >= BLK, b, NEG_INF)
            return carry

        lax.fori_loop(0, N_HEADS, head_body, 0)
        w_ref[...] = w_f32[...].astype(_BF16)
        wo_ref[...] = wo_f32[...].astype(_BF16)
        for h in range(N_HEADS):
            gq_ref[:, h * HEAD_DIM:(h + 1) * HEAD_DIM] = gq_head[...]
        for h in range(N_KV_HEADS):
            gk_ref[:, h * HEAD_DIM:(h + 1) * HEAD_DIM] = gk_head[...]
        kx_ref[ts - BLK:ts, :] = jnp.zeros((BLK, KX_WIDTH), _BF16)
        vx_ref[ts - BLK:ts, :] = jnp.zeros((BLK, VX_WIDTH), _BF16)

    @pl.when(seq_start)
    def _():
        zbuf[0:HALO, :] = jnp.zeros((HALO, CONV_WIDTH), _F32)

    kb0_ref[0:BLK, :] = kx_ref[ts - BLK:ts, :]
    vb0_ref[0:BLK, :] = vx_ref[ts - BLK:ts, :]

    u = _rms(x_ref[...], g_ref[...]).astype(_BF16)
    c0 = 3 * CONV_WIDTH
    c1 = c0 + ATTN_WIDTH
    q = _dot(u, w_ref[:, c0:c0 + ATTN_WIDTH])
    kv = _dot(u, w_ref[:, c1:c1 + 2 * KV_WIDTH])
    k = kv[:, 0:KV_WIDTH]
    msq = _dot((q * q).astype(_BF16), hm_ref[...])
    msk = _dot((k * k).astype(_BF16), hm_ref[0:KV_WIDTH, 0:KV_WIDTH])

    q_ref[...] = (q * lax.rsqrt(msq + EPS) * gq_ref[...] * (HEAD_DIM ** -0.5 * LOG2E)).astype(_BF16)
    kn = k * lax.rsqrt(msk + EPS) * gk_ref[...]
    lo = lax.broadcasted_iota(jnp.int32, (ts, LANES), 1) < HEAD_DIM
    kx_ref[...] = jnp.concatenate(_lo_hi_layout(kn, lo), axis=1).astype(_BF16)
    ones = [jnp.where(lo, 1.0, 0.0).astype(_F32), jnp.where(lo, 0.0, 1.0).astype(_F32)]
    v_pieces = _lo_hi_layout(kv[:, KV_WIDTH:2 * KV_WIDTH], lo)
    vx_ref[...] = jnp.concatenate([piece for s, v in enumerate(v_pieces) for piece in (v, ones[s % 2])],
                                  axis=1).astype(_BF16)
    kb0_ref[BLK:2 * BLK, :] = kx_ref[0:BLK, :]
    vb0_ref[BLK:2 * BLK, :] = vx_ref[0:BLK, :]

    lane_lo = lax.broadcasted_iota(jnp.int32, (BLK, LANES), 1) < HEAD_DIM

    def band(tile_ref, band0_ref, j, cols):
        if j == 0:
            return band0_ref[:, cols]
        return tile_ref[(j - 1) * BLK:(j + 1) * BLK, cols]

    def scores_stage(j):
        r0 = j * BLK
        slot = j % (LOOKAHEAD + 1)
        first = seq_start.astype(jnp.int32) if j == 0 else 0
        for g in range(N_KV_HEADS):
            qq = q_ref[pl.ds(r0, BLK), g * 2 * LANES:(g + 1) * 2 * LANES]
            lhs = jnp.concatenate([qq[:, 0:LANES], qq[:, LANES:2 * LANES]], axis=0)
            sink_terms = [[None, None], [None, None]]
            for par in range(2):
                kb = band(kx_ref, kb0_ref, j, slice((2 * g + par) * LANES, (2 * g + par + 1) * LANES))
                logits = _dot_nt(lhs, kb)
                for pair in range(2):
                    h = g * GQA_GROUP + 2 * pair + par
                    lg = logits[pair * BLK:(pair + 1) * BLK, :] + bias_ref[first, h]
                    sink = sink_ref[0, h] * LOG2E
                    m = jnp.maximum(jnp.max(lg, axis=-1, keepdims=True), sink)
                    p_ref[slot, 2 * g + par, pair * BLK:(pair + 1) * BLK, :] = jnp.exp2(lg - m).astype(_BF16)
                    sink_terms[pair][par] = jnp.exp2(sink - m)
            for pair in range(2):
                st_ref[slot, 2 * g + pair] = jnp.where(lane_lo, sink_terms[pair][0], sink_terms[pair][1])

    def values_stage(j):
        r0 = j * BLK
        slot = j % (LOOKAHEAD + 1)
        pairs = []
        for g in range(N_KV_HEADS):
            out = None
            for par in range(2):
                c = (4 * g + 2 * par) * LANES
                o = _dot(p_ref[slot, 2 * g + par],
                         band(vx_ref, vb0_ref, j, slice(c, c + 2 * LANES)))
                out = o if out is None else out + o
            for pair in range(2):
                blk = out[pair * BLK:(pair + 1) * BLK, :]
                den = blk[:, LANES:2 * LANES] + st_ref[slot, 2 * g + pair]
                pairs.append(blk[:, 0:LANES] / den)
        ya = jnp.concatenate(pairs, axis=1)
        yan_ref[r0:r0 + BLK, :] = _rms(ya, ga_ref[...]).astype(_BF16)

    def conv_piece(piece):
        half = piece % 2
        cols = slice(half * MXU_TILE, (half + 1) * MXU_TILE)
        w_cols = lambda base: w_ref[:, base + half * MXU_TILE:base + (half + 1) * MXU_TILE]
        if piece < 2:
            gatec_ref[:, cols] = _dot(u, w_cols(CONV_WIDTH))
        elif piece < 4:
            zbuf[HALO:HALO + ts, cols] = gatec_ref[:, cols] * _dot(u, w_cols(2 * CONV_WIDTH))
        else:
            gateb_ref[:, cols] = _dot(u, w_cols(0))

    piece_after_block = {0: 1, 1: 2, 3: 3, 5: 4, 6: 5}
    conv_piece(0)
    for j in range(LOOKAHEAD):
        scores_stage(j)
    for j in range(nblk):
        if j + LOOKAHEAD < nblk:
            scores_stage(j + LOOKAHEAD)
        if j in piece_after_block:
            conv_piece(piece_after_block[j])
        values_stage(j)

    conv = (cw_ref[0, 0:1, :] * zbuf[HALO - 2:HALO - 2 + ts, :]
            + cw_ref[0, 1:2, :] * zbuf[HALO - 1:HALO - 1 + ts, :]
            + cw_ref[0, 2:3, :] * zbuf[HALO:HALO + ts, :])
    yc = _rms(gateb_ref[...] * conv, gc_ref[...]).astype(_BF16)
    zbuf[0:HALO, :] = zbuf[ts:ts + HALO, :]

    h = (x_ref[...] + _dot(yc, wo_ref[0:CONV_WIDTH, :])
         + _dot(yan_ref[...], wo_ref[CONV_WIDTH:CONV_WIDTH + ATTN_WIDTH, :]))
    _store_interleaved(o_ref, h)


def _mix_attn(tab, sinks, x2, g_mix, w_in, conv_w, gq, gk, g_conv, hm, bkt, g_attn, w_out, w_up, w_down,
              tiles_per_seq):
    n = x2.shape[0]
    ts = ROW_TILE
    steps = n // ts
    assert ts // BLK >= 8 and CONV_WIDTH == 2 * MXU_TILE
    smem = pl.BlockSpec(memory_space=pltpu.SMEM)

    def slice_spec(w):
        rows, cols = w.shape
        assert rows % (steps * 2 * SUBLANES) == 0
        return pl.BlockSpec((rows // steps, cols), lambda i: (i, 0))

    cast_weights = (w_up, w_down)
    return pl.pallas_call(
        functools.partial(_mix_attn_kernel, tiles_per_seq),
        grid=(steps,),
        in_specs=[
            smem,
            smem,
            pl.BlockSpec((ts, D_MODEL), lambda i: (i, 0)),
            _const_spec((1, D_MODEL)),
            _const_spec((D_MODEL, IN_WIDTH)),
            _const_spec((1, 3, CONV_WIDTH)),
            _const_spec((1, HEAD_DIM)),
            _const_spec((1, HEAD_DIM)),
            _const_spec((1, CONV_WIDTH)),
            _const_spec((ATTN_WIDTH, ATTN_WIDTH)),
            _const_spec((BLK, 2 * BLK)),
            _const_spec((1, ATTN_WIDTH)),
            _const_spec((CONV_WIDTH + ATTN_WIDTH, D_MODEL)),
        ] + [slice_spec(w) for w in cast_weights],
        out_specs=[pl.BlockSpec((D_MODEL // LANES, ts, LANES), lambda i: (0, i, 0))]
        + [slice_spec(w) for w in cast_weights],
        out_shape=[jax.ShapeDtypeStruct((D_MODEL // LANES, n, LANES), _F32)]
        + [jax.ShapeDtypeStruct(w.shape, _BF16) for w in cast_weights],
        scratch_shapes=[
            pltpu.VMEM((2, N_HEADS, BLK, 2 * BLK), _F32),
            pltpu.VMEM((D_MODEL, IN_WIDTH), _BF16),
            pltpu.VMEM((CONV_WIDTH + ATTN_WIDTH, D_MODEL), _BF16),
            pltpu.VMEM((1, ATTN_WIDTH), _F32),
            pltpu.VMEM((1, KV_WIDTH), _F32),
            pltpu.VMEM((ts, ATTN_WIDTH), _BF16),
            pltpu.VMEM((ts, KX_WIDTH), _BF16),
            pltpu.VMEM((ts, VX_WIDTH), _BF16),
            pltpu.VMEM((2 * BLK, KX_WIDTH), _BF16),
            pltpu.VMEM((2 * BLK, VX_WIDTH), _BF16),
            pltpu.VMEM((ts, CONV_WIDTH), _F32),
            pltpu.VMEM((ts, CONV_WIDTH), _F32),
            pltpu.VMEM((ts + HALO, CONV_WIDTH), _F32),
            pltpu.VMEM((ts, ATTN_WIDTH), _BF16),
            pltpu.VMEM((LOOKAHEAD + 1, 2 * N_KV_HEADS, 2 * BLK, 2 * BLK), _BF16),
            pltpu.VMEM((LOOKAHEAD + 1, 2 * N_KV_HEADS, BLK, LANES), _F32),
        ],
        compiler_params=pltpu.CompilerParams(
            dimension_semantics=("arbitrary",), vmem_limit_bytes=VMEM_LIMIT_BYTES),
        name="mix_attn",
    )(tab, sinks, x2, g_mix, w_in, conv_w, gq, gk, g_conv, hm, bkt, g_attn, w_out, *cast_weights)


def _ffn_kernel(tiles_per_seq, h_ref, g_ref, wu_ref, cw_ref, cb_ref, wd_ref, o_ref,
                u_ref, ubuf, carry_ref, act_ref, res_ref):
    ts = o_ref.shape[0]
    i = pl.program_id(0)
    g2 = 2 * SUBLANES

    @pl.when(i % tiles_per_seq == 0)
    def _():
        carry_ref[...] = jnp.zeros(carry_ref.shape, _F32)

    h = jnp.concatenate([h_ref[l] for l in range(D_MODEL // LANES)], axis=1)
    u_ref[...] = _rms(h, g_ref[...]).astype(_BF16)
    first_sublane = lax.broadcasted_iota(jnp.int32, (SUBLANES, FF_CHUNK), 0) == 0

    def conv_half(c, part):
        slot = 2 * c + part
        buf = ubuf.at[slot % ubuf.shape[0]]
        cols = slice(part * D_FF + c * FF_CHUNK, part * D_FF + (c + 1) * FF_CHUNK)
        buf[g2:g2 + ts, :] = _dot(u_ref[...], wu_ref[:, cols])
        for k in range(2):
            cur = pltpu.roll(buf[ts + k * SUBLANES:ts + (k + 1) * SUBLANES, :], 1, axis=0)
            prev = pltpu.roll(carry_ref[slot, k * SUBLANES:(k + 1) * SUBLANES, :], 1, axis=0)
            buf[k * SUBLANES:(k + 1) * SUBLANES, :] = jnp.where(first_sublane, prev, cur)
        a = (cw_ref[0, 0:1, cols] * buf[0:ts, :]
             + cw_ref[0, 1:2, cols] * buf[SUBLANES:SUBLANES + ts, :]
             + cw_ref[0, 2:3, cols] * buf[g2:g2 + ts, :] + cb_ref[:, cols])
        carry_ref[slot] = buf[ts:ts + g2, :]
        return a

    for c in range(N_FF_CHUNKS):
        gate = conv_half(c, 0)
        val = conv_half(c, 1)
        act_ref[:, c * FF_CHUNK:(c + 1) * FF_CHUNK] = (gate / (1.0 + jnp.exp(-gate)) * val).astype(_BF16)

    res = h + _dot(act_ref[...], wd_ref[...])
    for l in range(D_MODEL // LANES):
        res_ref[l] = res[:, l * LANES:(l + 1) * LANES]
    _load_deinterleaved(res_ref, o_ref)


def _ffn(h, g_ffn, w_up, conv_w, conv_b, w_down, tiles_per_seq):
    n = h.shape[1]
    ts = ROW_TILE
    g2 = 2 * SUBLANES
    return pl.pallas_call(
        functools.partial(_ffn_kernel, tiles_per_seq),
        grid=(n // ts,),
        in_specs=[
            pl.BlockSpec((D_MODEL // LANES, ts, LANES), lambda i: (0, i, 0)),
            _const_spec((1, D_MODEL)),
            _const_spec((D_MODEL, 2 * D_FF)),
            _const_spec((1, 3, 2 * D_FF)),
            _const_spec((1, 2 * D_FF)),
            _const_spec((D_FF, D_MODEL)),
        ],
        out_specs=pl.BlockSpec((ts, D_MODEL), lambda i: (i, 0)),
        out_shape=jax.ShapeDtypeStruct((n, D_MODEL), _F32),
        scratch_shapes=[
            pltpu.VMEM((ts, D_MODEL), _BF16),
            pltpu.VMEM((4, ts + g2, FF_CHUNK), _F32),
            pltpu.VMEM((2 * N_FF_CHUNKS, g2, FF_CHUNK), _F32),
            pltpu.VMEM((ts, D_FF), _BF16),
            pltpu.VMEM((D_MODEL // LANES, ts, LANES), _F32),
        ],
        compiler_params=pltpu.CompilerParams(
            dimension_semantics=("arbitrary",), vmem_limit_bytes=VMEM_LIMIT_BYTES),
        name="ffn",
    )(h, g_ffn, w_up, conv_w, conv_b, w_down)


def kernel(x, norm_mix_g, w_in, conv_w, q_norm_g, k_norm_g, rel_bias_table, sinks, out_norm_conv_g,
           out_norm_attn_g, w_out, norm_ffn_g, w_up, ffn_conv_w, ffn_conv_b, w_down):
    bsz, seq, d = x.shape
    assert d == D_MODEL and seq % ROW_TILE == 0 and norm_mix_g.shape[0] == 1
    tiles_per_seq = seq // ROW_TILE
    n = bsz * seq
    x2 = x.reshape(n, d)
    bkt = jnp.asarray(_bucket_map())
    hm = _head_mean_matrix()

    h, w_up_b, w_down_b = _mix_attn(
        rel_bias_table, sinks, x2, norm_mix_g, w_in[0], conv_w, q_norm_g, k_norm_g,
        out_norm_conv_g, hm, bkt, out_norm_attn_g, w_out[0],
        w_up[0], w_down[0], tiles_per_seq)
    out = _ffn(h, norm_ffn_g, w_up_b, ffn_conv_w, ffn_conv_b, w_down_b, tiles_per_seq)
    return out.reshape(bsz, seq, d)
```

```python
import functools
import math

import jax
import jax.numpy as jnp
import numpy as np
from jax import lax
from jax.experimental import pallas as pl
from jax.experimental.pallas import tpu as pltpu

D_MODEL = 1024
CONV_WIDTH = 512
HEAD_DIM = 64
N_HEADS = 8
N_KV_HEADS = 2
GQA_GROUP = N_HEADS // N_KV_HEADS
ATTN_WIDTH = N_HEADS * HEAD_DIM
KV_WIDTH = N_KV_HEADS * HEAD_DIM
WINDOW = 128
BLK = 128
NUM_BUCKETS = 32
MAX_DISTANCE = 128
MAX_EXACT = NUM_BUCKETS // 2
D_FF = 2816
EPS = 1e-6
NEG_INF = -1e30
LOG2E = math.log2(math.e)
IN_WIDTH = 3 * CONV_WIDTH + ATTN_WIDTH + 2 * KV_WIDTH

LANES = 128
SUBLANES = 8
MXU_TILE = 256
PAIR = 2 * HEAD_DIM
assert PAIR == LANES and KV_WIDTH == LANES and GQA_GROUP == 4
KX_WIDTH = 2 * N_KV_HEADS * LANES
VX_WIDTH = 2 * KX_WIDTH

ROW_TILE = 1024
FF_CHUNK = MXU_TILE
N_FF_CHUNKS = D_FF // FF_CHUNK
HEAD_PIECES = 4
HALO = 8
LOOKAHEAD = 1
VMEM_LIMIT_BYTES = 60 * 1024 * 1024

_BF16 = jnp.bfloat16
_F32 = jnp.float32


def _dot(a, b):
    return jnp.dot(a, b, preferred_element_type=_F32)


def _dot_nt(a, b):
    return lax.dot_general(a, b, (((1,), (1,)), ((), ())), preferred_element_type=_F32)


def _rms(xf, g):
    return xf * lax.rsqrt(jnp.mean(xf * xf, axis=-1, keepdims=True) + EPS) * g


def _bucket_map():
    q = np.arange(BLK, dtype=np.int32)[:, None]
    j = np.arange(2 * BLK, dtype=np.int32)[None, :]
    d = q + BLK - j
    n = np.maximum(d, 0)
    nf = np.maximum(n, 1).astype(np.float32)
    large = MAX_EXACT + (np.log(nf / MAX_EXACT) / math.log(MAX_DISTANCE / MAX_EXACT)
                         * (NUM_BUCKETS - MAX_EXACT)).astype(np.int32)
    large = np.minimum(large, NUM_BUCKETS - 1)
    bucket = np.where(n < MAX_EXACT, n, large).astype(np.int32)
    within = (d >= 0) & (d < WINDOW)
    return np.where(within, bucket, -1).astype(np.int32)


def _head_mean_matrix():
    i = np.arange(ATTN_WIDTH)
    m = (i[:, None] // HEAD_DIM == i[None, :] // HEAD_DIM).astype(np.float32) / HEAD_DIM
    return jnp.asarray(m, dtype=_BF16)


def _const_spec(shape):
    return pl.BlockSpec(shape, lambda i: (0,) * len(shape), pipeline_mode=pl.Buffered(1))


def _store_interleaved(slab_ref, val):
    ts = val.shape[0]
    seg = ts // SUBLANES
    for l in range(D_MODEL // LANES):
        for s in range(SUBLANES):
            slab_ref[l, pl.ds(s, seg, stride=SUBLANES), :] = val[s * seg:(s + 1) * seg, l * LANES:(l + 1) * LANES]


def _load_deinterleaved(slab_ref, out_ref):
    ts = out_ref.shape[0]
    seg = ts // SUBLANES
    for l in range(D_MODEL // LANES):
        for s in range(SUBLANES):
            out_ref[s * seg:(s + 1) * seg, l * LANES:(l + 1) * LANES] = slab_ref[l, pl.ds(s, seg, stride=SUBLANES), :]


def _lo_hi_layout(a, lo):
    ar = pltpu.roll(a, HEAD_DIM, axis=1)
    zero = jnp.zeros_like(a)
    return [jnp.where(lo, a, zero), jnp.where(lo, zero, ar), jnp.where(lo, ar, zero), jnp.where(lo, zero, a)]


def _mix_attn_kernel(tiles_per_seq, tab_ref, sink_ref, x_ref, g_ref, w_f32, cw_ref, gq_head, gk_head, gc_ref,
                     hm_ref, bkt_ref, ga_ref, wo_f32, wu_f32, wd_f32,
                     o_ref, wu_bf16, wd_bf16,
                     bias_ref, w_ref, wo_ref, gq_ref, gk_ref, q_ref, kx_ref, vx_ref, kb0_ref, vb0_ref, gatec_ref, gateb_ref, zbuf,
                     yan_ref, p_ref, st_ref):
    ts = x_ref.shape[0]
    nblk = ts // BLK
    i = pl.program_id(0)
    seq_start = (i % tiles_per_seq == 0)

    wu_bf16[...] = wu_f32[...].astype(_BF16)
    wd_bf16[...] = wd_f32[...].astype(_BF16)

    @pl.when(i == 0)
    def _():
        bkt = bkt_ref[...]
        col = lax.broadcasted_iota(jnp.int32, (BLK, 2 * BLK), 1)

        def head_body(h, carry):
            b = jnp.full((BLK, 2 * BLK), NEG_INF, _F32)
            for t in range(NUM_BUCKETS):
                b = jnp.where(bkt == t, tab_ref[t, h] * LOG2E, b)
            bias_ref[0, h] = b
            bias_ref[1, h] = jnp.where(col >= BLK, b, NEG_INF)
            return carry

        lax.fori_loop(0, N_HEADS, head_body, 0)
        w_ref[...] = w_f32[...].astype(_BF16)
        wo_ref[...] = wo_f32[...].astype(_BF16)
        for h in range(N_HEADS):
            gq_ref[:, h * HEAD_DIM:(h + 1) * HEAD_DIM] = gq_head[...]
        for h in range(N_KV_HEADS):
            gk_ref[:, h * HEAD_DIM:(h + 1) * HEAD_DIM] = gk_head[...]
        kx_ref[ts - BLK:ts, :] = jnp.zeros((BLK, KX_WIDTH), _BF16)
        vx_ref[ts - BLK:ts, :] = jnp.zeros((BLK, VX_WIDTH), _BF16)

    @pl.when(seq_start)
    def _():
        zbuf[0:HALO, :] = jnp.zeros((HALO, CONV_WIDTH), _F32)

    kb0_ref[0:BLK, :] = kx_ref[ts - BLK:ts, :]
    vb0_ref[0:BLK, :] = vx_ref[ts - BLK:ts, :]

    u = _rms(x_ref[...], g_ref[...]).astype(_BF16)
    c0 = 3 * CONV_WIDTH
    c1 = c0 + ATTN_WIDTH
    q = _dot(u, w_ref[:, c0:c0 + ATTN_WIDTH])
    kv = _dot(u, w_ref[:, c1:c1 + 2 * KV_WIDTH])
    k = kv[:, 0:KV_WIDTH]
    msq = _dot((q * q).astype(_BF16), hm_ref[...])
    msk = _dot((k * k).astype(_BF16), hm_ref[0:KV_WIDTH, 0:KV_WIDTH])

    q_ref[...] = (q * lax.rsqrt(msq + EPS) * gq_ref[...] * (HEAD_DIM ** -0.5 * LOG2E)).astype(_BF16)
    kn = k * lax.rsqrt(msk + EPS) * gk_ref[...]
    lo = lax.broadcasted_iota(jnp.int32, (ts, LANES), 1) < HEAD_DIM
    kx_ref[...] = jnp.concatenate(_lo_hi_layout(kn, lo), axis=1).astype(_BF16)
    ones = [jnp.where(lo, 1.0, 0.0).astype(_F32), jnp.where(lo, 0.0, 1.0).astype(_F32)]
    v_pieces = _lo_hi_layout(kv[:, KV_WIDTH:2 * KV_WIDTH], lo)
    vx_ref[...] = jnp.concatenate([piece for s, v in enumerate(v_pieces) for piece in (v, ones[s % 2])],
                                  axis=1).astype(_BF16)
    kb0_ref[BLK:2 * BLK, :] = kx_ref[0:BLK, :]
    vb0_ref[BLK:2 * BLK, :] = vx_ref[0:BLK, :]

    lane_lo = lax.broadcasted_iota(jnp.int32, (BLK, LANES), 1) < HEAD_DIM

    def band(tile_ref, band0_ref, j, cols):
        if j == 0:
            return band0_ref[:, cols]
        return tile_ref[(j - 1) * BLK:(j + 1) * BLK, cols]

    def scores_stage(j):
        r0 = j * BLK
        slot = j % (LOOKAHEAD + 1)
        first = seq_start.astype(jnp.int32) if j == 0 else 0
        for g in range(N_KV_HEADS):
            qq = q_ref[pl.ds(r0, BLK), g * 2 * LANES:(g + 1) * 2 * LANES]
            lhs = jnp.concatenate([qq[:, 0:LANES], qq[:, LANES:2 * LANES]], axis=0)
            sink_terms = [[None, None], [None, None]]
            for par in range(2):
                kb = band(kx_ref, kb0_ref, j, slice((2 * g + par) * LANES, (2 * g + par + 1) * LANES))
                logits = _dot_nt(lhs, kb)
                for pair in range(2):
                    h = g * GQA_GROUP + 2 * pair + par
                    lg = logits[pair * BLK:(pair + 1) * BLK, :] + bias_ref[first, h]
                    sink = sink_ref[0, h] * LOG2E
                    m = jnp.maximum(jnp.max(lg, axis=-1, keepdims=True), sink)
                    p_ref[slot, 2 * g + par, pair * BLK:(pair + 1) * BLK, :] = jnp.exp2(lg - m).astype(_BF16)
                    sink_terms[pair][par] = jnp.exp2(sink - m)
            for pair in range(2):
                st_ref[slot, 2 * g + pair] = jnp.where(lane_lo, sink_terms[pair][0], sink_terms[pair][1])

    def values_stage(j):
        r0 = j * BLK
        slot = j % (LOOKAHEAD + 1)
        pairs = []
        for g in range(N_KV_HEADS):
            out = None
            for par in range(2):
                c = (4 * g + 2 * par) * LANES
                o = _dot(p_ref[slot, 2 * g + par],
                         band(vx_ref, vb0_ref, j, slice(c, c + 2 * LANES)))
                out = o if out is None else out + o
            for pair in range(2):
                blk = out[pair * BLK:(pair + 1) * BLK, :]
                den = blk[:, LANES:2 * LANES] + st_ref[slot, 2 * g + pair]
                pairs.append(blk[:, 0:LANES] / den)
        ya = jnp.concatenate(pairs, axis=1)
        yan_ref[r0:r0 + BLK, :] = _rms(ya, ga_ref[...]).astype(_BF16)

    def conv_piece(piece):
        half = piece % 2
        cols = slice(half * MXU_TILE, (half + 1) * MXU_TILE)
        w_cols = lambda base: w_ref[:, base + half * MXU_TILE:base + (half + 1) * MXU_TILE]
        if piece < 2:
            gatec_ref[:, cols] = _dot(u, w_cols(CONV_WIDTH))
        elif piece < 4:
            zbuf[HALO:HALO + ts, cols] = gatec_ref[:, cols] * _dot(u, w_cols(2 * CONV_WIDTH))
        else:
            gateb_ref[:, cols] = _dot(u, w_cols(0))

    piece_after_block = {0: 1, 1: 2, 3: 3, 5: 4, 6: 5}
    conv_piece(0)
    for j in range(LOOKAHEAD):
        scores_stage(j)
    for j in range(nblk):
        if j + LOOKAHEAD < nblk:
            scores_stage(j + LOOKAHEAD)
        if j in piece_after_block:
            conv_piece(piece_after_block[j])
        values_stage(j)

    conv = (cw_ref[0, 0:1, :] * zbuf[HALO - 2:HALO - 2 + ts, :]
            + cw_ref[0, 1:2, :] * zbuf[HALO - 1:HALO - 1 + ts, :]
            + cw_ref[0, 2:3, :] * zbuf[HALO:HALO + ts, :])
    yc = _rms(gateb_ref[...] * conv, gc_ref[...]).astype(_BF16)
    zbuf[0:HALO, :] = zbuf[ts:ts + HALO, :]

    h = (x_ref[...] + _dot(yc, wo_ref[0:CONV_WIDTH, :])
         + _dot(yan_ref[...], wo_ref[CONV_WIDTH:CONV_WIDTH + ATTN_WIDTH, :]))
    _store_interleaved(o_ref, h)


def _mix_attn(tab, sinks, x2, g_mix, w_in, conv_w, gq, gk, g_conv, hm, bkt, g_attn, w_out, w_up, w_down,
              tiles_per_seq):
    n = x2.shape[0]
    ts = ROW_TILE
    steps = n // ts
    assert ts // BLK >= 8 and CONV_WIDTH == 2 * MXU_TILE
    smem = pl.BlockSpec(memory_space=pltpu.SMEM)

    def slice_spec(w):
        rows, cols = w.shape
        assert rows % (steps * 2 * SUBLANES) == 0
        return pl.BlockSpec((rows // steps, cols), lambda i: (i, 0))

    cast_weights = (w_up, w_down)
    return pl.pallas_call(
        functools.partial(_mix_attn_kernel, tiles_per_seq),
        grid=(steps,),
        in_specs=[
            smem,
            smem,
            pl.BlockSpec((ts, D_MODEL), lambda i: (i, 0)),
            _const_spec((1, D_MODEL)),
            _const_spec((D_MODEL, IN_WIDTH)),
            _const_spec((1, 3, CONV_WIDTH)),
            _const_spec((1, HEAD_DIM)),
            _const_spec((1, HEAD_DIM)),
            _const_spec((1, CONV_WIDTH)),
            _const_spec((ATTN_WIDTH, ATTN_WIDTH)),
            _const_spec((BLK, 2 * BLK)),
            _const_spec((1, ATTN_WIDTH)),
            _const_spec((CONV_WIDTH + ATTN_WIDTH, D_MODEL)),
        ] + [slice_spec(w) for w in cast_weights],
        out_specs=[pl.BlockSpec((D_MODEL // LANES, ts, LANES), lambda i: (0, i, 0))]
        + [slice_spec(w) for w in cast_weights],
        out_shape=[jax.ShapeDtypeStruct((D_MODEL // LANES, n, LANES), _F32)]
        + [jax.ShapeDtypeStruct(w.shape, _BF16) for w in cast_weights],
        scratch_shapes=[
            pltpu.VMEM((2, N_HEADS, BLK, 2 * BLK), _F32),
            pltpu.VMEM((D_MODEL, IN_WIDTH), _BF16),
            pltpu.VMEM((CONV_WIDTH + ATTN_WIDTH, D_MODEL), _BF16),
            pltpu.VMEM((1, ATTN_WIDTH), _F32),
            pltpu.VMEM((1, KV_WIDTH), _F32),
            pltpu.VMEM((ts, ATTN_WIDTH), _BF16),
            pltpu.VMEM((ts, KX_WIDTH), _BF16),
            pltpu.VMEM((ts, VX_WIDTH), _BF16),
            pltpu.VMEM((2 * BLK, KX_WIDTH), _BF16),
            pltpu.VMEM((2 * BLK, VX_WIDTH), _BF16),
            pltpu.VMEM((ts, CONV_WIDTH), _F32),
            pltpu.VMEM((ts, CONV_WIDTH), _F32),
            pltpu.VMEM((ts + HALO, CONV_WIDTH), _F32),
            pltpu.VMEM((ts, ATTN_WIDTH), _BF16),
            pltpu.VMEM((LOOKAHEAD + 1, 2 * N_KV_HEADS, 2 * BLK, 2 * BLK), _BF16),
            pltpu.VMEM((LOOKAHEAD + 1, 2 * N_KV_HEADS, BLK, LANES), _F32),
        ],
        compiler_params=pltpu.CompilerParams(
            dimension_semantics=("arbitrary",), vmem_limit_bytes=VMEM_LIMIT_BYTES),
        name="mix_attn",
    )(tab, sinks, x2, g_mix, w_in, conv_w, gq, gk, g_conv, hm, bkt, g_attn, w_out, *cast_weights)


def _ffn_kernel(tiles_per_seq, h_ref, g_ref, wu_ref, cw_ref, cb_ref, wd_ref, o_ref,
                u_ref, ubuf, carry_ref, act_ref, res_ref):
    ts = o_ref.shape[0]
    i = pl.program_id(0)
    g2 = 2 * SUBLANES

    @pl.when(i % tiles_per_seq == 0)
    def _():
        carry_ref[...] = jnp.zeros(carry_ref.shape, _F32)

    h = jnp.concatenate([h_ref[l] for l in range(D_MODEL // LANES)], axis=1)
    first_sublane = lax.broadcasted_iota(jnp.int32, (SUBLANES, FF_CHUNK), 0) == 0
    head_rows = [slice(p * ts // HEAD_PIECES, (p + 1) * ts // HEAD_PIECES) for p in range(HEAD_PIECES)]

    def up_half(c, part, r):
        buf = ubuf.at[(2 * c + part) % ubuf.shape[0]]
        cols = slice(part * D_FF + c * FF_CHUNK, part * D_FF + (c + 1) * FF_CHUNK)
        buf[g2 + r.start:g2 + r.stop, :] = _dot(u_ref[r, :], wu_ref[:, cols])

    def conv_half(c, part):
        slot = 2 * c + part
        buf = ubuf.at[slot % ubuf.shape[0]]
        cols = slice(part * D_FF + c * FF_CHUNK, part * D_FF + (c + 1) * FF_CHUNK)
        for k in range(2):
            cur = pltpu.roll(buf[ts + k * SUBLANES:ts + (k + 1) * SUBLANES, :], 1, axis=0)
            prev = pltpu.roll(carry_ref[slot, k * SUBLANES:(k + 1) * SUBLANES, :], 1, axis=0)
            buf[k * SUBLANES:(k + 1) * SUBLANES, :] = jnp.where(first_sublane, prev, cur)
        a = (cw_ref[0, 0:1, cols] * buf[0:ts, :]
             + cw_ref[0, 1:2, cols] * buf[SUBLANES:SUBLANES + ts, :]
             + cw_ref[0, 2:3, cols] * buf[g2:g2 + ts, :] + cb_ref[:, cols])
        carry_ref[slot] = buf[ts:ts + g2, :]
        return a

    for r in head_rows:
        u_ref[r, :] = _rms(h[r, :], g_ref[...]).astype(_BF16)
    for r in head_rows:
        up_half(0, 0, r)
        up_half(0, 1, r)

    for c in range(N_FF_CHUNKS):
        if c > 0:
            up_half(c, 0, slice(0, ts))
        gate = conv_half(c, 0)
        if c > 0:
            up_half(c, 1, slice(0, ts))
        val = conv_half(c, 1)
        act_ref[:, c * FF_CHUNK:(c + 1) * FF_CHUNK] = (gate / (1.0 + jnp.exp(-gate)) * val).astype(_BF16)

    res = h + _dot(act_ref[...], wd_ref[...])
    for l in range(D_MODEL // LANES):
        res_ref[l] = res[:, l * LANES:(l + 1) * LANES]
    _load_deinterleaved(res_ref, o_ref)


def _ffn(h, g_ffn, w_up, conv_w, conv_b, w_down, tiles_per_seq):
    n = h.shape[1]
    ts = ROW_TILE
    g2 = 2 * SUBLANES
    return pl.pallas_call(
        functools.partial(_ffn_kernel, tiles_per_seq),
        grid=(n // ts,),
        in_specs=[
            pl.BlockSpec((D_MODEL // LANES, ts, LANES), lambda i: (0, i, 0)),
            _const_spec((1, D_MODEL)),
            _const_spec((D_MODEL, 2 * D_FF)),
            _const_spec((1, 3, 2 * D_FF)),
            _const_spec((1, 2 * D_FF)),
            _const_spec((D_FF, D_MODEL)),
        ],
        out_specs=pl.BlockSpec((ts, D_MODEL), lambda i: (i, 0)),
        out_shape=jax.ShapeDtypeStruct((n, D_MODEL), _F32),
        scratch_shapes=[
            pltpu.VMEM((ts, D_MODEL), _BF16),
            pltpu.VMEM((4, ts + g2, FF_CHUNK), _F32),
            pltpu.VMEM((2 * N_FF_CHUNKS, g2, FF_CHUNK), _F32),
            pltpu.VMEM((ts, D_FF), _BF16),
            pltpu.VMEM((D_MODEL // LANES, ts, LANES), _F32),
        ],
        compiler_params=pltpu.CompilerParams(
            dimension_semantics=("arbitrary",), vmem_limit_bytes=VMEM_LIMIT_BYTES),
        name="ffn",
    )(h, g_ffn, w_up, conv_w, conv_b, w_down)


def kernel(x, norm_mix_g, w_in, conv_w, q_norm_g, k_norm_g, rel_bias_table, sinks, out_norm_conv_g,
           out_norm_attn_g, w_out, norm_ffn_g, w_up, ffn_conv_w, ffn_conv_b, w_down):
    bsz, seq, d = x.shape
    assert d == D_MODEL and seq % ROW_TILE == 0 and norm_mix_g.shape[0] == 1
    tiles_per_seq = seq // ROW_TILE
    n = bsz * seq
    x2 = x.reshape(n, d)
    bkt = jnp.asarray(_bucket_map())
    hm = _head_mean_matrix()

    h, w_up_b, w_down_b = _mix_attn(
        rel_bias_table, sinks, x2, norm_mix_g, w_in[0], conv_w, q_norm_g, k_norm_g,
        out_norm_conv_g, hm, bkt, out_norm_attn_g, w_out[0],
        w_up[0], w_down[0], tiles_per_seq)
    out = _ffn(h, norm_ffn_g, w_up_b, ffn_conv_w, ffn_conv_b, w_down_b, tiles_per_seq)
    return out.reshape(bsz, seq, d)
```

```python
import functools
import math

import jax
import jax.numpy as jnp
import numpy as np
from jax import lax
from jax.experimental import pallas as pl
from jax.experimental.pallas import tpu as pltpu

D_MODEL = 1024
CONV_WIDTH = 512
HEAD_DIM = 64
N_HEADS = 8
N_KV_HEADS = 2
GQA_GROUP = N_HEADS // N_KV_HEADS
ATTN_WIDTH = N_HEADS * HEAD_DIM
KV_WIDTH = N_KV_HEADS * HEAD_DIM
WINDOW = 128
BLK = 128
NUM_BUCKETS = 32
MAX_DISTANCE = 128
MAX_EXACT = NUM_BUCKETS // 2
D_FF = 2816
EPS = 1e-6
NEG_INF = -1e30
LOG2E = math.log2(math.e)
IN_WIDTH = 3 * CONV_WIDTH + ATTN_WIDTH + 2 * KV_WIDTH

LANES = 128
SUBLANES = 8
MXU_TILE = 256
PAIR = 2 * HEAD_DIM
assert PAIR == LANES and KV_WIDTH == LANES and GQA_GROUP == 4
KX_WIDTH = 2 * N_KV_HEADS * LANES
VX_WIDTH = 2 * KX_WIDTH

ROW_TILE = 1024
FF_CHUNK = MXU_TILE
N_FF_CHUNKS = D_FF // FF_CHUNK
HEAD_PIECES = 4
HALO = 8
LOOKAHEAD = 1
VMEM_LIMIT_BYTES = 60 * 1024 * 1024

_BF16 = jnp.bfloat16
_F32 = jnp.float32


def _dot(a, b):
    return jnp.dot(a, b, preferred_element_type=_F32)


def _dot_nt(a, b):
    return lax.dot_general(a, b, (((1,), (1,)), ((), ())), preferred_element_type=_F32)


def _rms(xf, g):
    return xf * lax.rsqrt(jnp.mean(xf * xf, axis=-1, keepdims=True) + EPS) * g


def _bucket_map():
    q = np.arange(BLK, dtype=np.int32)[:, None]
    j = np.arange(2 * BLK, dtype=np.int32)[None, :]
    d = q + BLK - j
    n = np.maximum(d, 0)
    nf = np.maximum(n, 1).astype(np.float32)
    large = MAX_EXACT + (np.log(nf / MAX_EXACT) / math.log(MAX_DISTANCE / MAX_EXACT)
                         * (NUM_BUCKETS - MAX_EXACT)).astype(np.int32)
    large = np.minimum(large, NUM_BUCKETS - 1)
    bucket = np.where(n < MAX_EXACT, n, large).astype(np.int32)
    within = (d >= 0) & (d < WINDOW)
    return np.where(within, bucket, -1).astype(np.int32)


def _head_mean_matrix():
    i = np.arange(MXU_TILE)
    m = (i[:, None] // HEAD_DIM == i[None, :] // HEAD_DIM).astype(np.float32) / HEAD_DIM
    return jnp.asarray(m, dtype=_BF16)


def _const_spec(shape):
    return pl.BlockSpec(shape, lambda i: (0,) * len(shape), pipeline_mode=pl.Buffered(1))


def _store_interleaved(slab_ref, val):
    ts = val.shape[0]
    seg = ts // SUBLANES
    for l in range(D_MODEL // LANES):
        for s in range(SUBLANES):
            slab_ref[l, pl.ds(s, seg, stride=SUBLANES), :] = val[s * seg:(s + 1) * seg, l * LANES:(l + 1) * LANES]


def _load_deinterleaved(slab_ref, out_ref):
    ts = out_ref.shape[0]
    seg = ts // SUBLANES
    for l in range(D_MODEL // LANES):
        for s in range(SUBLANES):
            out_ref[s * seg:(s + 1) * seg, l * LANES:(l + 1) * LANES] = slab_ref[l, pl.ds(s, seg, stride=SUBLANES), :]


def _lo_hi_layout(a, lo):
    ar = pltpu.roll(a, HEAD_DIM, axis=1)
    zero = jnp.zeros_like(a)
    return [jnp.where(lo, a, zero), jnp.where(lo, zero, ar), jnp.where(lo, ar, zero), jnp.where(lo, zero, a)]


def _mix_attn_kernel(tiles_per_seq, tab_ref, sink_ref, x_ref, g_ref, w_f32, cw_ref, gq_head, gk_head, gc_ref,
                     hm_ref, bkt_ref, ga_ref, wo_f32, wu_f32, wd_f32,
                     o_ref, wu_bf16, wd_bf16,
                     bias_ref, w_ref, wo_ref, gq_ref, gk_ref, q_ref, kx_ref, vx_ref, kb0_ref, vb0_ref, gatec_ref, gateb_ref, zbuf,
                     yan_ref, p_ref, st_ref):
    ts = x_ref.shape[0]
    nblk = ts // BLK
    i = pl.program_id(0)
    seq_start = (i % tiles_per_seq == 0)

    wu_bf16[...] = wu_f32[...].astype(_BF16)
    wd_bf16[...] = wd_f32[...].astype(_BF16)

    @pl.when(i == 0)
    def _():
        bkt = bkt_ref[...]
        col = lax.broadcasted_iota(jnp.int32, (BLK, 2 * BLK), 1)

        def head_body(h, carry):
            b = jnp.full((BLK, 2 * BLK), NEG_INF, _F32)
            for t in range(NUM_BUCKETS):
                b = jnp.where(bkt == t, tab_ref[t, h] * LOG2E, b)
            bias_ref[0, h] = b
            bias_ref[1, h] = jnp.where(col >= BLK, b, NEG_INF)
            return carry

        lax.fori_loop(0, N_HEADS, head_body, 0)
        w_ref[...] = w_f32[...].astype(_BF16)
        wo_ref[...] = wo_f32[...].astype(_BF16)
        for h in range(N_HEADS):
            gq_ref[:, h * HEAD_DIM:(h + 1) * HEAD_DIM] = gq_head[...]
        for h in range(N_KV_HEADS):
            gk_ref[:, h * HEAD_DIM:(h + 1) * HEAD_DIM] = gk_head[...]
        kx_ref[ts - BLK:ts, :] = jnp.zeros((BLK, KX_WIDTH), _BF16)
        vx_ref[ts - BLK:ts, :] = jnp.zeros((BLK, VX_WIDTH), _BF16)

    @pl.when(seq_start)
    def _():
        zbuf[0:HALO, :] = jnp.zeros((HALO, CONV_WIDTH), _F32)

    kb0_ref[0:BLK, :] = kx_ref[ts - BLK:ts, :]
    vb0_ref[0:BLK, :] = vx_ref[ts - BLK:ts, :]

    u = _rms(x_ref[...], g_ref[...]).astype(_BF16)
    c0 = 3 * CONV_WIDTH
    c1 = c0 + ATTN_WIDTH
    q = _dot(u, w_ref[:, c0:c0 + ATTN_WIDTH])
    kv = _dot(u, w_ref[:, c1:c1 + 2 * KV_WIDTH])
    k = kv[:, 0:KV_WIDTH]
    qq = (q * q).astype(_BF16)
    msq = jnp.concatenate([_dot(qq[:, t * MXU_TILE:(t + 1) * MXU_TILE], hm_ref[...])
                           for t in range(ATTN_WIDTH // MXU_TILE)], axis=1)
    msk = _dot((k * k).astype(_BF16), hm_ref[0:KV_WIDTH, 0:KV_WIDTH])

    q_ref[...] = (q * lax.rsqrt(msq + EPS) * gq_ref[...] * (HEAD_DIM ** -0.5 * LOG2E)).astype(_BF16)
    kn = k * lax.rsqrt(msk + EPS) * gk_ref[...]
    lo = lax.broadcasted_iota(jnp.int32, (ts, LANES), 1) < HEAD_DIM
    kx_ref[...] = jnp.concatenate(_lo_hi_layout(kn, lo), axis=1).astype(_BF16)
    ones = [jnp.where(lo, 1.0, 0.0).astype(_F32), jnp.where(lo, 0.0, 1.0).astype(_F32)]
    v_pieces = _lo_hi_layout(kv[:, KV_WIDTH:2 * KV_WIDTH], lo)
    vx_ref[...] = jnp.concatenate([piece for s, v in enumerate(v_pieces) for piece in (v, ones[s % 2])],
                                  axis=1).astype(_BF16)
    kb0_ref[BLK:2 * BLK, :] = kx_ref[0:BLK, :]
    vb0_ref[BLK:2 * BLK, :] = vx_ref[0:BLK, :]

    lane_lo = lax.broadcasted_iota(jnp.int32, (BLK, LANES), 1) < HEAD_DIM

    def band(tile_ref, band0_ref, j, cols):
        if j == 0:
            return band0_ref[:, cols]
        return tile_ref[(j - 1) * BLK:(j + 1) * BLK, cols]

    def scores_stage(j):
        r0 = j * BLK
        slot = j % (LOOKAHEAD + 1)
        first = seq_start.astype(jnp.int32) if j == 0 else 0
        for g in range(N_KV_HEADS):
            qq = q_ref[pl.ds(r0, BLK), g * 2 * LANES:(g + 1) * 2 * LANES]
            lhs = jnp.concatenate([qq[:, 0:LANES], qq[:, LANES:2 * LANES]], axis=0)
            sink_terms = [[None, None], [None, None]]
            for par in range(2):
                kb = band(kx_ref, kb0_ref, j, slice((2 * g + par) * LANES, (2 * g + par + 1) * LANES))
                logits = _dot_nt(lhs, kb)
                for pair in range(2):
                    h = g * GQA_GROUP + 2 * pair + par
                    lg = logits[pair * BLK:(pair + 1) * BLK, :] + bias_ref[first, h]
                    sink = sink_ref[0, h] * LOG2E
                    m = jnp.maximum(jnp.max(lg, axis=-1, keepdims=True), sink)
                    p_ref[slot, 2 * g + par, pair * BLK:(pair + 1) * BLK, :] = jnp.exp2(lg - m).astype(_BF16)
                    sink_terms[pair][par] = jnp.exp2(sink - m)
            for pair in range(2):
                st_ref[slot, 2 * g + pair] = jnp.where(lane_lo, sink_terms[pair][0], sink_terms[pair][1])

    def values_stage(j):
        r0 = j * BLK
        slot = j % (LOOKAHEAD + 1)
        pairs = []
        for g in range(N_KV_HEADS):
            out = None
            for par in range(2):
                c = (4 * g + 2 * par) * LANES
                o = _dot(p_ref[slot, 2 * g + par],
                         band(vx_ref, vb0_ref, j, slice(c, c + 2 * LANES)))
                out = o if out is None else out + o
            for pair in range(2):
                blk = out[pair * BLK:(pair + 1) * BLK, :]
                den = blk[:, LANES:2 * LANES] + st_ref[slot, 2 * g + pair]
                pairs.append(blk[:, 0:LANES] / den)
        ya = jnp.concatenate(pairs, axis=1)
        yan_ref[r0:r0 + BLK, :] = _rms(ya, ga_ref[...]).astype(_BF16)

    def conv_piece(piece):
        half = piece % 2
        cols = slice(half * MXU_TILE, (half + 1) * MXU_TILE)
        w_cols = lambda base: w_ref[:, base + half * MXU_TILE:base + (half + 1) * MXU_TILE]
        if piece < 2:
            gatec_ref[:, cols] = _dot(u, w_cols(CONV_WIDTH))
        elif piece < 4:
            zbuf[HALO:HALO + ts, cols] = gatec_ref[:, cols] * _dot(u, w_cols(2 * CONV_WIDTH))
        else:
            gateb_ref[:, cols] = _dot(u, w_cols(0))

    piece_after_block = {0: 1, 1: 2, 3: 3, 5: 4, 6: 5}
    conv_piece(0)
    for j in range(LOOKAHEAD):
        scores_stage(j)
    for j in range(nblk):
        if j + LOOKAHEAD < nblk:
            scores_stage(j + LOOKAHEAD)
        if j in piece_after_block:
            conv_piece(piece_after_block[j])
        values_stage(j)

    conv = (cw_ref[0, 0:1, :] * zbuf[HALO - 2:HALO - 2 + ts, :]
            + cw_ref[0, 1:2, :] * zbuf[HALO - 1:HALO - 1 + ts, :]
            + cw_ref[0, 2:3, :] * zbuf[HALO:HALO + ts, :])
    yc = _rms(gateb_ref[...] * conv, gc_ref[...]).astype(_BF16)
    zbuf[0:HALO, :] = zbuf[ts:ts + HALO, :]

    h = (x_ref[...] + _dot(yc, wo_ref[0:CONV_WIDTH, :])
         + _dot(yan_ref[...], wo_ref[CONV_WIDTH:CONV_WIDTH + ATTN_WIDTH, :]))
    _store_interleaved(o_ref, h)


def _mix_attn(tab, sinks, x2, g_mix, w_in, conv_w, gq, gk, g_conv, hm, bkt, g_attn, w_out, w_up, w_down,
              tiles_per_seq):
    n = x2.shape[0]
    ts = ROW_TILE
    steps = n // ts
    assert ts // BLK >= 8 and CONV_WIDTH == 2 * MXU_TILE
    smem = pl.BlockSpec(memory_space=pltpu.SMEM)

    def slice_spec(w):
        rows, cols = w.shape
        assert rows % (steps * 2 * SUBLANES) == 0
        return pl.BlockSpec((rows // steps, cols), lambda i: (i, 0))

    cast_weights = (w_up, w_down)
    return pl.pallas_call(
        functools.partial(_mix_attn_kernel, tiles_per_seq),
        grid=(steps,),
        in_specs=[
            smem,
            smem,
            pl.BlockSpec((ts, D_MODEL), lambda i: (i, 0)),
            _const_spec((1, D_MODEL)),
            _const_spec((D_MODEL, IN_WIDTH)),
            _const_spec((1, 3, CONV_WIDTH)),
            _const_spec((1, HEAD_DIM)),
            _const_spec((1, HEAD_DIM)),
            _const_spec((1, CONV_WIDTH)),
            _const_spec((MXU_TILE, MXU_TILE)),
            _const_spec((BLK, 2 * BLK)),
            _const_spec((1, ATTN_WIDTH)),
            _const_spec((CONV_WIDTH + ATTN_WIDTH, D_MODEL)),
        ] + [slice_spec(w) for w in cast_weights],
        out_specs=[pl.BlockSpec((D_MODEL // LANES, ts, LANES), lambda i: (0, i, 0))]
        + [slice_spec(w) for w in cast_weights],
        out_shape=[jax.ShapeDtypeStruct((D_MODEL // LANES, n, LANES), _F32)]
        + [jax.ShapeDtypeStruct(w.shape, _BF16) for w in cast_weights],
        scratch_shapes=[
            pltpu.VMEM((2, N_HEADS, BLK, 2 * BLK), _F32),
            pltpu.VMEM((D_MODEL, IN_WIDTH), _BF16),
            pltpu.VMEM((CONV_WIDTH + ATTN_WIDTH, D_MODEL), _BF16),
            pltpu.VMEM((1, ATTN_WIDTH), _F32),
            pltpu.VMEM((1, KV_WIDTH), _F32),
            pltpu.VMEM((ts, ATTN_WIDTH), _BF16),
            pltpu.VMEM((ts, KX_WIDTH), _BF16),
            pltpu.VMEM((ts, VX_WIDTH), _BF16),
            pltpu.VMEM((2 * BLK, KX_WIDTH), _BF16),
            pltpu.VMEM((2 * BLK, VX_WIDTH), _BF16),
            pltpu.VMEM((ts, CONV_WIDTH), _F32),
            pltpu.VMEM((ts, CONV_WIDTH), _F32),
            pltpu.VMEM((ts + HALO, CONV_WIDTH), _F32),
            pltpu.VMEM((ts, ATTN_WIDTH), _BF16),
            pltpu.VMEM((LOOKAHEAD + 1, 2 * N_KV_HEADS, 2 * BLK, 2 * BLK), _BF16),
            pltpu.VMEM((LOOKAHEAD + 1, 2 * N_KV_HEADS, BLK, LANES), _F32),
        ],
        compiler_params=pltpu.CompilerParams(
            dimension_semantics=("arbitrary",), vmem_limit_bytes=VMEM_LIMIT_BYTES),
        name="mix_attn",
    )(tab, sinks, x2, g_mix, w_in, conv_w, gq, gk, g_conv, hm, bkt, g_attn, w_out, *cast_weights)


def _ffn_kernel(tiles_per_seq, h_ref, g_ref, wu_ref, cw_ref, cb_ref, wd_ref, o_ref,
                u_ref, ubuf, carry_ref, act_ref, res_ref):
    ts = o_ref.shape[0]
    i = pl.program_id(0)
    g2 = 2 * SUBLANES

    @pl.when(i % tiles_per_seq == 0)
    def _():
        carry_ref[...] = jnp.zeros(carry_ref.shape, _F32)

    h = jnp.concatenate([h_ref[l] for l in range(D_MODEL // LANES)], axis=1)
    first_sublane = lax.broadcasted_iota(jnp.int32, (SUBLANES, FF_CHUNK), 0) == 0
    head_rows = [slice(p * ts // HEAD_PIECES, (p + 1) * ts // HEAD_PIECES) for p in range(HEAD_PIECES)]

    def up_half(c, part, r):
        buf = ubuf.at[(2 * c + part) % ubuf.shape[0]]
        cols = slice(part * D_FF + c * FF_CHUNK, part * D_FF + (c + 1) * FF_CHUNK)
        buf[g2 + r.start:g2 + r.stop, :] = _dot(u_ref[r, :], wu_ref[:, cols])

    def conv_half(c, part):
        slot = 2 * c + part
        buf = ubuf.at[slot % ubuf.shape[0]]
        cols = slice(part * D_FF + c * FF_CHUNK, part * D_FF + (c + 1) * FF_CHUNK)
        for k in range(2):
            cur = pltpu.roll(buf[ts + k * SUBLANES:ts + (k + 1) * SUBLANES, :], 1, axis=0)
            prev = pltpu.roll(carry_ref[slot, k * SUBLANES:(k + 1) * SUBLANES, :], 1, axis=0)
            buf[k * SUBLANES:(k + 1) * SUBLANES, :] = jnp.where(first_sublane, prev, cur)
        a = (cw_ref[0, 0:1, cols] * buf[0:ts, :]
             + cw_ref[0, 1:2, cols] * buf[SUBLANES:SUBLANES + ts, :]
             + cw_ref[0, 2:3, cols] * buf[g2:g2 + ts, :] + cb_ref[:, cols])
        carry_ref[slot] = buf[ts:ts + g2, :]
        return a

    for r in head_rows:
        u_ref[r, :] = _rms(h[r, :], g_ref[...]).astype(_BF16)
    for r in head_rows:
        up_half(0, 0, r)
        up_half(0, 1, r)

    for c in range(N_FF_CHUNKS):
        if c > 0:
            up_half(c, 0, slice(0, ts))
        gate = conv_half(c, 0)
        if c > 0:
            up_half(c, 1, slice(0, ts))
        val = conv_half(c, 1)
        act_ref[:, c * FF_CHUNK:(c + 1) * FF_CHUNK] = (gate / (1.0 + jnp.exp(-gate)) * val).astype(_BF16)

    res = h + _dot(act_ref[...], wd_ref[...])
    for l in range(D_MODEL // LANES):
        res_ref[l] = res[:, l * LANES:(l + 1) * LANES]
    _load_deinterleaved(res_ref, o_ref)


def _ffn(h, g_ffn, w_up, conv_w, conv_b, w_down, tiles_per_seq):
    n = h.shape[1]
    ts = ROW_TILE
    g2 = 2 * SUBLANES
    return pl.pallas_call(
        functools.partial(_ffn_kernel, tiles_per_seq),
        grid=(n // ts,),
        in_specs=[
            pl.BlockSpec((D_MODEL // LANES, ts, LANES), lambda i: (0, i, 0)),
            _const_spec((1, D_MODEL)),
            _const_spec((D_MODEL, 2 * D_FF)),
            _const_spec((1, 3, 2 * D_FF)),
            _const_spec((1, 2 * D_FF)),
            _const_spec((D_FF, D_MODEL)),
        ],
        out_specs=pl.BlockSpec((ts, D_MODEL), lambda i: (i, 0)),
        out_shape=jax.ShapeDtypeStruct((n, D_MODEL), _F32),
        scratch_shapes=[
            pltpu.VMEM((ts, D_MODEL), _BF16),
            pltpu.VMEM((4, ts + g2, FF_CHUNK), _F32),
            pltpu.VMEM((2 * N_FF_CHUNKS, g2, FF_CHUNK), _F32),
            pltpu.VMEM((ts, D_FF), _BF16),
            pltpu.VMEM((D_MODEL // LANES, ts, LANES), _F32),
        ],
        compiler_params=pltpu.CompilerParams(
            dimension_semantics=("arbitrary",), vmem_limit_bytes=VMEM_LIMIT_BYTES),
        name="ffn",
    )(h, g_ffn, w_up, conv_w, conv_b, w_down)


def kernel(x, norm_mix_g, w_in, conv_w, q_norm_g, k_norm_g, rel_bias_table, sinks, out_norm_conv_g,
           out_norm_attn_g, w_out, norm_ffn_g, w_up, ffn_conv_w, ffn_conv_b, w_down):
    bsz, seq, d = x.shape
    assert d == D_MODEL and seq % ROW_TILE == 0 and norm_mix_g.shape[0] == 1
    tiles_per_seq = seq // ROW_TILE
    n = bsz * seq
    x2 = x.reshape(n, d)
    bkt = jnp.asarray(_bucket_map())
    hm = _head_mean_matrix()

    h, w_up_b, w_down_b = _mix_attn(
        rel_bias_table, sinks, x2, norm_mix_g, w_in[0], conv_w, q_norm_g, k_norm_g,
        out_norm_conv_g, hm, bkt, out_norm_attn_g, w_out[0],
        w_up[0], w_down[0], tiles_per_seq)
    out = _ffn(h, norm_ffn_g, w_up_b, ffn_conv_w, ffn_conv_b, w_down_b, tiles_per_seq)
    return out.reshape(bsz, seq, d)
```

```python
import functools
import math

import jax
import jax.numpy as jnp
import numpy as np
from jax import lax
from jax.experimental import pallas as pl
from jax.experimental.pallas import tpu as pltpu

D_MODEL = 1024
CONV_WIDTH = 512
HEAD_DIM = 64
N_HEADS = 8
N_KV_HEADS = 2
GQA_GROUP = N_HEADS // N_KV_HEADS
ATTN_WIDTH = N_HEADS * HEAD_DIM
KV_WIDTH = N_KV_HEADS * HEAD_DIM
WINDOW = 128
BLK = 128
NUM_BUCKETS = 32
MAX_DISTANCE = 128
MAX_EXACT = NUM_BUCKETS // 2
D_FF = 2816
EPS = 1e-6
NEG_INF = -1e30
LOG2E = math.log2(math.e)
IN_WIDTH = 3 * CONV_WIDTH + ATTN_WIDTH + 2 * KV_WIDTH

LANES = 128
SUBLANES = 8
MXU_TILE = 256
PAIR = 2 * HEAD_DIM
assert PAIR == LANES and KV_WIDTH == LANES and GQA_GROUP == 4
KX_WIDTH = 2 * N_KV_HEADS * LANES
VX_WIDTH = 2 * KX_WIDTH

ROW_TILE = 1024
FF_CHUNK = MXU_TILE
N_FF_CHUNKS = D_FF // FF_CHUNK
HEAD_PIECES = 4
HALO = 8
LOOKAHEAD = 1
VMEM_LIMIT_BYTES = 60 * 1024 * 1024

_BF16 = jnp.bfloat16
_F32 = jnp.float32


def _dot(a, b):
    return jnp.dot(a, b, preferred_element_type=_F32)


def _dot_nt(a, b):
    return lax.dot_general(a, b, (((1,), (1,)), ((), ())), preferred_element_type=_F32)


def _rms(xf, g):
    return xf * lax.rsqrt(jnp.mean(xf * xf, axis=-1, keepdims=True) + EPS) * g


def _bucket_map():
    q = np.arange(BLK, dtype=np.int32)[:, None]
    j = np.arange(2 * BLK, dtype=np.int32)[None, :]
    d = q + BLK - j
    n = np.maximum(d, 0)
    nf = np.maximum(n, 1).astype(np.float32)
    large = MAX_EXACT + (np.log(nf / MAX_EXACT) / math.log(MAX_DISTANCE / MAX_EXACT)
                         * (NUM_BUCKETS - MAX_EXACT)).astype(np.int32)
    large = np.minimum(large, NUM_BUCKETS - 1)
    bucket = np.where(n < MAX_EXACT, n, large).astype(np.int32)
    within = (d >= 0) & (d < WINDOW)
    return np.where(within, bucket, -1).astype(np.int32)


def _head_mean_matrix():
    i = np.arange(ATTN_WIDTH)
    m = (i[:, None] // HEAD_DIM == i[None, :] // HEAD_DIM).astype(np.float32) / HEAD_DIM
    return jnp.asarray(m, dtype=_BF16)


def _const_spec(shape):
    return pl.BlockSpec(shape, lambda i: (0,) * len(shape), pipeline_mode=pl.Buffered(1))


def _store_interleaved(slab_ref, val):
    ts = val.shape[0]
    seg = ts // SUBLANES
    for l in range(D_MODEL // LANES):
        for s in range(SUBLANES):
            slab_ref[l, pl.ds(s, seg, stride=SUBLANES), :] = val[s * seg:(s + 1) * seg, l * LANES:(l + 1) * LANES]


def _load_deinterleaved(slab_ref, out_ref):
    ts = out_ref.shape[0]
    seg = ts // SUBLANES
    for l in range(D_MODEL // LANES):
        for s in range(SUBLANES):
            out_ref[s * seg:(s + 1) * seg, l * LANES:(l + 1) * LANES] = slab_ref[l, pl.ds(s, seg, stride=SUBLANES), :]


def _lo_hi_layout(a, lo):
    ar = pltpu.roll(a, HEAD_DIM, axis=1)
    zero = jnp.zeros_like(a)
    return [jnp.where(lo, a, zero), jnp.where(lo, zero, ar), jnp.where(lo, ar, zero), jnp.where(lo, zero, a)]


def _mix_attn_kernel(tiles_per_seq, tab_ref, sink_ref, x_ref, g_ref, w_f32, cw_ref, gq_head, gk_head, gc_ref,
                     hm_ref, bkt_ref, ga_ref, wo_f32, wu_f32, wd_f32,
                     o_ref, wu_bf16, wd_bf16,
                     bias_ref, w_ref, wo_ref, gq_ref, gk_ref, q_ref, kx_ref, vx_ref, kb0_ref, vb0_ref, gatec_ref, gateb_ref, zbuf,
                     yan_ref, p_ref, st_ref):
    ts = x_ref.shape[0]
    nblk = ts // BLK
    i = pl.program_id(0)
    seq_start = (i % tiles_per_seq == 0)

    wu_bf16[...] = wu_f32[...].astype(_BF16)
    wd_bf16[...] = wd_f32[...].astype(_BF16)

    @pl.when(i == 0)
    def _():
        bkt = bkt_ref[...]
        col = lax.broadcasted_iota(jnp.int32, (BLK, 2 * BLK), 1)

        def head_body(h, carry):
            b = jnp.full((BLK, 2 * BLK), NEG_INF, _F32)
            for t in range(NUM_BUCKETS):
                b = jnp.where(bkt == t, tab_ref[h, t] * LOG2E, b)
            bias_ref[0, h] = b
            bias_ref[1, h] = jnp.where(col >= BLK, b, NEG_INF)
            return carry

        lax.fori_loop(0, N_HEADS, head_body, 0)
        w_ref[...] = w_f32[...].astype(_BF16)
        wo_ref[...] = wo_f32[...].astype(_BF16)
        for h in range(N_HEADS):
            gq_ref[:, h * HEAD_DIM:(h + 1) * HEAD_DIM] = gq_head[...]
        for h in range(N_KV_HEADS):
            gk_ref[:, h * HEAD_DIM:(h + 1) * HEAD_DIM] = gk_head[...]
        kx_ref[ts - BLK:ts, :] = jnp.zeros((BLK, KX_WIDTH), _BF16)
        vx_ref[ts - BLK:ts, :] = jnp.zeros((BLK, VX_WIDTH), _BF16)

    @pl.when(seq_start)
    def _():
        zbuf[0:HALO, :] = jnp.zeros((HALO, CONV_WIDTH), _F32)

    kb0_ref[0:BLK, :] = kx_ref[ts - BLK:ts, :]
    vb0_ref[0:BLK, :] = vx_ref[ts - BLK:ts, :]

    u = _rms(x_ref[...], g_ref[...]).astype(_BF16)
    c0 = 3 * CONV_WIDTH
    c1 = c0 + ATTN_WIDTH
    q = _dot(u, w_ref[:, c0:c0 + ATTN_WIDTH])
    kv = _dot(u, w_ref[:, c1:c1 + 2 * KV_WIDTH])
    k = kv[:, 0:KV_WIDTH]
    msq = _dot((q * q).astype(_BF16), hm_ref[...])
    msk = _dot((k * k).astype(_BF16), hm_ref[0:KV_WIDTH, 0:KV_WIDTH])

    q_ref[...] = (q * lax.rsqrt(msq + EPS) * gq_ref[...] * (HEAD_DIM ** -0.5 * LOG2E)).astype(_BF16)
    kn = k * lax.rsqrt(msk + EPS) * gk_ref[...]
    lo = lax.broadcasted_iota(jnp.int32, (ts, LANES), 1) < HEAD_DIM
    kx_ref[...] = jnp.concatenate(_lo_hi_layout(kn, lo), axis=1).astype(_BF16)
    ones = [jnp.where(lo, 1.0, 0.0).astype(_F32), jnp.where(lo, 0.0, 1.0).astype(_F32)]
    v_pieces = _lo_hi_layout(kv[:, KV_WIDTH:2 * KV_WIDTH], lo)
    vx_ref[...] = jnp.concatenate([piece for s, v in enumerate(v_pieces) for piece in (v, ones[s % 2])],
                                  axis=1).astype(_BF16)
    kb0_ref[BLK:2 * BLK, :] = kx_ref[0:BLK, :]
    vb0_ref[BLK:2 * BLK, :] = vx_ref[0:BLK, :]

    lane_lo = lax.broadcasted_iota(jnp.int32, (BLK, LANES), 1) < HEAD_DIM

    def band(tile_ref, band0_ref, j, cols):
        if j == 0:
            return band0_ref[:, cols]
        return tile_ref[(j - 1) * BLK:(j + 1) * BLK, cols]

    def scores_stage(j):
        r0 = j * BLK
        slot = j % (LOOKAHEAD + 1)
        first = seq_start.astype(jnp.int32) if j == 0 else 0
        for g in range(N_KV_HEADS):
            qq = q_ref[pl.ds(r0, BLK), g * 2 * LANES:(g + 1) * 2 * LANES]
            lhs = jnp.concatenate([qq[:, 0:LANES], qq[:, LANES:2 * LANES]], axis=0)
            sink_terms = [[None, None], [None, None]]
            for par in range(2):
                kb = band(kx_ref, kb0_ref, j, slice((2 * g + par) * LANES, (2 * g + par + 1) * LANES))
                logits = _dot_nt(lhs, kb)
                for pair in range(2):
                    h = g * GQA_GROUP + 2 * pair + par
                    lg = logits[pair * BLK:(pair + 1) * BLK, :] + bias_ref[first, h]
                    sink = sink_ref[0, h] * LOG2E
                    m = jnp.maximum(jnp.max(lg, axis=-1, keepdims=True), sink)
                    p_ref[slot, 2 * g + par, pair * BLK:(pair + 1) * BLK, :] = jnp.exp2(lg - m).astype(_BF16)
                    sink_terms[pair][par] = jnp.exp2(sink - m)
            for pair in range(2):
                st_ref[slot, 2 * g + pair] = jnp.where(lane_lo, sink_terms[pair][0], sink_terms[pair][1])

    def values_stage(j):
        r0 = j * BLK
        slot = j % (LOOKAHEAD + 1)
        pairs = []
        for g in range(N_KV_HEADS):
            out = None
            for par in range(2):
                c = (4 * g + 2 * par) * LANES
                o = _dot(p_ref[slot, 2 * g + par],
                         band(vx_ref, vb0_ref, j, slice(c, c + 2 * LANES)))
                out = o if out is None else out + o
            for pair in range(2):
                blk = out[pair * BLK:(pair + 1) * BLK, :]
                den = blk[:, LANES:2 * LANES] + st_ref[slot, 2 * g + pair]
                pairs.append(blk[:, 0:LANES] / den)
        ya = jnp.concatenate(pairs, axis=1)
        yan_ref[r0:r0 + BLK, :] = _rms(ya, ga_ref[...]).astype(_BF16)

    def conv_piece(piece):
        half = piece % 2
        cols = slice(half * MXU_TILE, (half + 1) * MXU_TILE)
        w_cols = lambda base: w_ref[:, base + half * MXU_TILE:base + (half + 1) * MXU_TILE]
        if piece < 2:
            gatec_ref[:, cols] = _dot(u, w_cols(CONV_WIDTH))
        elif piece < 4:
            zbuf[HALO:HALO + ts, cols] = gatec_ref[:, cols] * _dot(u, w_cols(2 * CONV_WIDTH))
        else:
            gateb_ref[:, cols] = _dot(u, w_cols(0))

    piece_after_block = {0: 1, 1: 2, 3: 3, 5: 4, 6: 5}
    conv_piece(0)
    for j in range(LOOKAHEAD):
        scores_stage(j)
    for j in range(nblk):
        if j + LOOKAHEAD < nblk:
            scores_stage(j + LOOKAHEAD)
        if j in piece_after_block:
            conv_piece(piece_after_block[j])
        values_stage(j)

    tap = lambda k: cw_ref[:, k * CONV_WIDTH:(k + 1) * CONV_WIDTH]
    conv = (tap(0) * zbuf[HALO - 2:HALO - 2 + ts, :]
            + tap(1) * zbuf[HALO - 1:HALO - 1 + ts, :]
            + tap(2) * zbuf[HALO:HALO + ts, :])
    yc = _rms(gateb_ref[...] * conv, gc_ref[...]).astype(_BF16)
    zbuf[0:HALO, :] = zbuf[ts:ts + HALO, :]

    h = (x_ref[...] + _dot(yc, wo_ref[0:CONV_WIDTH, :])
         + _dot(yan_ref[...], wo_ref[CONV_WIDTH:CONV_WIDTH + ATTN_WIDTH, :]))
    _store_interleaved(o_ref, h)


def _mix_attn(tab, sinks, x2, g_mix, w_in, conv_w, gq, gk, g_conv, hm, bkt, g_attn, w_out, w_up, w_down,
              tiles_per_seq):
    n = x2.shape[0]
    ts = ROW_TILE
    steps = n // ts
    assert ts // BLK >= 8 and CONV_WIDTH == 2 * MXU_TILE
    smem = pl.BlockSpec(memory_space=pltpu.SMEM)

    def slice_spec(w):
        rows, cols = w.shape
        assert rows % (steps * 2 * SUBLANES) == 0
        return pl.BlockSpec((rows // steps, cols), lambda i: (i, 0))

    cast_weights = (w_up, w_down)
    return pl.pallas_call(
        functools.partial(_mix_attn_kernel, tiles_per_seq),
        grid=(steps,),
        in_specs=[
            smem,
            smem,
            pl.BlockSpec((ts, D_MODEL), lambda i: (i, 0)),
            _const_spec((1, D_MODEL)),
            _const_spec((D_MODEL, IN_WIDTH)),
            _const_spec((1, 3 * CONV_WIDTH)),
            _const_spec((1, HEAD_DIM)),
            _const_spec((1, HEAD_DIM)),
            _const_spec((1, CONV_WIDTH)),
            _const_spec((ATTN_WIDTH, ATTN_WIDTH)),
            _const_spec((BLK, 2 * BLK)),
            _const_spec((1, ATTN_WIDTH)),
            _const_spec((CONV_WIDTH + ATTN_WIDTH, D_MODEL)),
        ] + [slice_spec(w) for w in cast_weights],
        out_specs=[pl.BlockSpec((D_MODEL // LANES, ts, LANES), lambda i: (0, i, 0))]
        + [slice_spec(w) for w in cast_weights],
        out_shape=[jax.ShapeDtypeStruct((D_MODEL // LANES, n, LANES), _F32)]
        + [jax.ShapeDtypeStruct(w.shape, _BF16) for w in cast_weights],
        scratch_shapes=[
            pltpu.VMEM((2, N_HEADS, BLK, 2 * BLK), _F32),
            pltpu.VMEM((D_MODEL, IN_WIDTH), _BF16),
            pltpu.VMEM((CONV_WIDTH + ATTN_WIDTH, D_MODEL), _BF16),
            pltpu.VMEM((1, ATTN_WIDTH), _F32),
            pltpu.VMEM((1, KV_WIDTH), _F32),
            pltpu.VMEM((ts, ATTN_WIDTH), _BF16),
            pltpu.VMEM((ts, KX_WIDTH), _BF16),
            pltpu.VMEM((ts, VX_WIDTH), _BF16),
            pltpu.VMEM((2 * BLK, KX_WIDTH), _BF16),
            pltpu.VMEM((2 * BLK, VX_WIDTH), _BF16),
            pltpu.VMEM((ts, CONV_WIDTH), _F32),
            pltpu.VMEM((ts, CONV_WIDTH), _F32),
            pltpu.VMEM((ts + HALO, CONV_WIDTH), _F32),
            pltpu.VMEM((ts, ATTN_WIDTH), _BF16),
            pltpu.VMEM((LOOKAHEAD + 1, 2 * N_KV_HEADS, 2 * BLK, 2 * BLK), _BF16),
            pltpu.VMEM((LOOKAHEAD + 1, 2 * N_KV_HEADS, BLK, LANES), _F32),
        ],
        compiler_params=pltpu.CompilerParams(
            dimension_semantics=("arbitrary",), vmem_limit_bytes=VMEM_LIMIT_BYTES),
        name="mix_attn",
    )(tab, sinks, x2, g_mix, w_in, conv_w, gq, gk, g_conv, hm, bkt, g_attn, w_out, *cast_weights)


def _ffn_kernel(tiles_per_seq, h_ref, g_ref, wu_ref, cw_ref, cb_ref, wd_ref, o_ref,
                u_ref, ubuf, carry_ref, act_ref, res_ref):
    ts = o_ref.shape[0]
    i = pl.program_id(0)
    g2 = 2 * SUBLANES

    @pl.when(i % tiles_per_seq == 0)
    def _():
        carry_ref[...] = jnp.zeros(carry_ref.shape, _F32)

    h = jnp.concatenate([h_ref[l] for l in range(D_MODEL // LANES)], axis=1)
    first_sublane = lax.broadcasted_iota(jnp.int32, (SUBLANES, FF_CHUNK), 0) == 0
    head_rows = [slice(p * ts // HEAD_PIECES, (p + 1) * ts // HEAD_PIECES) for p in range(HEAD_PIECES)]

    def up_half(c, part, r):
        buf = ubuf.at[(2 * c + part) % ubuf.shape[0]]
        cols = slice(part * D_FF + c * FF_CHUNK, part * D_FF + (c + 1) * FF_CHUNK)
        buf[g2 + r.start:g2 + r.stop, :] = _dot(u_ref[r, :], wu_ref[:, cols])

    def conv_half(c, part):
        slot = 2 * c + part
        buf = ubuf.at[slot % ubuf.shape[0]]
        cols = slice(part * D_FF + c * FF_CHUNK, part * D_FF + (c + 1) * FF_CHUNK)
        for k in range(2):
            cur = pltpu.roll(buf[ts + k * SUBLANES:ts + (k + 1) * SUBLANES, :], 1, axis=0)
            prev = pltpu.roll(carry_ref[slot, k * SUBLANES:(k + 1) * SUBLANES, :], 1, axis=0)
            buf[k * SUBLANES:(k + 1) * SUBLANES, :] = jnp.where(first_sublane, prev, cur)
        tap = lambda k: cw_ref[:, k * 2 * D_FF + cols.start:k * 2 * D_FF + cols.stop]
        a = (tap(0) * buf[0:ts, :]
             + tap(1) * buf[SUBLANES:SUBLANES + ts, :]
             + tap(2) * buf[g2:g2 + ts, :] + cb_ref[:, cols])
        carry_ref[slot] = buf[ts:ts + g2, :]
        return a

    for r in head_rows:
        u_ref[r, :] = _rms(h[r, :], g_ref[...]).astype(_BF16)
    for r in head_rows:
        up_half(0, 0, r)
        up_half(0, 1, r)

    for c in range(N_FF_CHUNKS):
        if c > 0:
            up_half(c, 0, slice(0, ts))
        gate = conv_half(c, 0)
        if c > 0:
            up_half(c, 1, slice(0, ts))
        val = conv_half(c, 1)
        act_ref[:, c * FF_CHUNK:(c + 1) * FF_CHUNK] = (gate / (1.0 + jnp.exp(-gate)) * val).astype(_BF16)

    res = h + _dot(act_ref[...], wd_ref[...])
    for l in range(D_MODEL // LANES):
        res_ref[l] = res[:, l * LANES:(l + 1) * LANES]
    _load_deinterleaved(res_ref, o_ref)


def _ffn(h, g_ffn, w_up, conv_w, conv_b, w_down, tiles_per_seq):
    n = h.shape[1]
    ts = ROW_TILE
    g2 = 2 * SUBLANES
    return pl.pallas_call(
        functools.partial(_ffn_kernel, tiles_per_seq),
        grid=(n // ts,),
        in_specs=[
            pl.BlockSpec((D_MODEL // LANES, ts, LANES), lambda i: (0, i, 0)),
            _const_spec((1, D_MODEL)),
            _const_spec((D_MODEL, 2 * D_FF)),
            _const_spec((1, 3 * 2 * D_FF)),
            _const_spec((1, 2 * D_FF)),
            _const_spec((D_FF, D_MODEL)),
        ],
        out_specs=pl.BlockSpec((ts, D_MODEL), lambda i: (i, 0)),
        out_shape=jax.ShapeDtypeStruct((n, D_MODEL), _F32),
        scratch_shapes=[
            pltpu.VMEM((ts, D_MODEL), _BF16),
            pltpu.VMEM((4, ts + g2, FF_CHUNK), _F32),
            pltpu.VMEM((2 * N_FF_CHUNKS, g2, FF_CHUNK), _F32),
            pltpu.VMEM((ts, D_FF), _BF16),
            pltpu.VMEM((D_MODEL // LANES, ts, LANES), _F32),
        ],
        compiler_params=pltpu.CompilerParams(
            dimension_semantics=("arbitrary",), vmem_limit_bytes=VMEM_LIMIT_BYTES),
        name="ffn",
    )(h, g_ffn, w_up, conv_w, conv_b, w_down)


def kernel(x, norm_mix_g, w_in, conv_w, q_norm_g, k_norm_g, rel_bias_table, sinks, out_norm_conv_g,
           out_norm_attn_g, w_out, norm_ffn_g, w_up, ffn_conv_w, ffn_conv_b, w_down):
    bsz, seq, d = x.shape
    assert d == D_MODEL and seq % ROW_TILE == 0 and norm_mix_g.shape[0] == 1
    tiles_per_seq = seq // ROW_TILE
    n = bsz * seq
    x2 = x.reshape(n, d)
    bkt = jnp.asarray(_bucket_map())
    hm = _head_mean_matrix()

    h, w_up_b, w_down_b = _mix_attn(
        rel_bias_table.T, sinks, x2, norm_mix_g, w_in[0], conv_w.reshape(1, -1), q_norm_g, k_norm_g,
        out_norm_conv_g, hm, bkt, out_norm_attn_g, w_out[0],
        w_up[0], w_down[0], tiles_per_seq)
    out = _ffn(h, norm_ffn_g, w_up_b, ffn_conv_w.reshape(1, -1), ffn_conv_b, w_down_b, tiles_per_seq)
    return out.reshape(bsz, seq, d)
```

```python
import functools
import math

import jax
import jax.numpy as jnp
import numpy as np
from jax import lax
from jax.experimental import pallas as pl
from jax.experimental.pallas import tpu as pltpu

D_MODEL = 1024
CONV_WIDTH = 512
HEAD_DIM = 64
N_HEADS = 8
N_KV_HEADS = 2
GQA_GROUP = N_HEADS // N_KV_HEADS
ATTN_WIDTH = N_HEADS * HEAD_DIM
KV_WIDTH = N_KV_HEADS * HEAD_DIM
WINDOW = 128
BLK = 128
NUM_BUCKETS = 32
MAX_DISTANCE = 128
MAX_EXACT = NUM_BUCKETS // 2
D_FF = 2816
EPS = 1e-6
NEG_INF = -1e30
LOG2E = math.log2(math.e)
IN_WIDTH = 3 * CONV_WIDTH + ATTN_WIDTH + 2 * KV_WIDTH

LANES = 128
SUBLANES = 8
MXU_TILE = 256
PAIR = 2 * HEAD_DIM
assert PAIR == LANES and KV_WIDTH == LANES and GQA_GROUP == 4
KX_WIDTH = 2 * N_KV_HEADS * LANES
VX_WIDTH = 2 * KX_WIDTH

ROW_TILE = 1024
FF_CHUNK = MXU_TILE
N_FF_CHUNKS = D_FF // FF_CHUNK
HEAD_PIECES = 4
HALO = 8
LOOKAHEAD = 1
VMEM_LIMIT_BYTES = 60 * 1024 * 1024

_BF16 = jnp.bfloat16
_F32 = jnp.float32


def _dot(a, b):
    return jnp.dot(a, b, preferred_element_type=_F32)


def _dot_nt(a, b):
    return lax.dot_general(a, b, (((1,), (1,)), ((), ())), preferred_element_type=_F32)


def _rms(xf, g):
    return xf * lax.rsqrt(jnp.mean(xf * xf, axis=-1, keepdims=True) + EPS) * g


def _bucket_map():
    q = np.arange(BLK, dtype=np.int32)[:, None]
    j = np.arange(2 * BLK, dtype=np.int32)[None, :]
    d = q + BLK - j
    n = np.maximum(d, 0)
    nf = np.maximum(n, 1).astype(np.float32)
    large = MAX_EXACT + (np.log(nf / MAX_EXACT) / math.log(MAX_DISTANCE / MAX_EXACT)
                         * (NUM_BUCKETS - MAX_EXACT)).astype(np.int32)
    large = np.minimum(large, NUM_BUCKETS - 1)
    bucket = np.where(n < MAX_EXACT, n, large).astype(np.int32)
    within = (d >= 0) & (d < WINDOW)
    return np.where(within, bucket, -1).astype(np.int32)


def _head_mean_matrix():
    i = np.arange(ATTN_WIDTH)
    m = (i[:, None] // HEAD_DIM == i[None, :] // HEAD_DIM).astype(np.float32) / HEAD_DIM
    return jnp.asarray(m, dtype=_BF16)


def _const_spec(shape):
    return pl.BlockSpec(shape, lambda i: (0,) * len(shape), pipeline_mode=pl.Buffered(1))


def _store_interleaved(slab_ref, val):
    ts = val.shape[0]
    seg = ts // SUBLANES
    for l in range(D_MODEL // LANES):
        for s in range(SUBLANES):
            slab_ref[l, pl.ds(s, seg, stride=SUBLANES), :] = val[s * seg:(s + 1) * seg, l * LANES:(l + 1) * LANES]


def _load_deinterleaved(slab_ref, out_ref):
    ts = out_ref.shape[0]
    seg = ts // SUBLANES
    for l in range(D_MODEL // LANES):
        for s in range(SUBLANES):
            out_ref[s * seg:(s + 1) * seg, l * LANES:(l + 1) * LANES] = slab_ref[l, pl.ds(s, seg, stride=SUBLANES), :]


def _lo_hi_layout(a, lo):
    ar = pltpu.roll(a, HEAD_DIM, axis=1)
    zero = jnp.zeros_like(a)
    return [jnp.where(lo, a, zero), jnp.where(lo, zero, ar), jnp.where(lo, ar, zero), jnp.where(lo, zero, a)]


def _mix_attn_kernel(tiles_per_seq, tab_ref, sink_ref, x_ref, g_ref, w_f32, cw_ref, gq_head, gk_head, gc_ref,
                     hm_ref, bkt_ref, ga_ref, wo_f32, wu_f32, wd_f32,
                     o_ref, wu_bf16, wd_bf16,
                     bias_ref, w_ref, wo_ref, gq_ref, gk_ref, q_ref, kx_ref, vx_ref, kb0_ref, vb0_ref, gatec_ref, gateb_ref, zbuf,
                     yan_ref, p_ref, st_ref):
    ts = x_ref.shape[0]
    nblk = ts // BLK
    i = pl.program_id(0)
    seq_start = (i % tiles_per_seq == 0)

    wu_bf16[...] = wu_f32[...].astype(_BF16)
    wd_bf16[...] = wd_f32[...].astype(_BF16)

    @pl.when(i == 0)
    def _():
        bkt = bkt_ref[...]
        col = lax.broadcasted_iota(jnp.int32, (BLK, 2 * BLK), 1)

        def head_body(h, carry):
            b = jnp.full((BLK, 2 * BLK), NEG_INF, _F32)
            for t in range(NUM_BUCKETS):
                b = jnp.where(bkt == t, tab_ref[h, t] * LOG2E, b)
            bias_ref[0, h] = b
            bias_ref[1, h] = jnp.where(col >= BLK, b, NEG_INF)
            return carry

        lax.fori_loop(0, N_HEADS, head_body, 0)
        w_ref[...] = w_f32[...].astype(_BF16)
        wo_ref[...] = wo_f32[...].astype(_BF16)
        for h in range(N_HEADS):
            gq_ref[:, h * HEAD_DIM:(h + 1) * HEAD_DIM] = gq_head[...]
        for h in range(N_KV_HEADS):
            gk_ref[:, h * HEAD_DIM:(h + 1) * HEAD_DIM] = gk_head[...]
        kx_ref[ts - BLK:ts, :] = jnp.zeros((BLK, KX_WIDTH), _BF16)
        vx_ref[ts - BLK:ts, :] = jnp.zeros((BLK, VX_WIDTH), _BF16)

    @pl.when(seq_start)
    def _():
        zbuf[0:HALO, :] = jnp.zeros((HALO, CONV_WIDTH), _F32)

    kb0_ref[0:BLK, :] = kx_ref[ts - BLK:ts, :]
    vb0_ref[0:BLK, :] = vx_ref[ts - BLK:ts, :]

    u = _rms(x_ref[...], g_ref[...]).astype(_BF16)
    c0 = 3 * CONV_WIDTH
    c1 = c0 + ATTN_WIDTH
    q = _dot(u, w_ref[:, c0:c0 + ATTN_WIDTH])
    kv = _dot(u, w_ref[:, c1:c1 + 2 * KV_WIDTH])
    k = kv[:, 0:KV_WIDTH]
    msq = _dot((q * q).astype(_BF16), hm_ref[...])
    msk = _dot((k * k).astype(_BF16), hm_ref[0:KV_WIDTH, 0:KV_WIDTH])

    q_ref[...] = (q * lax.rsqrt(msq + EPS) * gq_ref[...] * (HEAD_DIM ** -0.5 * LOG2E)).astype(_BF16)
    kn = k * lax.rsqrt(msk + EPS) * gk_ref[...]
    lo = lax.broadcasted_iota(jnp.int32, (ts, LANES), 1) < HEAD_DIM
    kx_ref[...] = jnp.concatenate(_lo_hi_layout(kn, lo), axis=1).astype(_BF16)
    ones = [jnp.where(lo, 1.0, 0.0).astype(_F32), jnp.where(lo, 0.0, 1.0).astype(_F32)]
    v_pieces = _lo_hi_layout(kv[:, KV_WIDTH:2 * KV_WIDTH], lo)
    vx_ref[...] = jnp.concatenate([piece for s, v in enumerate(v_pieces) for piece in (v, ones[s % 2])],
                                  axis=1).astype(_BF16)
    kb0_ref[BLK:2 * BLK, :] = kx_ref[0:BLK, :]
    vb0_ref[BLK:2 * BLK, :] = vx_ref[0:BLK, :]

    lane_lo = lax.broadcasted_iota(jnp.int32, (BLK, LANES), 1) < HEAD_DIM

    def band(tile_ref, band0_ref, j, cols):
        if j == 0:
            return band0_ref[:, cols]
        return tile_ref[(j - 1) * BLK:(j + 1) * BLK, cols]

    def scores_stage(j):
        r0 = j * BLK
        slot = j % (LOOKAHEAD + 1)
        first = seq_start.astype(jnp.int32) if j == 0 else 0
        for g in range(N_KV_HEADS):
            qq = q_ref[pl.ds(r0, BLK), g * 2 * LANES:(g + 1) * 2 * LANES]
            lhs = jnp.concatenate([qq[:, 0:LANES], qq[:, LANES:2 * LANES]], axis=0)
            sink_terms = [[None, None], [None, None]]
            for par in range(2):
                kb = band(kx_ref, kb0_ref, j, slice((2 * g + par) * LANES, (2 * g + par + 1) * LANES))
                logits = _dot_nt(lhs, kb)
                for pair in range(2):
                    h = g * GQA_GROUP + 2 * pair + par
                    lg = logits[pair * BLK:(pair + 1) * BLK, :] + bias_ref[first, h]
                    sink = sink_ref[0, h] * LOG2E
                    m = jnp.maximum(jnp.max(lg, axis=-1, keepdims=True), sink)
                    p_ref[slot, 2 * g + par, pair * BLK:(pair + 1) * BLK, :] = jnp.exp2(lg - m).astype(_BF16)
                    sink_terms[pair][par] = jnp.exp2(sink - m)
            for pair in range(2):
                st_ref[slot, 2 * g + pair] = jnp.where(lane_lo, sink_terms[pair][0], sink_terms[pair][1])

    def values_stage(j):
        r0 = j * BLK
        slot = j % (LOOKAHEAD + 1)
        pairs = []
        for g in range(N_KV_HEADS):
            out = None
            for par in range(2):
                c = (4 * g + 2 * par) * LANES
                o = _dot(p_ref[slot, 2 * g + par],
                         band(vx_ref, vb0_ref, j, slice(c, c + 2 * LANES)))
                out = o if out is None else out + o
            for pair in range(2):
                blk = out[pair * BLK:(pair + 1) * BLK, :]
                den = blk[:, LANES:2 * LANES] + st_ref[slot, 2 * g + pair]
                pairs.append(blk[:, 0:LANES] / den)
        ya = jnp.concatenate(pairs, axis=1)
        yan_ref[r0:r0 + BLK, :] = _rms(ya, ga_ref[...]).astype(_BF16)

    def conv_piece(piece):
        half = piece % 2
        cols = slice(half * MXU_TILE, (half + 1) * MXU_TILE)
        w_cols = lambda base: w_ref[:, base + half * MXU_TILE:base + (half + 1) * MXU_TILE]
        if piece < 2:
            gatec_ref[:, cols] = _dot(u, w_cols(CONV_WIDTH))
        elif piece < 4:
            zbuf[HALO:HALO + ts, cols] = gatec_ref[:, cols] * _dot(u, w_cols(2 * CONV_WIDTH))
        else:
            gateb_ref[:, cols] = _dot(u, w_cols(0))

    piece_after_block = {0: 1, 1: 2, 3: 3, 5: 4, 6: 5}
    conv_piece(0)
    for j in range(LOOKAHEAD):
        scores_stage(j)
    for j in range(nblk):
        if j + LOOKAHEAD < nblk:
            scores_stage(j + LOOKAHEAD)
        if j in piece_after_block:
            conv_piece(piece_after_block[j])
        values_stage(j)

    tap = lambda k: cw_ref[:, k * CONV_WIDTH:(k + 1) * CONV_WIDTH]
    conv = (tap(0) * zbuf[HALO - 2:HALO - 2 + ts, :]
            + tap(1) * zbuf[HALO - 1:HALO - 1 + ts, :]
            + tap(2) * zbuf[HALO:HALO + ts, :])
    yc = _rms(gateb_ref[...] * conv, gc_ref[...]).astype(_BF16)
    zbuf[0:HALO, :] = zbuf[ts:ts + HALO, :]

    h = (x_ref[...] + _dot(yc, wo_ref[0:CONV_WIDTH, :])
         + _dot(yan_ref[...], wo_ref[CONV_WIDTH:CONV_WIDTH + ATTN_WIDTH, :]))
    _store_interleaved(o_ref, h)


def _mix_attn(tab, sinks, x2, g_mix, w_in, conv_w, gq, gk, g_conv, hm, bkt, g_attn, w_out, w_up, w_down,
              tiles_per_seq):
    n = x2.shape[0]
    ts = ROW_TILE
    steps = n // ts
    assert ts // BLK >= 8 and CONV_WIDTH == 2 * MXU_TILE
    smem = pl.BlockSpec(memory_space=pltpu.SMEM)

    def slice_spec(w):
        rows, cols = w.shape
        assert rows % (steps * 2 * SUBLANES) == 0
        return pl.BlockSpec((rows // steps, cols), lambda i: (i, 0))

    cast_weights = (w_up, w_down)
    return pl.pallas_call(
        functools.partial(_mix_attn_kernel, tiles_per_seq),
        grid=(steps,),
        in_specs=[
            smem,
            smem,
            pl.BlockSpec((ts, D_MODEL), lambda i: (i, 0)),
            _const_spec((1, D_MODEL)),
            _const_spec((D_MODEL, IN_WIDTH)),
            _const_spec((1, 3 * CONV_WIDTH)),
            _const_spec((1, HEAD_DIM)),
            _const_spec((1, HEAD_DIM)),
            _const_spec((1, CONV_WIDTH)),
            _const_spec((ATTN_WIDTH, ATTN_WIDTH)),
            _const_spec((BLK, 2 * BLK)),
            _const_spec((1, ATTN_WIDTH)),
            _const_spec((CONV_WIDTH + ATTN_WIDTH, D_MODEL)),
        ] + [slice_spec(w) for w in cast_weights],
        out_specs=[pl.BlockSpec((D_MODEL // LANES, ts, LANES), lambda i: (0, i, 0))]
        + [slice_spec(w) for w in cast_weights],
        out_shape=[jax.ShapeDtypeStruct((D_MODEL // LANES, n, LANES), _F32)]
        + [jax.ShapeDtypeStruct(w.shape, _BF16) for w in cast_weights],
        scratch_shapes=[
            pltpu.VMEM((2, N_HEADS, BLK, 2 * BLK), _F32),
            pltpu.VMEM((D_MODEL, IN_WIDTH), _BF16),
            pltpu.VMEM((CONV_WIDTH + ATTN_WIDTH, D_MODEL), _BF16),
            pltpu.VMEM((1, ATTN_WIDTH), _F32),
            pltpu.VMEM((1, KV_WIDTH), _F32),
            pltpu.VMEM((ts, ATTN_WIDTH), _BF16),
            pltpu.VMEM((ts, KX_WIDTH), _BF16),
            pltpu.VMEM((ts, VX_WIDTH), _BF16),
            pltpu.VMEM((2 * BLK, KX_WIDTH), _BF16),
            pltpu.VMEM((2 * BLK, VX_WIDTH), _BF16),
            pltpu.VMEM((ts, CONV_WIDTH), _F32),
            pltpu.VMEM((ts, CONV_WIDTH), _F32),
            pltpu.VMEM((ts + HALO, CONV_WIDTH), _F32),
            pltpu.VMEM((ts, ATTN_WIDTH), _BF16),
            pltpu.VMEM((LOOKAHEAD + 1, 2 * N_KV_HEADS, 2 * BLK, 2 * BLK), _BF16),
            pltpu.VMEM((LOOKAHEAD + 1, 2 * N_KV_HEADS, BLK, LANES), _F32),
        ],
        compiler_params=pltpu.CompilerParams(
            dimension_semantics=("arbitrary",), vmem_limit_bytes=VMEM_LIMIT_BYTES),
        name="mix_attn",
    )(tab, sinks, x2, g_mix, w_in, conv_w, gq, gk, g_conv, hm, bkt, g_attn, w_out, *cast_weights)


def _ffn_kernel(tiles_per_seq, h_ref, g_ref, wu_ref, cw_ref, cb_ref, wd_hbm, o_ref,
                u_ref, ubuf, carry_ref, act_ref, res_ref, wd_ref, wd_sem):
    i = pl.program_id(0)

    @pl.when(i % tiles_per_seq == 0)
    def _():
        carry_ref[...] = jnp.zeros(carry_ref.shape, _F32)

    args = (h_ref, g_ref, wu_ref, cw_ref, cb_ref, wd_ref, o_ref, u_ref, ubuf, carry_ref, act_ref, res_ref)
    wd_copy = pltpu.make_async_copy(wd_hbm, wd_ref, wd_sem.at[0])

    @pl.when(i == 0)
    def _():
        wd_copy.start()
        _ffn_step(*args, before_down=wd_copy.wait)

    @pl.when(i > 0)
    def _():
        _ffn_step(*args, before_down=lambda: None)


def _ffn_step(h_ref, g_ref, wu_ref, cw_ref, cb_ref, wd_ref, o_ref, u_ref, ubuf, carry_ref, act_ref, res_ref,
              before_down):
    ts = o_ref.shape[0]
    g2 = 2 * SUBLANES

    h = jnp.concatenate([h_ref[l] for l in range(D_MODEL // LANES)], axis=1)
    first_sublane = lax.broadcasted_iota(jnp.int32, (SUBLANES, FF_CHUNK), 0) == 0
    head_rows = [slice(p * ts // HEAD_PIECES, (p + 1) * ts // HEAD_PIECES) for p in range(HEAD_PIECES)]

    def up_half(c, part, r):
        buf = ubuf.at[(2 * c + part) % ubuf.shape[0]]
        cols = slice(part * D_FF + c * FF_CHUNK, part * D_FF + (c + 1) * FF_CHUNK)
        buf[g2 + r.start:g2 + r.stop, :] = _dot(u_ref[r, :], wu_ref[:, cols])

    def conv_half(c, part):
        slot = 2 * c + part
        buf = ubuf.at[slot % ubuf.shape[0]]
        cols = slice(part * D_FF + c * FF_CHUNK, part * D_FF + (c + 1) * FF_CHUNK)
        for k in range(2):
            cur = pltpu.roll(buf[ts + k * SUBLANES:ts + (k + 1) * SUBLANES, :], 1, axis=0)
            prev = pltpu.roll(carry_ref[slot, k * SUBLANES:(k + 1) * SUBLANES, :], 1, axis=0)
            buf[k * SUBLANES:(k + 1) * SUBLANES, :] = jnp.where(first_sublane, prev, cur)
        tap = lambda k: cw_ref[:, k * 2 * D_FF + cols.start:k * 2 * D_FF + cols.stop]
        a = (tap(0) * buf[0:ts, :]
             + tap(1) * buf[SUBLANES:SUBLANES + ts, :]
             + tap(2) * buf[g2:g2 + ts, :] + cb_ref[:, cols])
        carry_ref[slot] = buf[ts:ts + g2, :]
        return a

    for r in head_rows:
        u_ref[r, :] = _rms(h[r, :], g_ref[...]).astype(_BF16)
    for r in head_rows:
        up_half(0, 0, r)
        up_half(0, 1, r)

    for c in range(N_FF_CHUNKS):
        if c > 0:
            up_half(c, 0, slice(0, ts))
        gate = conv_half(c, 0)
        if c > 0:
            up_half(c, 1, slice(0, ts))
        val = conv_half(c, 1)
        act_ref[:, c * FF_CHUNK:(c + 1) * FF_CHUNK] = (gate / (1.0 + jnp.exp(-gate)) * val).astype(_BF16)

    before_down()
    res = h + _dot(act_ref[...], wd_ref[...])
    for l in range(D_MODEL // LANES):
        res_ref[l] = res[:, l * LANES:(l + 1) * LANES]
    _load_deinterleaved(res_ref, o_ref)


def _ffn(h, g_ffn, w_up, conv_w, conv_b, w_down, tiles_per_seq):
    n = h.shape[1]
    ts = ROW_TILE
    g2 = 2 * SUBLANES
    return pl.pallas_call(
        functools.partial(_ffn_kernel, tiles_per_seq),
        grid=(n // ts,),
        in_specs=[
            pl.BlockSpec((D_MODEL // LANES, ts, LANES), lambda i: (0, i, 0)),
            _const_spec((1, D_MODEL)),
            _const_spec((D_MODEL, 2 * D_FF)),
            _const_spec((1, 3 * 2 * D_FF)),
            _const_spec((1, 2 * D_FF)),
            pl.BlockSpec(memory_space=pl.ANY),
        ],
        out_specs=pl.BlockSpec((ts, D_MODEL), lambda i: (i, 0)),
        out_shape=jax.ShapeDtypeStruct((n, D_MODEL), _F32),
        scratch_shapes=[
            pltpu.VMEM((ts, D_MODEL), _BF16),
            pltpu.VMEM((4, ts + g2, FF_CHUNK), _F32),
            pltpu.VMEM((2 * N_FF_CHUNKS, g2, FF_CHUNK), _F32),
            pltpu.VMEM((ts, D_FF), _BF16),
            pltpu.VMEM((D_MODEL // LANES, ts, LANES), _F32),
            pltpu.VMEM((D_FF, D_MODEL), _BF16),
            pltpu.SemaphoreType.DMA((1,)),
        ],
        compiler_params=pltpu.CompilerParams(
            dimension_semantics=("arbitrary",), vmem_limit_bytes=VMEM_LIMIT_BYTES),
        name="ffn",
    )(h, g_ffn, w_up, conv_w, conv_b, w_down)


def kernel(x, norm_mix_g, w_in, conv_w, q_norm_g, k_norm_g, rel_bias_table, sinks, out_norm_conv_g,
           out_norm_attn_g, w_out, norm_ffn_g, w_up, ffn_conv_w, ffn_conv_b, w_down):
    bsz, seq, d = x.shape
    assert d == D_MODEL and seq % ROW_TILE == 0 and norm_mix_g.shape[0] == 1
    tiles_per_seq = seq // ROW_TILE
    n = bsz * seq
    x2 = x.reshape(n, d)
    bkt = jnp.asarray(_bucket_map())
    hm = _head_mean_matrix()

    h, w_up_b, w_down_b = _mix_attn(
        rel_bias_table.T, sinks, x2, norm_mix_g, w_in[0], conv_w.reshape(1, -1), q_norm_g, k_norm_g,
        out_norm_conv_g, hm, bkt, out_norm_attn_g, w_out[0],
        w_up[0], w_down[0], tiles_per_seq)
    out = _ffn(h, norm_ffn_g, w_up_b, ffn_conv_w.reshape(1, -1), ffn_conv_b, w_down_b, tiles_per_seq)
    return out.reshape(bsz, seq, d)
```

```python
import functools
import math

import jax
import jax.numpy as jnp
import numpy as np
from jax import lax
from jax.experimental import pallas as pl
from jax.experimental.pallas import tpu as pltpu

D_MODEL = 1024
CONV_WIDTH = 512
HEAD_DIM = 64
N_HEADS = 8
N_KV_HEADS = 2
GQA_GROUP = N_HEADS // N_KV_HEADS
ATTN_WIDTH = N_HEADS * HEAD_DIM
KV_WIDTH = N_KV_HEADS * HEAD_DIM
WINDOW = 128
BLK = 128
NUM_BUCKETS = 32
MAX_DISTANCE = 128
MAX_EXACT = NUM_BUCKETS // 2
D_FF = 2816
EPS = 1e-6
NEG_INF = -1e30
LOG2E = math.log2(math.e)
IN_WIDTH = 3 * CONV_WIDTH + ATTN_WIDTH + 2 * KV_WIDTH

LANES = 128
SUBLANES = 8
MXU_TILE = 256
PAIR = 2 * HEAD_DIM
assert PAIR == LANES and KV_WIDTH == LANES and GQA_GROUP == 4
KX_WIDTH = 2 * N_KV_HEADS * LANES
VX_WIDTH = 2 * KX_WIDTH

ROW_TILE = 1024
FF_CHUNK = MXU_TILE
N_FF_CHUNKS = D_FF // FF_CHUNK
HEAD_PIECES = 4
HALO = 8
LOOKAHEAD = 1
VMEM_LIMIT_BYTES = 60 * 1024 * 1024

_BF16 = jnp.bfloat16
_F32 = jnp.float32


def _dot(a, b):
    return jnp.dot(a, b, preferred_element_type=_F32)


def _dot_nt(a, b):
    return lax.dot_general(a, b, (((1,), (1,)), ((), ())), preferred_element_type=_F32)


def _rms(xf, g):
    return xf * lax.rsqrt(jnp.mean(xf * xf, axis=-1, keepdims=True) + EPS) * g


def _bucket_map():
    q = np.arange(BLK, dtype=np.int32)[:, None]
    j = np.arange(2 * BLK, dtype=np.int32)[None, :]
    d = q + BLK - j
    n = np.maximum(d, 0)
    nf = np.maximum(n, 1).astype(np.float32)
    large = MAX_EXACT + (np.log(nf / MAX_EXACT) / math.log(MAX_DISTANCE / MAX_EXACT)
                         * (NUM_BUCKETS - MAX_EXACT)).astype(np.int32)
    large = np.minimum(large, NUM_BUCKETS - 1)
    bucket = np.where(n < MAX_EXACT, n, large).astype(np.int32)
    within = (d >= 0) & (d < WINDOW)
    return np.where(within, bucket, -1).astype(np.int32)


def _head_mean_matrix():
    i = np.arange(ATTN_WIDTH)
    m = (i[:, None] // HEAD_DIM == i[None, :] // HEAD_DIM).astype(np.float32) / HEAD_DIM
    return jnp.asarray(m, dtype=_BF16)


def _const_spec(shape):
    return pl.BlockSpec(shape, lambda i: (0,) * len(shape), pipeline_mode=pl.Buffered(1))


def _store_interleaved(slab_ref, val):
    ts = val.shape[0]
    seg = ts // SUBLANES
    for l in range(D_MODEL // LANES):
        for s in range(SUBLANES):
            slab_ref[l, pl.ds(s, seg, stride=SUBLANES), :] = val[s * seg:(s + 1) * seg, l * LANES:(l + 1) * LANES]


def _load_deinterleaved(slab_ref, out_ref):
    ts = out_ref.shape[0]
    seg = ts // SUBLANES
    for l in range(D_MODEL // LANES):
        for s in range(SUBLANES):
            out_ref[s * seg:(s + 1) * seg, l * LANES:(l + 1) * LANES] = slab_ref[l, pl.ds(s, seg, stride=SUBLANES), :]


def _lo_hi_layout(a, lo):
    ar = pltpu.roll(a, HEAD_DIM, axis=1)
    zero = jnp.zeros_like(a)
    return [jnp.where(lo, a, zero), jnp.where(lo, zero, ar), jnp.where(lo, ar, zero), jnp.where(lo, zero, a)]


def _mix_attn_kernel(tiles_per_seq, tab_ref, sink_ref, x_ref, g_ref, w_f32, cw_ref, gq_head, gk_head, gc_ref,
                     hm_ref, bkt_ref, ga_ref, wo_f32, wu_f32, wd_f32,
                     o_ref, wu_bf16, wd_bf16,
                     bias_ref, w_ref, wo_ref, gq_ref, gk_ref, q_ref, kx_ref, vx_ref, kb0_ref, vb0_ref, gatec_ref, gateb_ref, zbuf,
                     yan_ref, p_ref, st_ref):
    ts = x_ref.shape[0]
    nblk = ts // BLK
    i = pl.program_id(0)
    seq_start = (i % tiles_per_seq == 0)

    wu_bf16[...] = wu_f32[...].astype(_BF16)
    wd_bf16[...] = wd_f32[...].astype(_BF16)

    @pl.when(i == 0)
    def _():
        bkt = bkt_ref[...]
        col = lax.broadcasted_iota(jnp.int32, (BLK, 2 * BLK), 1)

        def head_body(h, carry):
            b = jnp.full((BLK, 2 * BLK), NEG_INF, _F32)
            for t in range(NUM_BUCKETS):
                b = jnp.where(bkt == t, tab_ref[h, t] * LOG2E, b)
            bias_ref[0, h] = b
            bias_ref[1, h] = jnp.where(col >= BLK, b, NEG_INF)
            return carry

        lax.fori_loop(0, N_HEADS, head_body, 0)
        w_ref[...] = w_f32[...].astype(_BF16)
        wo_ref[...] = wo_f32[...].astype(_BF16)
        for h in range(N_HEADS):
            gq_ref[:, h * HEAD_DIM:(h + 1) * HEAD_DIM] = gq_head[...]
        for h in range(N_KV_HEADS):
            gk_ref[:, h * HEAD_DIM:(h + 1) * HEAD_DIM] = gk_head[...]
        kx_ref[ts - BLK:ts, :] = jnp.zeros((BLK, KX_WIDTH), _BF16)
        vx_ref[ts - BLK:ts, :] = jnp.zeros((BLK, VX_WIDTH), _BF16)

    @pl.when(seq_start)
    def _():
        zbuf[0:HALO, :] = jnp.zeros((HALO, CONV_WIDTH), _F32)

    kb0_ref[0:BLK, :] = kx_ref[ts - BLK:ts, :]
    vb0_ref[0:BLK, :] = vx_ref[ts - BLK:ts, :]

    u = _rms(x_ref[...], g_ref[...]).astype(_BF16)
    c0 = 3 * CONV_WIDTH
    c1 = c0 + ATTN_WIDTH
    q = _dot(u, w_ref[:, c0:c0 + ATTN_WIDTH])
    kv = _dot(u, w_ref[:, c1:c1 + 2 * KV_WIDTH])
    k = kv[:, 0:KV_WIDTH]
    msq = _dot((q * q).astype(_BF16), hm_ref[...])
    msk = _dot((k * k).astype(_BF16), hm_ref[0:KV_WIDTH, 0:KV_WIDTH])

    q_ref[...] = (q * lax.rsqrt(msq + EPS) * gq_ref[...] * (HEAD_DIM ** -0.5 * LOG2E)).astype(_BF16)
    kn = k * lax.rsqrt(msk + EPS) * gk_ref[...]
    lo = lax.broadcasted_iota(jnp.int32, (ts, LANES), 1) < HEAD_DIM
    kx_ref[...] = jnp.concatenate(_lo_hi_layout(kn, lo), axis=1).astype(_BF16)
    ones = [jnp.where(lo, 1.0, 0.0).astype(_F32), jnp.where(lo, 0.0, 1.0).astype(_F32)]
    v_pieces = _lo_hi_layout(kv[:, KV_WIDTH:2 * KV_WIDTH], lo)
    vx_ref[...] = jnp.concatenate([piece for s, v in enumerate(v_pieces) for piece in (v, ones[s % 2])],
                                  axis=1).astype(_BF16)
    kb0_ref[BLK:2 * BLK, :] = kx_ref[0:BLK, :]
    vb0_ref[BLK:2 * BLK, :] = vx_ref[0:BLK, :]

    lane_lo = lax.broadcasted_iota(jnp.int32, (BLK, LANES), 1) < HEAD_DIM

    def band(tile_ref, band0_ref, j, cols):
        if j == 0:
            return band0_ref[:, cols]
        return tile_ref[(j - 1) * BLK:(j + 1) * BLK, cols]

    def scores_stage(j):
        r0 = j * BLK
        slot = j % (LOOKAHEAD + 1)
        first = seq_start.astype(jnp.int32) if j == 0 else 0
        for g in range(N_KV_HEADS):
            qq = q_ref[pl.ds(r0, BLK), g * 2 * LANES:(g + 1) * 2 * LANES]
            lhs = jnp.concatenate([qq[:, 0:LANES], qq[:, LANES:2 * LANES]], axis=0)
            sink_terms = [[None, None], [None, None]]
            for par in range(2):
                kb = band(kx_ref, kb0_ref, j, slice((2 * g + par) * LANES, (2 * g + par + 1) * LANES))
                logits = _dot_nt(lhs, kb)
                for pair in range(2):
                    h = g * GQA_GROUP + 2 * pair + par
                    lg = logits[pair * BLK:(pair + 1) * BLK, :] + bias_ref[first, h]
                    sink = sink_ref[0, h] * LOG2E
                    m = jnp.maximum(jnp.max(lg, axis=-1, keepdims=True), sink)
                    p_ref[slot, 2 * g + par, pair * BLK:(pair + 1) * BLK, :] = jnp.exp2(lg - m).astype(_BF16)
                    sink_terms[pair][par] = jnp.exp2(sink - m)
            for pair in range(2):
                st_ref[slot, 2 * g + pair] = jnp.where(lane_lo, sink_terms[pair][0], sink_terms[pair][1])

    def values_stage(j):
        r0 = j * BLK
        slot = j % (LOOKAHEAD + 1)
        pairs = []
        for g in range(N_KV_HEADS):
            out = None
            for par in range(2):
                c = (4 * g + 2 * par) * LANES
                o = _dot(p_ref[slot, 2 * g + par],
                         band(vx_ref, vb0_ref, j, slice(c, c + 2 * LANES)))
                out = o if out is None else out + o
            for pair in range(2):
                blk = out[pair * BLK:(pair + 1) * BLK, :]
                den = blk[:, LANES:2 * LANES] + st_ref[slot, 2 * g + pair]
                pairs.append(blk[:, 0:LANES] / den)
        ya = jnp.concatenate(pairs, axis=1)
        yan_ref[r0:r0 + BLK, :] = _rms(ya, ga_ref[...]).astype(_BF16)

    def conv_piece(piece):
        half = piece % 2
        cols = slice(half * MXU_TILE, (half + 1) * MXU_TILE)
        w_cols = lambda base: w_ref[:, base + half * MXU_TILE:base + (half + 1) * MXU_TILE]
        if piece < 2:
            gatec_ref[:, cols] = _dot(u, w_cols(CONV_WIDTH))
        elif piece < 4:
            zbuf[HALO:HALO + ts, cols] = gatec_ref[:, cols] * _dot(u, w_cols(2 * CONV_WIDTH))
        else:
            gateb_ref[:, cols] = _dot(u, w_cols(0))

    piece_after_block = {0: 1, 1: 2, 3: 3, 5: 4, 6: 5}
    conv_piece(0)
    for j in range(LOOKAHEAD):
        scores_stage(j)
    for j in range(nblk):
        if j + LOOKAHEAD < nblk:
            scores_stage(j + LOOKAHEAD)
        if j in piece_after_block:
            conv_piece(piece_after_block[j])
        values_stage(j)

    tap = lambda k: cw_ref[:, k * CONV_WIDTH:(k + 1) * CONV_WIDTH]
    conv = (tap(0) * zbuf[HALO - 2:HALO - 2 + ts, :]
            + tap(1) * zbuf[HALO - 1:HALO - 1 + ts, :]
            + tap(2) * zbuf[HALO:HALO + ts, :])
    yc = _rms(gateb_ref[...] * conv, gc_ref[...]).astype(_BF16)
    zbuf[0:HALO, :] = zbuf[ts:ts + HALO, :]

    h = x_ref[...] + _dot(jnp.concatenate([yc, yan_ref[...]], axis=1), wo_ref[...])
    _store_interleaved(o_ref, h)


def _mix_attn(tab, sinks, x2, g_mix, w_in, conv_w, gq, gk, g_conv, hm, bkt, g_attn, w_out, w_up, w_down,
              tiles_per_seq):
    n = x2.shape[0]
    ts = ROW_TILE
    steps = n // ts
    assert ts // BLK >= 8 and CONV_WIDTH == 2 * MXU_TILE
    smem = pl.BlockSpec(memory_space=pltpu.SMEM)

    def slice_spec(w):
        rows, cols = w.shape
        assert rows % (steps * 2 * SUBLANES) == 0
        return pl.BlockSpec((rows // steps, cols), lambda i: (i, 0))

    cast_weights = (w_up, w_down)
    return pl.pallas_call(
        functools.partial(_mix_attn_kernel, tiles_per_seq),
        grid=(steps,),
        in_specs=[
            smem,
            smem,
            pl.BlockSpec((ts, D_MODEL), lambda i: (i, 0)),
            _const_spec((1, D_MODEL)),
            _const_spec((D_MODEL, IN_WIDTH)),
            _const_spec((1, 3 * CONV_WIDTH)),
            _const_spec((1, HEAD_DIM)),
            _const_spec((1, HEAD_DIM)),
            _const_spec((1, CONV_WIDTH)),
            _const_spec((ATTN_WIDTH, ATTN_WIDTH)),
            _const_spec((BLK, 2 * BLK)),
            _const_spec((1, ATTN_WIDTH)),
            _const_spec((CONV_WIDTH + ATTN_WIDTH, D_MODEL)),
        ] + [slice_spec(w) for w in cast_weights],
        out_specs=[pl.BlockSpec((D_MODEL // LANES, ts, LANES), lambda i: (0, i, 0))]
        + [slice_spec(w) for w in cast_weights],
        out_shape=[jax.ShapeDtypeStruct((D_MODEL // LANES, n, LANES), _F32)]
        + [jax.ShapeDtypeStruct(w.shape, _BF16) for w in cast_weights],
        scratch_shapes=[
            pltpu.VMEM((2, N_HEADS, BLK, 2 * BLK), _F32),
            pltpu.VMEM((D_MODEL, IN_WIDTH), _BF16),
            pltpu.VMEM((CONV_WIDTH + ATTN_WIDTH, D_MODEL), _BF16),
            pltpu.VMEM((1, ATTN_WIDTH), _F32),
            pltpu.VMEM((1, KV_WIDTH), _F32),
            pltpu.VMEM((ts, ATTN_WIDTH), _BF16),
            pltpu.VMEM((ts, KX_WIDTH), _BF16),
            pltpu.VMEM((ts, VX_WIDTH), _BF16),
            pltpu.VMEM((2 * BLK, KX_WIDTH), _BF16),
            pltpu.VMEM((2 * BLK, VX_WIDTH), _BF16),
            pltpu.VMEM((ts, CONV_WIDTH), _F32),
            pltpu.VMEM((ts, CONV_WIDTH), _F32),
            pltpu.VMEM((ts + HALO, CONV_WIDTH), _F32),
            pltpu.VMEM((ts, ATTN_WIDTH), _BF16),
            pltpu.VMEM((LOOKAHEAD + 1, 2 * N_KV_HEADS, 2 * BLK, 2 * BLK), _BF16),
            pltpu.VMEM((LOOKAHEAD + 1, 2 * N_KV_HEADS, BLK, LANES), _F32),
        ],
        compiler_params=pltpu.CompilerParams(
            dimension_semantics=("arbitrary",), vmem_limit_bytes=VMEM_LIMIT_BYTES),
        name="mix_attn",
    )(tab, sinks, x2, g_mix, w_in, conv_w, gq, gk, g_conv, hm, bkt, g_attn, w_out, *cast_weights)


def _ffn_kernel(tiles_per_seq, h_ref, g_ref, wu_ref, cw_ref, cb_ref, wd_ref, o_ref,
                u_ref, ubuf, carry_ref, act_ref, res_ref):
    ts = o_ref.shape[0]
    i = pl.program_id(0)
    g2 = 2 * SUBLANES

    @pl.when(i % tiles_per_seq == 0)
    def _():
        carry_ref[...] = jnp.zeros(carry_ref.shape, _F32)

    h = jnp.concatenate([h_ref[l] for l in range(D_MODEL // LANES)], axis=1)
    first_sublane = lax.broadcasted_iota(jnp.int32, (SUBLANES, FF_CHUNK), 0) == 0
    head_rows = [slice(p * ts // HEAD_PIECES, (p + 1) * ts // HEAD_PIECES) for p in range(HEAD_PIECES)]

    def up_half(c, part, r):
        buf = ubuf.at[(2 * c + part) % ubuf.shape[0]]
        cols = slice(part * D_FF + c * FF_CHUNK, part * D_FF + (c + 1) * FF_CHUNK)
        buf[g2 + r.start:g2 + r.stop, :] = _dot(u_ref[r, :], wu_ref[:, cols])

    def conv_half(c, part):
        slot = 2 * c + part
        buf = ubuf.at[slot % ubuf.shape[0]]
        cols = slice(part * D_FF + c * FF_CHUNK, part * D_FF + (c + 1) * FF_CHUNK)
        for k in range(2):
            cur = pltpu.roll(buf[ts + k * SUBLANES:ts + (k + 1) * SUBLANES, :], 1, axis=0)
            prev = pltpu.roll(carry_ref[slot, k * SUBLANES:(k + 1) * SUBLANES, :], 1, axis=0)
            buf[k * SUBLANES:(k + 1) * SUBLANES, :] = jnp.where(first_sublane, prev, cur)
        tap = lambda k: cw_ref[:, k * 2 * D_FF + cols.start:k * 2 * D_FF + cols.stop]
        a = (tap(0) * buf[0:ts, :]
             + tap(1) * buf[SUBLANES:SUBLANES + ts, :]
             + tap(2) * buf[g2:g2 + ts, :] + cb_ref[:, cols])
        carry_ref[slot] = buf[ts:ts + g2, :]
        return a

    for r in head_rows:
        u_ref[r, :] = _rms(h[r, :], g_ref[...]).astype(_BF16)
    for r in head_rows:
        up_half(0, 0, r)
        up_half(0, 1, r)

    for c in range(N_FF_CHUNKS):
        if c > 0:
            up_half(c, 0, slice(0, ts))
        gate = conv_half(c, 0)
        if c > 0:
            up_half(c, 1, slice(0, ts))
        val = conv_half(c, 1)
        act_ref[:, c * FF_CHUNK:(c + 1) * FF_CHUNK] = (gate / (1.0 + jnp.exp(-gate)) * val).astype(_BF16)

    res = h + _dot(act_ref[...], wd_ref[...])
    for l in range(D_MODEL // LANES):
        res_ref[l] = res[:, l * LANES:(l + 1) * LANES]
    _load_deinterleaved(res_ref, o_ref)


def _ffn(h, g_ffn, w_up, conv_w, conv_b, w_down, tiles_per_seq):
    n = h.shape[1]
    ts = ROW_TILE
    g2 = 2 * SUBLANES
    return pl.pallas_call(
        functools.partial(_ffn_kernel, tiles_per_seq),
        grid=(n // ts,),
        in_specs=[
            pl.BlockSpec((D_MODEL // LANES, ts, LANES), lambda i: (0, i, 0)),
            _const_spec((1, D_MODEL)),
            _const_spec((D_MODEL, 2 * D_FF)),
            _const_spec((1, 3 * 2 * D_FF)),
            _const_spec((1, 2 * D_FF)),
            _const_spec((D_FF, D_MODEL)),
        ],
        out_specs=pl.BlockSpec((ts, D_MODEL), lambda i: (i, 0)),
        out_shape=jax.ShapeDtypeStruct((n, D_MODEL), _F32),
        scratch_shapes=[
            pltpu.VMEM((ts, D_MODEL), _BF16),
            pltpu.VMEM((4, ts + g2, FF_CHUNK), _F32),
            pltpu.VMEM((2 * N_FF_CHUNKS, g2, FF_CHUNK), _F32),
            pltpu.VMEM((ts, D_FF), _BF16),
            pltpu.VMEM((D_MODEL // LANES, ts, LANES), _F32),
        ],
        compiler_params=pltpu.CompilerParams(
            dimension_semantics=("arbitrary",), vmem_limit_bytes=VMEM_LIMIT_BYTES),
        name="ffn",
    )(h, g_ffn, w_up, conv_w, conv_b, w_down)


def kernel(x, norm_mix_g, w_in, conv_w, q_norm_g, k_norm_g, rel_bias_table, sinks, out_norm_conv_g,
           out_norm_attn_g, w_out, norm_ffn_g, w_up, ffn_conv_w, ffn_conv_b, w_down):
    bsz, seq, d = x.shape
    assert d == D_MODEL and seq % ROW_TILE == 0 and norm_mix_g.shape[0] == 1
    tiles_per_seq = seq // ROW_TILE
    n = bsz * seq
    x2 = x.reshape(n, d)
    bkt = jnp.asarray(_bucket_map())
    hm = _head_mean_matrix()

    h, w_up_b, w_down_b = _mix_attn(
        rel_bias_table.T, sinks, x2, norm_mix_g, w_in[0], conv_w.reshape(1, -1), q_norm_g, k_norm_g,
        out_norm_conv_g, hm, bkt, out_norm_attn_g, w_out[0],
        w_up[0], w_down[0], tiles_per_seq)
    out = _ffn(h, norm_ffn_g, w_up_b, ffn_conv_w.reshape(1, -1), ffn_conv_b, w_down_b, tiles_per_seq)
    return out.reshape(bsz, seq, d)
```

```python
import functools
import math

import jax
import jax.numpy as jnp
import numpy as np
from jax import lax
from jax.experimental import pallas as pl
from jax.experimental.pallas import tpu as pltpu

D_MODEL = 1024
CONV_WIDTH = 512
HEAD_DIM = 64
N_HEADS = 8
N_KV_HEADS = 2
GQA_GROUP = N_HEADS // N_KV_HEADS
ATTN_WIDTH = N_HEADS * HEAD_DIM
KV_WIDTH = N_KV_HEADS * HEAD_DIM
WINDOW = 128
BLK = 128
NUM_BUCKETS = 32
MAX_DISTANCE = 128
MAX_EXACT = NUM_BUCKETS // 2
D_FF = 2816
EPS = 1e-6
NEG_INF = -1e30
LOG2E = math.log2(math.e)
IN_WIDTH = 3 * CONV_WIDTH + ATTN_WIDTH + 2 * KV_WIDTH

LANES = 128
SUBLANES = 8
MXU_TILE = 256
PAIR = 2 * HEAD_DIM
assert PAIR == LANES and KV_WIDTH == LANES and GQA_GROUP == 4
KX_WIDTH = 2 * N_KV_HEADS * LANES
VX_WIDTH = 2 * KX_WIDTH

ROW_TILE = 1024
FF_CHUNK = MXU_TILE
N_FF_CHUNKS = D_FF // FF_CHUNK
HEAD_PIECES = 4
HALO = 8
LOOKAHEAD = 1
VMEM_LIMIT_BYTES = 60 * 1024 * 1024

_BF16 = jnp.bfloat16
_F32 = jnp.float32


def _dot(a, b):
    return jnp.dot(a, b, preferred_element_type=_F32)


def _dot_nt(a, b):
    return lax.dot_general(a, b, (((1,), (1,)), ((), ())), preferred_element_type=_F32)


def _rms(xf, g):
    return xf * lax.rsqrt(jnp.mean(xf * xf, axis=-1, keepdims=True) + EPS) * g


def _bucket_map():
    q = np.arange(BLK, dtype=np.int32)[:, None]
    j = np.arange(2 * BLK, dtype=np.int32)[None, :]
    d = q + BLK - j
    n = np.maximum(d, 0)
    nf = np.maximum(n, 1).astype(np.float32)
    large = MAX_EXACT + (np.log(nf / MAX_EXACT) / math.log(MAX_DISTANCE / MAX_EXACT)
                         * (NUM_BUCKETS - MAX_EXACT)).astype(np.int32)
    large = np.minimum(large, NUM_BUCKETS - 1)
    bucket = np.where(n < MAX_EXACT, n, large).astype(np.int32)
    within = (d >= 0) & (d < WINDOW)
    return np.where(within, bucket, -1).astype(np.int32)


def _head_mean_matrix():
    i = np.arange(ATTN_WIDTH)
    m = (i[:, None] // HEAD_DIM == i[None, :] // HEAD_DIM).astype(np.float32) / HEAD_DIM
    return jnp.asarray(m, dtype=_BF16)


def _const_spec(shape):
    return pl.BlockSpec(shape, lambda i: (0,) * len(shape), pipeline_mode=pl.Buffered(1))


def _store_interleaved(slab_ref, val):
    ts = val.shape[0]
    seg = ts // SUBLANES
    for l in range(D_MODEL // LANES):
        for s in range(SUBLANES):
            slab_ref[l, pl.ds(s, seg, stride=SUBLANES), :] = val[s * seg:(s + 1) * seg, l * LANES:(l + 1) * LANES]


def _load_deinterleaved(slab_ref, out_ref):
    ts = out_ref.shape[0]
    seg = ts // SUBLANES
    for l in range(D_MODEL // LANES):
        for s in range(SUBLANES):
            out_ref[s * seg:(s + 1) * seg, l * LANES:(l + 1) * LANES] = slab_ref[l, pl.ds(s, seg, stride=SUBLANES), :]


def _lo_hi_layout(a, lo):
    ar = pltpu.roll(a, HEAD_DIM, axis=1)
    zero = jnp.zeros_like(a)
    return [jnp.where(lo, a, zero), jnp.where(lo, zero, ar), jnp.where(lo, ar, zero), jnp.where(lo, zero, a)]


def _mix_attn_kernel(tiles_per_seq, tab_ref, sink_ref, x_ref, g_ref, w_f32, cw_ref, gq_head, gk_head, gc_ref,
                     hm_ref, bkt_ref, ga_ref, wo_f32, wu_f32, wd_f32,
                     o_ref, wu_bf16, wd_bf16,
                     bias_ref, w_ref, wo_ref, gq_ref, gk_ref, q_ref, kx_ref, vx_ref, kb0_ref, vb0_ref, gatec_ref, gateb_ref, zbuf,
                     yan_ref, p_ref, st_ref):
    ts = x_ref.shape[0]
    nblk = ts // BLK
    i = pl.program_id(0)
    seq_start = (i % tiles_per_seq == 0)

    wu_bf16[...] = wu_f32[...].astype(_BF16)
    wd_bf16[...] = wd_f32[...].astype(_BF16)

    @pl.when(i == 0)
    def _():
        bkt = bkt_ref[...]
        col = lax.broadcasted_iota(jnp.int32, (BLK, 2 * BLK), 1)

        def head_body(h, carry):
            b = jnp.full((BLK, 2 * BLK), NEG_INF, _F32)
            for t in range(NUM_BUCKETS):
                b = jnp.where(bkt == t, tab_ref[h, t] * LOG2E, b)
            bias_ref[0, h] = b
            bias_ref[1, h] = jnp.where(col >= BLK, b, NEG_INF)
            return carry

        lax.fori_loop(0, N_HEADS, head_body, 0)
        w_ref[...] = w_f32[...].astype(_BF16)
        wo_ref[...] = wo_f32[...].astype(_BF16)
        for h in range(N_HEADS):
            gq_ref[:, h * HEAD_DIM:(h + 1) * HEAD_DIM] = gq_head[...]
        for h in range(N_KV_HEADS):
            gk_ref[:, h * HEAD_DIM:(h + 1) * HEAD_DIM] = gk_head[...]
        kx_ref[ts - BLK:ts, :] = jnp.zeros((BLK, KX_WIDTH), _BF16)
        vx_ref[ts - BLK:ts, :] = jnp.zeros((BLK, VX_WIDTH), _BF16)

    @pl.when(seq_start)
    def _():
        zbuf[0:HALO, :] = jnp.zeros((HALO, CONV_WIDTH), _F32)

    kb0_ref[0:BLK, :] = kx_ref[ts - BLK:ts, :]
    vb0_ref[0:BLK, :] = vx_ref[ts - BLK:ts, :]

    u = _rms(x_ref[...], g_ref[...]).astype(_BF16)
    c0 = 3 * CONV_WIDTH
    c1 = c0 + ATTN_WIDTH
    q = _dot(u, w_ref[:, c0:c0 + ATTN_WIDTH])
    kv = _dot(u, w_ref[:, c1:c1 + 2 * KV_WIDTH])
    k = kv[:, 0:KV_WIDTH]
    msq = _dot((q * q).astype(_BF16), hm_ref[...])
    msk = _dot((k * k).astype(_BF16), hm_ref[0:KV_WIDTH, 0:KV_WIDTH])

    q_ref[...] = (q * lax.rsqrt(msq + EPS) * gq_ref[...] * (HEAD_DIM ** -0.5 * LOG2E)).astype(_BF16)
    kn = k * lax.rsqrt(msk + EPS) * gk_ref[...]
    lo = lax.broadcasted_iota(jnp.int32, (ts, LANES), 1) < HEAD_DIM
    kx_ref[...] = jnp.concatenate(_lo_hi_layout(kn, lo), axis=1).astype(_BF16)
    ones = [jnp.where(lo, 1.0, 0.0).astype(_F32), jnp.where(lo, 0.0, 1.0).astype(_F32)]
    v_pieces = _lo_hi_layout(kv[:, KV_WIDTH:2 * KV_WIDTH], lo)
    vx_ref[...] = jnp.concatenate([piece for s, v in enumerate(v_pieces) for piece in (v, ones[s % 2])],
                                  axis=1).astype(_BF16)
    kb0_ref[BLK:2 * BLK, :] = kx_ref[0:BLK, :]
    vb0_ref[BLK:2 * BLK, :] = vx_ref[0:BLK, :]

    lane_lo = lax.broadcasted_iota(jnp.int32, (BLK, LANES), 1) < HEAD_DIM

    def band(tile_ref, band0_ref, j, cols):
        if j == 0:
            return band0_ref[:, cols]
        return tile_ref[(j - 1) * BLK:(j + 1) * BLK, cols]

    def scores_stage(j):
        r0 = j * BLK
        slot = j % (LOOKAHEAD + 1)
        first = seq_start.astype(jnp.int32) if j == 0 else 0
        for g in range(N_KV_HEADS):
            qq = q_ref[pl.ds(r0, BLK), g * 2 * LANES:(g + 1) * 2 * LANES]
            lhs = jnp.concatenate([qq[:, 0:LANES], qq[:, LANES:2 * LANES]], axis=0)
            sink_terms = [[None, None], [None, None]]
            for par in range(2):
                kb = band(kx_ref, kb0_ref, j, slice((2 * g + par) * LANES, (2 * g + par + 1) * LANES))
                logits = _dot_nt(lhs, kb)
                for pair in range(2):
                    h = g * GQA_GROUP + 2 * pair + par
                    lg = logits[pair * BLK:(pair + 1) * BLK, :] + bias_ref[first, h]
                    sink = sink_ref[0, h] * LOG2E
                    m = jnp.maximum(jnp.max(lg, axis=-1, keepdims=True), sink)
                    p_ref[slot, 2 * g + par, pair * BLK:(pair + 1) * BLK, :] = jnp.exp2(lg - m).astype(_BF16)
                    sink_terms[pair][par] = jnp.exp2(sink - m)
            for pair in range(2):
                st_ref[slot, 2 * g + pair] = jnp.where(lane_lo, sink_terms[pair][0], sink_terms[pair][1])

    def values_stage(j):
        r0 = j * BLK
        slot = j % (LOOKAHEAD + 1)
        pairs = []
        for g in range(N_KV_HEADS):
            c = 4 * g * LANES
            out = _dot(jnp.concatenate([p_ref[slot, 2 * g], p_ref[slot, 2 * g + 1]], axis=1),
                       jnp.concatenate([band(vx_ref, vb0_ref, j, slice(c, c + 2 * LANES)),
                                        band(vx_ref, vb0_ref, j, slice(c + 2 * LANES, c + 4 * LANES))], axis=0))
            for pair in range(2):
                blk = out[pair * BLK:(pair + 1) * BLK, :]
                den = blk[:, LANES:2 * LANES] + st_ref[slot, 2 * g + pair]
                pairs.append(blk[:, 0:LANES] / den)
        ya = jnp.concatenate(pairs, axis=1)
        yan_ref[r0:r0 + BLK, :] = _rms(ya, ga_ref[...]).astype(_BF16)

    def conv_piece(piece):
        half = piece % 2
        cols = slice(half * MXU_TILE, (half + 1) * MXU_TILE)
        w_cols = lambda base: w_ref[:, base + half * MXU_TILE:base + (half + 1) * MXU_TILE]
        if piece < 2:
            gatec_ref[:, cols] = _dot(u, w_cols(CONV_WIDTH))
        elif piece < 4:
            zbuf[HALO:HALO + ts, cols] = gatec_ref[:, cols] * _dot(u, w_cols(2 * CONV_WIDTH))
        else:
            gateb_ref[:, cols] = _dot(u, w_cols(0))

    piece_after_block = {0: 1, 1: 2, 3: 3, 5: 4, 6: 5}
    conv_piece(0)
    for j in range(LOOKAHEAD):
        scores_stage(j)
    for j in range(nblk):
        if j + LOOKAHEAD < nblk:
            scores_stage(j + LOOKAHEAD)
        if j in piece_after_block:
            conv_piece(piece_after_block[j])
        values_stage(j)

    tap = lambda k: cw_ref[:, k * CONV_WIDTH:(k + 1) * CONV_WIDTH]
    conv = (tap(0) * zbuf[HALO - 2:HALO - 2 + ts, :]
            + tap(1) * zbuf[HALO - 1:HALO - 1 + ts, :]
            + tap(2) * zbuf[HALO:HALO + ts, :])
    yc = _rms(gateb_ref[...] * conv, gc_ref[...]).astype(_BF16)
    zbuf[0:HALO, :] = zbuf[ts:ts + HALO, :]

    h = x_ref[...] + _dot(jnp.concatenate([yc, yan_ref[...]], axis=1), wo_ref[...])
    _store_interleaved(o_ref, h)


def _mix_attn(tab, sinks, x2, g_mix, w_in, conv_w, gq, gk, g_conv, hm, bkt, g_attn, w_out, w_up, w_down,
              tiles_per_seq):
    n = x2.shape[0]
    ts = ROW_TILE
    steps = n // ts
    assert ts // BLK >= 8 and CONV_WIDTH == 2 * MXU_TILE
    smem = pl.BlockSpec(memory_space=pltpu.SMEM)

    def slice_spec(w):
        rows, cols = w.shape
        assert rows % (steps * 2 * SUBLANES) == 0
        return pl.BlockSpec((rows // steps, cols), lambda i: (i, 0))

    cast_weights = (w_up, w_down)
    return pl.pallas_call(
        functools.partial(_mix_attn_kernel, tiles_per_seq),
        grid=(steps,),
        in_specs=[
            smem,
            smem,
            pl.BlockSpec((ts, D_MODEL), lambda i: (i, 0)),
            _const_spec((1, D_MODEL)),
            _const_spec((D_MODEL, IN_WIDTH)),
            _const_spec((1, 3 * CONV_WIDTH)),
            _const_spec((1, HEAD_DIM)),
            _const_spec((1, HEAD_DIM)),
            _const_spec((1, CONV_WIDTH)),
            _const_spec((ATTN_WIDTH, ATTN_WIDTH)),
            _const_spec((BLK, 2 * BLK)),
            _const_spec((1, ATTN_WIDTH)),
            _const_spec((CONV_WIDTH + ATTN_WIDTH, D_MODEL)),
        ] + [slice_spec(w) for w in cast_weights],
        out_specs=[pl.BlockSpec((D_MODEL // LANES, ts, LANES), lambda i: (0, i, 0))]
        + [slice_spec(w) for w in cast_weights],
        out_shape=[jax.ShapeDtypeStruct((D_MODEL // LANES, n, LANES), _F32)]
        + [jax.ShapeDtypeStruct(w.shape, _BF16) for w in cast_weights],
        scratch_shapes=[
            pltpu.VMEM((2, N_HEADS, BLK, 2 * BLK), _F32),
            pltpu.VMEM((D_MODEL, IN_WIDTH), _BF16),
            pltpu.VMEM((CONV_WIDTH + ATTN_WIDTH, D_MODEL), _BF16),
            pltpu.VMEM((1, ATTN_WIDTH), _F32),
            pltpu.VMEM((1, KV_WIDTH), _F32),
            pltpu.VMEM((ts, ATTN_WIDTH), _BF16),
            pltpu.VMEM((ts, KX_WIDTH), _BF16),
            pltpu.VMEM((ts, VX_WIDTH), _BF16),
            pltpu.VMEM((2 * BLK, KX_WIDTH), _BF16),
            pltpu.VMEM((2 * BLK, VX_WIDTH), _BF16),
            pltpu.VMEM((ts, CONV_WIDTH), _F32),
            pltpu.VMEM((ts, CONV_WIDTH), _F32),
            pltpu.VMEM((ts + HALO, CONV_WIDTH), _F32),
            pltpu.VMEM((ts, ATTN_WIDTH), _BF16),
            pltpu.VMEM((LOOKAHEAD + 1, 2 * N_KV_HEADS, 2 * BLK, 2 * BLK), _BF16),
            pltpu.VMEM((LOOKAHEAD + 1, 2 * N_KV_HEADS, BLK, LANES), _F32),
        ],
        compiler_params=pltpu.CompilerParams(
            dimension_semantics=("arbitrary",), vmem_limit_bytes=VMEM_LIMIT_BYTES),
        name="mix_attn",
    )(tab, sinks, x2, g_mix, w_in, conv_w, gq, gk, g_conv, hm, bkt, g_attn, w_out, *cast_weights)


def _ffn_kernel(tiles_per_seq, h_ref, g_ref, wu_ref, cw_ref, cb_ref, wd_ref, o_ref,
                u_ref, ubuf, carry_ref, act_ref, res_ref):
    ts = o_ref.shape[0]
    i = pl.program_id(0)
    g2 = 2 * SUBLANES

    @pl.when(i % tiles_per_seq == 0)
    def _():
        carry_ref[...] = jnp.zeros(carry_ref.shape, _F32)

    h = jnp.concatenate([h_ref[l] for l in range(D_MODEL // LANES)], axis=1)
    first_sublane = lax.broadcasted_iota(jnp.int32, (SUBLANES, FF_CHUNK), 0) == 0
    head_rows = [slice(p * ts // HEAD_PIECES, (p + 1) * ts // HEAD_PIECES) for p in range(HEAD_PIECES)]

    def up_half(c, part, r):
        buf = ubuf.at[(2 * c + part) % ubuf.shape[0]]
        cols = slice(part * D_FF + c * FF_CHUNK, part * D_FF + (c + 1) * FF_CHUNK)
        buf[g2 + r.start:g2 + r.stop, :] = _dot(u_ref[r, :], wu_ref[:, cols])

    def conv_half(c, part):
        slot = 2 * c + part
        buf = ubuf.at[slot % ubuf.shape[0]]
        cols = slice(part * D_FF + c * FF_CHUNK, part * D_FF + (c + 1) * FF_CHUNK)
        for k in range(2):
            cur = pltpu.roll(buf[ts + k * SUBLANES:ts + (k + 1) * SUBLANES, :], 1, axis=0)
            prev = pltpu.roll(carry_ref[slot, k * SUBLANES:(k + 1) * SUBLANES, :], 1, axis=0)
            buf[k * SUBLANES:(k + 1) * SUBLANES, :] = jnp.where(first_sublane, prev, cur)
        tap = lambda k: cw_ref[:, k * 2 * D_FF + cols.start:k * 2 * D_FF + cols.stop]
        a = (tap(0) * buf[0:ts, :]
             + tap(1) * buf[SUBLANES:SUBLANES + ts, :]
             + tap(2) * buf[g2:g2 + ts, :] + cb_ref[:, cols])
        carry_ref[slot] = buf[ts:ts + g2, :]
        return a

    for r in head_rows:
        u_ref[r, :] = _rms(h[r, :], g_ref[...]).astype(_BF16)
    for r in head_rows:
        up_half(0, 0, r)
        up_half(0, 1, r)

    for c in range(N_FF_CHUNKS):
        if c > 0:
            up_half(c, 0, slice(0, ts))
        gate = conv_half(c, 0)
        if c > 0:
            up_half(c, 1, slice(0, ts))
        val = conv_half(c, 1)
        act_ref[:, c * FF_CHUNK:(c + 1) * FF_CHUNK] = (gate / (1.0 + jnp.exp(-gate)) * val).astype(_BF16)

    res = h + _dot(act_ref[...], wd_ref[...])
    for l in range(D_MODEL // LANES):
        res_ref[l] = res[:, l * LANES:(l + 1) * LANES]
    _load_deinterleaved(res_ref, o_ref)


def _ffn(h, g_ffn, w_up, conv_w, conv_b, w_down, tiles_per_seq):
    n = h.shape[1]
    ts = ROW_TILE
    g2 = 2 * SUBLANES
    return pl.pallas_call(
        functools.partial(_ffn_kernel, tiles_per_seq),
        grid=(n // ts,),
        in_specs=[
            pl.BlockSpec((D_MODEL // LANES, ts, LANES), lambda i: (0, i, 0)),
            _const_spec((1, D_MODEL)),
            _const_spec((D_MODEL, 2 * D_FF)),
            _const_spec((1, 3 * 2 * D_FF)),
            _const_spec((1, 2 * D_FF)),
            _const_spec((D_FF, D_MODEL)),
        ],
        out_specs=pl.BlockSpec((ts, D_MODEL), lambda i: (i, 0)),
        out_shape=jax.ShapeDtypeStruct((n, D_MODEL), _F32),
        scratch_shapes=[
            pltpu.VMEM((ts, D_MODEL), _BF16),
            pltpu.VMEM((4, ts + g2, FF_CHUNK), _F32),
            pltpu.VMEM((2 * N_FF_CHUNKS, g2, FF_CHUNK), _F32),
            pltpu.VMEM((ts, D_FF), _BF16),
            pltpu.VMEM((D_MODEL // LANES, ts, LANES), _F32),
        ],
        compiler_params=pltpu.CompilerParams(
            dimension_semantics=("arbitrary",), vmem_limit_bytes=VMEM_LIMIT_BYTES),
        name="ffn",
    )(h, g_ffn, w_up, conv_w, conv_b, w_down)


def kernel(x, norm_mix_g, w_in, conv_w, q_norm_g, k_norm_g, rel_bias_table, sinks, out_norm_conv_g,
           out_norm_attn_g, w_out, norm_ffn_g, w_up, ffn_conv_w, ffn_conv_b, w_down):
    bsz, seq, d = x.shape
    assert d == D_MODEL and seq % ROW_TILE == 0 and norm_mix_g.shape[0] == 1
    tiles_per_seq = seq // ROW_TILE
    n = bsz * seq
    x2 = x.reshape(n, d)
    bkt = jnp.asarray(_bucket_map())
    hm = _head_mean_matrix()

    h, w_up_b, w_down_b = _mix_attn(
        rel_bias_table.T, sinks, x2, norm_mix_g, w_in[0], conv_w.reshape(1, -1), q_norm_g, k_norm_g,
        out_norm_conv_g, hm, bkt, out_norm_attn_g, w_out[0],
        w_up[0], w_down[0], tiles_per_seq)
    out = _ffn(h, norm_ffn_g, w_up_b, ffn_conv_w.reshape(1, -1), ffn_conv_b, w_down_b, tiles_per_seq)
    return out.reshape(bsz, seq, d)
```

```python
import functools
import math

import jax
import jax.numpy as jnp
import numpy as np
from jax import lax
from jax.experimental import pallas as pl
from jax.experimental.pallas import tpu as pltpu

D_MODEL = 1024
CONV_WIDTH = 512
HEAD_DIM = 64
N_HEADS = 8
N_KV_HEADS = 2
GQA_GROUP = N_HEADS // N_KV_HEADS
ATTN_WIDTH = N_HEADS * HEAD_DIM
KV_WIDTH = N_KV_HEADS * HEAD_DIM
WINDOW = 128
BLK = 128
NUM_BUCKETS = 32
MAX_DISTANCE = 128
MAX_EXACT = NUM_BUCKETS // 2
D_FF = 2816
EPS = 1e-6
NEG_INF = -1e30
LOG2E = math.log2(math.e)
IN_WIDTH = 3 * CONV_WIDTH + ATTN_WIDTH + 2 * KV_WIDTH

LANES = 128
SUBLANES = 8
MXU_TILE = 256
PAIR = 2 * HEAD_DIM
assert PAIR == LANES and KV_WIDTH == LANES and GQA_GROUP == 4
KX_WIDTH = 2 * N_KV_HEADS * LANES
VX_WIDTH = 2 * KX_WIDTH

ROW_TILE = 1024
FF_CHUNK = MXU_TILE
N_FF_CHUNKS = D_FF // FF_CHUNK
HEAD_PIECES = 4
HALO = 8
LOOKAHEAD = 1
VMEM_LIMIT_BYTES = 60 * 1024 * 1024

_BF16 = jnp.bfloat16
_F32 = jnp.float32


def _dot(a, b):
    return jnp.dot(a, b, preferred_element_type=_F32)


def _dot_nt(a, b):
    return lax.dot_general(a, b, (((1,), (1,)), ((), ())), preferred_element_type=_F32)


def _rms(xf, g):
    return xf * lax.rsqrt(jnp.mean(xf * xf, axis=-1, keepdims=True) + EPS) * g


def _bucket_map():
    q = np.arange(BLK, dtype=np.int32)[:, None]
    j = np.arange(2 * BLK, dtype=np.int32)[None, :]
    d = q + BLK - j
    n = np.maximum(d, 0)
    nf = np.maximum(n, 1).astype(np.float32)
    large = MAX_EXACT + (np.log(nf / MAX_EXACT) / math.log(MAX_DISTANCE / MAX_EXACT)
                         * (NUM_BUCKETS - MAX_EXACT)).astype(np.int32)
    large = np.minimum(large, NUM_BUCKETS - 1)
    bucket = np.where(n < MAX_EXACT, n, large).astype(np.int32)
    within = (d >= 0) & (d < WINDOW)
    return np.where(within, bucket, -1).astype(np.int32)


def _head_mean_matrix():
    i = np.arange(ATTN_WIDTH)
    m = (i[:, None] // HEAD_DIM == i[None, :] // HEAD_DIM).astype(np.float32) / HEAD_DIM
    return jnp.asarray(m, dtype=_BF16)


def _const_spec(shape):
    return pl.BlockSpec(shape, lambda i: (0,) * len(shape), pipeline_mode=pl.Buffered(1))


def _store_interleaved(slab_ref, val):
    ts = val.shape[0]
    seg = ts // SUBLANES
    for l in range(D_MODEL // LANES):
        for s in range(SUBLANES):
            slab_ref[l, pl.ds(s, seg, stride=SUBLANES), :] = val[s * seg:(s + 1) * seg, l * LANES:(l + 1) * LANES]


def _load_deinterleaved(slab_ref, out_ref):
    ts = out_ref.shape[0]
    seg = ts // SUBLANES
    for l in range(D_MODEL // LANES):
        for s in range(SUBLANES):
            out_ref[s * seg:(s + 1) * seg, l * LANES:(l + 1) * LANES] = slab_ref[l, pl.ds(s, seg, stride=SUBLANES), :]


def _lo_hi_layout(a, lo):
    ar = pltpu.roll(a, HEAD_DIM, axis=1)
    zero = jnp.zeros_like(a)
    return [jnp.where(lo, a, zero), jnp.where(lo, zero, ar), jnp.where(lo, ar, zero), jnp.where(lo, zero, a)]


def _mix_attn_kernel(tiles_per_seq, tab_ref, sink_ref, x_ref, g_ref, w_f32, cw_ref, gq_head, gk_head, gc_ref,
                     hm_ref, bkt_ref, ga_ref, wo_f32, wu_f32, wd_f32,
                     o_ref, wu_bf16, wd_bf16,
                     bias_ref, w_ref, wo_ref, gq_ref, gk_ref, q_ref, kx_ref, vx_ref, kb0_ref, vb0_ref, gatec_ref, gateb_ref, zbuf,
                     yan_ref, p_ref, st_ref):
    ts = x_ref.shape[0]
    nblk = ts // BLK
    i = pl.program_id(0)
    seq_start = (i % tiles_per_seq == 0)

    wu_bf16[...] = wu_f32[...].astype(_BF16)
    wd_bf16[...] = wd_f32[...].astype(_BF16)

    @pl.when(i == 0)
    def _():
        bkt = bkt_ref[...]
        col = lax.broadcasted_iota(jnp.int32, (BLK, 2 * BLK), 1)

        def head_body(h, carry):
            b = jnp.full((BLK, 2 * BLK), NEG_INF, _F32)
            for t in range(NUM_BUCKETS):
                b = jnp.where(bkt == t, tab_ref[h, t] * LOG2E, b)
            bias_ref[0, h] = b
            bias_ref[1, h] = jnp.where(col >= BLK, b, NEG_INF)
            return carry

        lax.fori_loop(0, N_HEADS, head_body, 0)
        w_ref[...] = w_f32[...].astype(_BF16)
        wo_ref[...] = wo_f32[...].astype(_BF16)
        for h in range(N_HEADS):
            gq_ref[:, h * HEAD_DIM:(h + 1) * HEAD_DIM] = gq_head[...]
        for h in range(N_KV_HEADS):
            gk_ref[:, h * HEAD_DIM:(h + 1) * HEAD_DIM] = gk_head[...]
        kx_ref[ts - BLK:ts, :] = jnp.zeros((BLK, KX_WIDTH), _BF16)
        vx_ref[ts - BLK:ts, :] = jnp.zeros((BLK, VX_WIDTH), _BF16)

    @pl.when(seq_start)
    def _():
        zbuf[0:HALO, :] = jnp.zeros((HALO, CONV_WIDTH), _F32)

    kb0_ref[0:BLK, :] = kx_ref[ts - BLK:ts, :]
    vb0_ref[0:BLK, :] = vx_ref[ts - BLK:ts, :]

    u = _rms(x_ref[...], g_ref[...]).astype(_BF16)
    c0 = 3 * CONV_WIDTH
    c1 = c0 + ATTN_WIDTH
    qkv = _dot(u, w_ref[:, c0:c1 + 2 * KV_WIDTH])
    q = qkv[:, 0:ATTN_WIDTH]
    kv = qkv[:, ATTN_WIDTH:ATTN_WIDTH + 2 * KV_WIDTH]
    k = kv[:, 0:KV_WIDTH]
    msq = _dot((q * q).astype(_BF16), hm_ref[...])
    msk = _dot((k * k).astype(_BF16), hm_ref[0:KV_WIDTH, 0:KV_WIDTH])

    q_ref[...] = (q * lax.rsqrt(msq + EPS) * gq_ref[...] * (HEAD_DIM ** -0.5 * LOG2E)).astype(_BF16)
    kn = k * lax.rsqrt(msk + EPS) * gk_ref[...]
    lo = lax.broadcasted_iota(jnp.int32, (ts, LANES), 1) < HEAD_DIM
    kx_ref[...] = jnp.concatenate(_lo_hi_layout(kn, lo), axis=1).astype(_BF16)
    ones = [jnp.where(lo, 1.0, 0.0).astype(_F32), jnp.where(lo, 0.0, 1.0).astype(_F32)]
    v_pieces = _lo_hi_layout(kv[:, KV_WIDTH:2 * KV_WIDTH], lo)
    vx_ref[...] = jnp.concatenate([piece for s, v in enumerate(v_pieces) for piece in (v, ones[s % 2])],
                                  axis=1).astype(_BF16)
    kb0_ref[BLK:2 * BLK, :] = kx_ref[0:BLK, :]
    vb0_ref[BLK:2 * BLK, :] = vx_ref[0:BLK, :]

    lane_lo = lax.broadcasted_iota(jnp.int32, (BLK, LANES), 1) < HEAD_DIM

    def band(tile_ref, band0_ref, j, cols):
        if j == 0:
            return band0_ref[:, cols]
        return tile_ref[(j - 1) * BLK:(j + 1) * BLK, cols]

    def scores_stage(j):
        r0 = j * BLK
        slot = j % (LOOKAHEAD + 1)
        first = seq_start.astype(jnp.int32) if j == 0 else 0
        for g in range(N_KV_HEADS):
            qq = q_ref[pl.ds(r0, BLK), g * 2 * LANES:(g + 1) * 2 * LANES]
            lhs = jnp.concatenate([qq[:, 0:LANES], qq[:, LANES:2 * LANES]], axis=0)
            sink_terms = [[None, None], [None, None]]
            for par in range(2):
                kb = band(kx_ref, kb0_ref, j, slice((2 * g + par) * LANES, (2 * g + par + 1) * LANES))
                logits = _dot_nt(lhs, kb)
                for pair in range(2):
                    h = g * GQA_GROUP + 2 * pair + par
                    lg = logits[pair * BLK:(pair + 1) * BLK, :] + bias_ref[first, h]
                    sink = sink_ref[0, h] * LOG2E
                    m = jnp.maximum(jnp.max(lg, axis=-1, keepdims=True), sink)
                    p_ref[slot, 2 * g + par, pair * BLK:(pair + 1) * BLK, :] = jnp.exp2(lg - m).astype(_BF16)
                    sink_terms[pair][par] = jnp.exp2(sink - m)
            for pair in range(2):
                st_ref[slot, 2 * g + pair] = jnp.where(lane_lo, sink_terms[pair][0], sink_terms[pair][1])

    def values_stage(j):
        r0 = j * BLK
        slot = j % (LOOKAHEAD + 1)
        pairs = []
        for g in range(N_KV_HEADS):
            c = 4 * g * LANES
            out = _dot(jnp.concatenate([p_ref[slot, 2 * g], p_ref[slot, 2 * g + 1]], axis=1),
                       jnp.concatenate([band(vx_ref, vb0_ref, j, slice(c, c + 2 * LANES)),
                                        band(vx_ref, vb0_ref, j, slice(c + 2 * LANES, c + 4 * LANES))], axis=0))
            for pair in range(2):
                blk = out[pair * BLK:(pair + 1) * BLK, :]
                den = blk[:, LANES:2 * LANES] + st_ref[slot, 2 * g + pair]
                pairs.append(blk[:, 0:LANES] / den)
        ya = jnp.concatenate(pairs, axis=1)
        yan_ref[r0:r0 + BLK, :] = _rms(ya, ga_ref[...]).astype(_BF16)

    def conv_piece(piece):
        half = piece % 2
        cols = slice(half * MXU_TILE, (half + 1) * MXU_TILE)
        w_cols = lambda base: w_ref[:, base + half * MXU_TILE:base + (half + 1) * MXU_TILE]
        if piece < 2:
            gatec_ref[:, cols] = _dot(u, w_cols(CONV_WIDTH))
        elif piece < 4:
            zbuf[HALO:HALO + ts, cols] = gatec_ref[:, cols] * _dot(u, w_cols(2 * CONV_WIDTH))
        else:
            gateb_ref[:, cols] = _dot(u, w_cols(0))

    piece_after_block = {0: 1, 1: 2, 3: 3, 5: 4, 6: 5}
    conv_piece(0)
    for j in range(LOOKAHEAD):
        scores_stage(j)
    for j in range(nblk):
        if j + LOOKAHEAD < nblk:
            scores_stage(j + LOOKAHEAD)
        if j in piece_after_block:
            conv_piece(piece_after_block[j])
        values_stage(j)

    tap = lambda k: cw_ref[:, k * CONV_WIDTH:(k + 1) * CONV_WIDTH]
    conv = (tap(0) * zbuf[HALO - 2:HALO - 2 + ts, :]
            + tap(1) * zbuf[HALO - 1:HALO - 1 + ts, :]
            + tap(2) * zbuf[HALO:HALO + ts, :])
    yc = _rms(gateb_ref[...] * conv, gc_ref[...]).astype(_BF16)
    zbuf[0:HALO, :] = zbuf[ts:ts + HALO, :]

    h = x_ref[...] + _dot(jnp.concatenate([yc, yan_ref[...]], axis=1), wo_ref[...])
    _store_interleaved(o_ref, h)


def _mix_attn(tab, sinks, x2, g_mix, w_in, conv_w, gq, gk, g_conv, hm, bkt, g_attn, w_out, w_up, w_down,
              tiles_per_seq):
    n = x2.shape[0]
    ts = ROW_TILE
    steps = n // ts
    assert ts // BLK >= 8 and CONV_WIDTH == 2 * MXU_TILE
    smem = pl.BlockSpec(memory_space=pltpu.SMEM)

    def slice_spec(w):
        rows, cols = w.shape
        assert rows % (steps * 2 * SUBLANES) == 0
        return pl.BlockSpec((rows // steps, cols), lambda i: (i, 0))

    cast_weights = (w_up, w_down)
    return pl.pallas_call(
        functools.partial(_mix_attn_kernel, tiles_per_seq),
        grid=(steps,),
        in_specs=[
            smem,
            smem,
            pl.BlockSpec((ts, D_MODEL), lambda i: (i, 0)),
            _const_spec((1, D_MODEL)),
            _const_spec((D_MODEL, IN_WIDTH)),
            _const_spec((1, 3 * CONV_WIDTH)),
            _const_spec((1, HEAD_DIM)),
            _const_spec((1, HEAD_DIM)),
            _const_spec((1, CONV_WIDTH)),
            _const_spec((ATTN_WIDTH, ATTN_WIDTH)),
            _const_spec((BLK, 2 * BLK)),
            _const_spec((1, ATTN_WIDTH)),
            _const_spec((CONV_WIDTH + ATTN_WIDTH, D_MODEL)),
        ] + [slice_spec(w) for w in cast_weights],
        out_specs=[pl.BlockSpec((D_MODEL // LANES, ts, LANES), lambda i: (0, i, 0))]
        + [slice_spec(w) for w in cast_weights],
        out_shape=[jax.ShapeDtypeStruct((D_MODEL // LANES, n, LANES), _F32)]
        + [jax.ShapeDtypeStruct(w.shape, _BF16) for w in cast_weights],
        scratch_shapes=[
            pltpu.VMEM((2, N_HEADS, BLK, 2 * BLK), _F32),
            pltpu.VMEM((D_MODEL, IN_WIDTH), _BF16),
            pltpu.VMEM((CONV_WIDTH + ATTN_WIDTH, D_MODEL), _BF16),
            pltpu.VMEM((1, ATTN_WIDTH), _F32),
            pltpu.VMEM((1, KV_WIDTH), _F32),
            pltpu.VMEM((ts, ATTN_WIDTH), _BF16),
            pltpu.VMEM((ts, KX_WIDTH), _BF16),
            pltpu.VMEM((ts, VX_WIDTH), _BF16),
            pltpu.VMEM((2 * BLK, KX_WIDTH), _BF16),
            pltpu.VMEM((2 * BLK, VX_WIDTH), _BF16),
            pltpu.VMEM((ts, CONV_WIDTH), _F32),
            pltpu.VMEM((ts, CONV_WIDTH), _F32),
            pltpu.VMEM((ts + HALO, CONV_WIDTH), _F32),
            pltpu.VMEM((ts, ATTN_WIDTH), _BF16),
            pltpu.VMEM((LOOKAHEAD + 1, 2 * N_KV_HEADS, 2 * BLK, 2 * BLK), _BF16),
            pltpu.VMEM((LOOKAHEAD + 1, 2 * N_KV_HEADS, BLK, LANES), _F32),
        ],
        compiler_params=pltpu.CompilerParams(
            dimension_semantics=("arbitrary",), vmem_limit_bytes=VMEM_LIMIT_BYTES),
        name="mix_attn",
    )(tab, sinks, x2, g_mix, w_in, conv_w, gq, gk, g_conv, hm, bkt, g_attn, w_out, *cast_weights)


def _ffn_kernel(tiles_per_seq, h_ref, g_ref, wu_ref, cw_ref, cb_ref, wd_ref, o_ref,
                u_ref, ubuf, carry_ref, act_ref, res_ref):
    ts = o_ref.shape[0]
    i = pl.program_id(0)
    g2 = 2 * SUBLANES

    @pl.when(i % tiles_per_seq == 0)
    def _():
        carry_ref[...] = jnp.zeros(carry_ref.shape, _F32)

    h = jnp.concatenate([h_ref[l] for l in range(D_MODEL // LANES)], axis=1)
    first_sublane = lax.broadcasted_iota(jnp.int32, (SUBLANES, FF_CHUNK), 0) == 0
    head_rows = [slice(p * ts // HEAD_PIECES, (p + 1) * ts // HEAD_PIECES) for p in range(HEAD_PIECES)]

    def up_half(c, part, r):
        buf = ubuf.at[(2 * c + part) % ubuf.shape[0]]
        cols = slice(part * D_FF + c * FF_CHUNK, part * D_FF + (c + 1) * FF_CHUNK)
        buf[g2 + r.start:g2 + r.stop, :] = _dot(u_ref[r, :], wu_ref[:, cols])

    def conv_half(c, part):
        slot = 2 * c + part
        buf = ubuf.at[slot % ubuf.shape[0]]
        cols = slice(part * D_FF + c * FF_CHUNK, part * D_FF + (c + 1) * FF_CHUNK)
        for k in range(2):
            cur = pltpu.roll(buf[ts + k * SUBLANES:ts + (k + 1) * SUBLANES, :], 1, axis=0)
            prev = pltpu.roll(carry_ref[slot, k * SUBLANES:(k + 1) * SUBLANES, :], 1, axis=0)
            buf[k * SUBLANES:(k + 1) * SUBLANES, :] = jnp.where(first_sublane, prev, cur)
        tap = lambda k: cw_ref[:, k * 2 * D_FF + cols.start:k * 2 * D_FF + cols.stop]
        a = (tap(0) * buf[0:ts, :]
             + tap(1) * buf[SUBLANES:SUBLANES + ts, :]
             + tap(2) * buf[g2:g2 + ts, :] + cb_ref[:, cols])
        carry_ref[slot] = buf[ts:ts + g2, :]
        return a

    for r in head_rows:
        u_ref[r, :] = _rms(h[r, :], g_ref[...]).astype(_BF16)
    for r in head_rows:
        up_half(0, 0, r)
        up_half(0, 1, r)

    for c in range(N_FF_CHUNKS):
        if c > 0:
            up_half(c, 0, slice(0, ts))
        gate = conv_half(c, 0)
        if c > 0:
            up_half(c, 1, slice(0, ts))
        val = conv_half(c, 1)
        act_ref[:, c * FF_CHUNK:(c + 1) * FF_CHUNK] = (gate / (1.0 + jnp.exp(-gate)) * val).astype(_BF16)

    res = h + _dot(act_ref[...], wd_ref[...])
    for l in range(D_MODEL // LANES):
        res_ref[l] = res[:, l * LANES:(l + 1) * LANES]
    _load_deinterleaved(res_ref, o_ref)


def _ffn(h, g_ffn, w_up, conv_w, conv_b, w_down, tiles_per_seq):
    n = h.shape[1]
    ts = ROW_TILE
    g2 = 2 * SUBLANES
    return pl.pallas_call(
        functools.partial(_ffn_kernel, tiles_per_seq),
        grid=(n // ts,),
        in_specs=[
            pl.BlockSpec((D_MODEL // LANES, ts, LANES), lambda i: (0, i, 0)),
            _const_spec((1, D_MODEL)),
            _const_spec((D_MODEL, 2 * D_FF)),
            _const_spec((1, 3 * 2 * D_FF)),
            _const_spec((1, 2 * D_FF)),
            _const_spec((D_FF, D_MODEL)),
        ],
        out_specs=pl.BlockSpec((ts, D_MODEL), lambda i: (i, 0)),
        out_shape=jax.ShapeDtypeStruct((n, D_MODEL), _F32),
        scratch_shapes=[
            pltpu.VMEM((ts, D_MODEL), _BF16),
            pltpu.VMEM((4, ts + g2, FF_CHUNK), _F32),
            pltpu.VMEM((2 * N_FF_CHUNKS, g2, FF_CHUNK), _F32),
            pltpu.VMEM((ts, D_FF), _BF16),
            pltpu.VMEM((D_MODEL // LANES, ts, LANES), _F32),
        ],
        compiler_params=pltpu.CompilerParams(
            dimension_semantics=("arbitrary",), vmem_limit_bytes=VMEM_LIMIT_BYTES),
        name="ffn",
    )(h, g_ffn, w_up, conv_w, conv_b, w_down)


def kernel(x, norm_mix_g, w_in, conv_w, q_norm_g, k_norm_g, rel_bias_table, sinks, out_norm_conv_g,
           out_norm_attn_g, w_out, norm_ffn_g, w_up, ffn_conv_w, ffn_conv_b, w_down):
    bsz, seq, d = x.shape
    assert d == D_MODEL and seq % ROW_TILE == 0 and norm_mix_g.shape[0] == 1
    tiles_per_seq = seq // ROW_TILE
    n = bsz * seq
    x2 = x.reshape(n, d)
    bkt = jnp.asarray(_bucket_map())
    hm = _head_mean_matrix()

    h, w_up_b, w_down_b = _mix_attn(
        rel_bias_table.T, sinks, x2, norm_mix_g, w_in[0], conv_w.reshape(1, -1), q_norm_g, k_norm_g,
        out_norm_conv_g, hm, bkt, out_norm_attn_g, w_out[0],
        w_up[0], w_down[0], tiles_per_seq)
    out = _ffn(h, norm_ffn_g, w_up_b, ffn_conv_w.reshape(1, -1), ffn_conv_b, w_down_b, tiles_per_seq)
    return out.reshape(bsz, seq, d)
```
